```python
import jax, jax.numpy as jnp
from jax import lax
import numpy as np

D_MODEL = 1024
BATCH = 4
SEQ = 4096
DEPTH = 2

EPS = 1e-6
CHUNK = 128
GMLP_GROUPS = 4
GMLP_HEAD = 64
GMLP_WIDTH = GMLP_GROUPS * GMLP_HEAD
FOX_HEADS = 12
HEAD_DIM = 64
FOX_WIDTH = FOX_HEADS * HEAD_DIM
Q_BLOCK = 128
EVEN_SPLITS = (GMLP_WIDTH, 2 * GMLP_WIDTH, 2 * GMLP_WIDTH + FOX_WIDTH,
               2 * GMLP_WIDTH + 2 * FOX_WIDTH, 2 * GMLP_WIDTH + 3 * FOX_WIDTH)
EVEN_PROJ = 2 * GMLP_WIDTH + 3 * FOX_WIDTH + FOX_HEADS
EVEN_MIX = GMLP_WIDTH + FOX_WIDTH
CONV_WIDTH = 512
CONV_K = 3
POOL_WINDOWS = (2, 4, 8, 16)
POOL_GROUPS = 4
POOL_GROUP = 128
POOL_WIDTH = POOL_GROUPS * POOL_GROUP
ODD_SPLITS = (CONV_WIDTH, 2 * CONV_WIDTH, 3 * CONV_WIDTH)
ODD_PROJ = 3 * CONV_WIDTH + POOL_WIDTH
ODD_MIX = CONV_WIDTH + POOL_WIDTH
N_GROUPS = 4
EXPERTS_PER_GROUP = 4
N_EXPERTS = N_GROUPS * EXPERTS_PER_GROUP
EXPERT_HIDDEN = 256
TOP_K_INNER = 2

kernel_name = "hybrid_gmlp_fox_conv_pool_hmoe"


def rms_norm(x, g):
    xf = x.astype(jnp.float32)
    y = xf * lax.rsqrt(jnp.mean(xf * xf, axis=-1, keepdims=True) + EPS)
    return (y * g.astype(jnp.float32)).astype(x.dtype)


def gmlp_mixer(u, v, w_s, b_s, g_v):
    bsz, s_len, _ = u.shape
    n_chunks = s_len // CHUNK
    u = jax.nn.gelu(u)
    v = rms_norm(jax.nn.gelu(v).reshape(bsz, s_len, GMLP_GROUPS, GMLP_HEAD), g_v)
    v = v.reshape(bsz, n_chunks, CHUNK, GMLP_GROUPS, GMLP_HEAD)
    w = w_s * jnp.tril(jnp.ones((CHUNK, CHUNK), dtype=w_s.dtype))
    s = jnp.einsum('gts,bcsgd->bctgd', w, v) + b_s.T[None, None, :, :, None]
    return u * s.reshape(bsz, s_len, GMLP_WIDTH)


def fox_attention(q, k, v, f_logit, g_q, g_k):
    bsz, s_len = q.shape[0], q.shape[1]
    q = rms_norm(q, g_q).transpose(0, 2, 1, 3)
    k = rms_norm(k, g_k).transpose(0, 2, 1, 3)
    v = v.transpose(0, 2, 1, 3)
    log_f = jax.nn.log_sigmoid(f_logit.astype(jnp.float32))
    cum_f = jnp.cumsum(log_f, axis=1).transpose(0, 2, 1)
    scale = HEAD_DIM ** -0.5
    outs = []
    for blk in range(s_len // Q_BLOCK):
        q0 = blk * Q_BLOCK
        q1 = q0 + Q_BLOCK
        logits = jnp.einsum('bhqd,bhkd->bhqk', q[:, :, q0:q1], k[:, :, :q1]).astype(jnp.float32) * scale
        logits = logits + cum_f[:, :, q0:q1, None] - cum_f[:, :, None, :q1]
        causal = jnp.arange(q1)[None, :] <= jnp.arange(q0, q1)[:, None]
        logits = jnp.where(causal, logits, -jnp.inf)
        p = jax.nn.softmax(logits, axis=-1).astype(v.dtype)
        outs.append(jnp.einsum('bhqk,bhkd->bhqd', p, v[:, :, :q1]))
    o = jnp.concatenate(outs, axis=2)
    return o.transpose(0, 2, 1, 3).reshape(bsz, s_len, FOX_WIDTH)


def even_mixer(xn, w_in, b_forget, w_s, b_s, g_v, g_q, g_k, w_out):
    bsz, s_len, _ = xn.shape
    h = xn @ w_in
    u, v, q, k, val, f = jnp.split(h, EVEN_SPLITS, axis=-1)
    a = gmlp_mixer(u, v, w_s, b_s, g_v)
    hs = (bsz, s_len, FOX_HEADS, HEAD_DIM)
    b = fox_attention(q.reshape(hs), k.reshape(hs), val.reshape(hs), f + b_forget, g_q, g_k)
    return jnp.concatenate([a, b], axis=-1) @ w_out


def short_conv(bg, cg, hc, conv_w):
    s_len = hc.shape[1]
    z = cg * hc
    zp = jnp.pad(z, ((0, 0), (CONV_K - 1, 0), (0, 0)))
    y = conv_w[0] * zp[:, 0:s_len]
    for tap in range(1, CONV_K):
        y = y + conv_w[tap] * zp[:, tap:tap + s_len]
    return bg * y


def pool_mixer(p, w_pool, pool_scale):
    bsz, s_len, _ = p.shape
    pf = p.astype(jnp.float32).reshape(bsz, s_len, POOL_GROUPS, POOL_GROUP)
    cs = jnp.concatenate([jnp.zeros_like(pf[:, :1]), jnp.cumsum(pf, axis=1)], axis=1)
    hi = jnp.arange(1, s_len + 1)
    groups = []
    for gi, win in enumerate(POOL_WINDOWS):
        lo = jnp.maximum(hi - win, 0)
        cnt = (hi - lo).astype(jnp.float32)
        mean = (cs[:, hi, gi] - cs[:, lo, gi]) / cnt[None, :, None]
        groups.append(mean - pf[:, :, gi])
    pooled = jnp.stack(groups, axis=2).astype(p.dtype)
    y = jnp.einsum('bsgc,gcd->bsgd', pooled, w_pool).reshape(bsz, s_len, POOL_WIDTH)
    return y * pool_scale


def odd_mixer(xn, w_in, conv_w, w_pool, pool_scale, w_out):
    h = xn @ w_in
    bg, cg, hc, p = jnp.split(h, ODD_SPLITS, axis=-1)
    c = short_conv(bg, cg, hc, conv_w)
    d = pool_mixer(p, w_pool, pool_scale)
    return jnp.concatenate([c, d], axis=-1) @ w_out


def hier_moe(xn, w_group, b_group, w_router, b_router, w_gate, w_up, w_down):
    bsz, s_len, d = xn.shape
    xt = xn.reshape(bsz * s_len, d)
    g_prob = jax.nn.softmax((xt @ w_group).astype(jnp.float32) + b_group.astype(jnp.float32), axis=-1)
    g_top_p, g_idx = lax.top_k(g_prob, 1)
    e_logits_all = jnp.einsum('td,dge->tge', xt, w_router).astype(jnp.float32) + b_router.astype(jnp.float32)
    g_sel = jax.nn.one_hot(g_idx[:, 0], N_GROUPS, dtype=jnp.float32)
    e_logits = jnp.einsum('tge,tg->te', e_logits_all, g_sel)
    e_prob = jax.nn.softmax(e_logits, axis=-1)
    e_top_p, e_idx = lax.top_k(e_prob, TOP_K_INNER)
    e_top_p = e_top_p / jnp.sum(e_top_p, axis=-1, keepdims=True)
    weights = g_top_p * e_top_p
    expert_id = g_idx * EXPERTS_PER_GROUP + e_idx
    combine = jnp.einsum('tk,tke->te', weights, jax.nn.one_hot(expert_id, N_EXPERTS, dtype=jnp.float32))
    hid = jax.nn.silu(jnp.einsum('td,edh->teh', xt, w_gate)) * jnp.einsum('td,edh->teh', xt, w_up)
    hid = hid * combine[:, :, None].astype(hid.dtype)
    y = jnp.einsum('teh,ehd->td', hid, w_down)
    return y.reshape(bsz, s_len, d)


def setup_inputs(seed: int = 0) -> dict:
    key = jax.random.key(seed)
    ks = iter(jax.random.split(key, 32))
    n_even = (DEPTH + 1) // 2
    n_odd = DEPTH // 2
    f32 = jnp.float32

    def nrm(shape, scale):
        return jax.random.normal(next(ks), shape, f32) * scale

    def gain(shape):
        return 1.0 + 0.05 * jax.random.normal(next(ks), shape, f32)

    return {
        "x": jax.random.normal(next(ks), (BATCH, SEQ, D_MODEL), f32),
        "ev_norm": gain((n_even, D_MODEL)),
        "ev_w_in": nrm((n_even, D_MODEL, EVEN_PROJ), D_MODEL ** -0.5),
        "ev_b_forget": jax.random.uniform(next(ks), (n_even, FOX_HEADS), f32, 1.0, 5.0),
        "ev_w_s": nrm((n_even, GMLP_GROUPS, CHUNK, CHUNK), CHUNK ** -0.5),
        "ev_b_s": gain((n_even, GMLP_GROUPS, CHUNK)),
        "ev_g_v": gain((n_even, GMLP_GROUPS, GMLP_HEAD)),
        "ev_g_q": gain((n_even, HEAD_DIM)),
        "ev_g_k": gain((n_even, HEAD_DIM)),
        "ev_w_out": nrm((n_even, EVEN_MIX, D_MODEL), EVEN_MIX ** -0.5),
        "od_norm": gain((n_odd, D_MODEL)),
        "od_w_in": nrm((n_odd, D_MODEL, ODD_PROJ), D_MODEL ** -0.5),
        "od_conv_w": nrm((n_odd, CONV_K, CONV_WIDTH), CONV_K ** -0.5),
        "od_w_pool": nrm((n_odd, POOL_GROUPS, POOL_GROUP, POOL_GROUP), POOL_GROUP ** -0.5),
        "od_pool_scale": gain((n_odd, POOL_WIDTH)),
        "od_w_out": nrm((n_odd, ODD_MIX, D_MODEL), ODD_MIX ** -0.5),
        "moe_norm": gain((DEPTH, D_MODEL)),
        "moe_w_group": nrm((DEPTH, D_MODEL, N_GROUPS), D_MODEL ** -0.5),
        "moe_b_group": nrm((DEPTH, N_GROUPS), 0.01),
        "moe_w_router": nrm((DEPTH, D_MODEL, N_GROUPS, EXPERTS_PER_GROUP), D_MODEL ** -0.5),
        "moe_b_router": nrm((DEPTH, N_GROUPS, EXPERTS_PER_GROUP), 0.01),
        "moe_w_gate": nrm((DEPTH, N_EXPERTS, D_MODEL, EXPERT_HIDDEN), D_MODEL ** -0.5),
        "moe_w_up": nrm((DEPTH, N_EXPERTS, D_MODEL, EXPERT_HIDDEN), D_MODEL ** -0.5),
        "moe_w_down": nrm((DEPTH, N_EXPERTS, EXPERT_HIDDEN, D_MODEL), EXPERT_HIDDEN ** -0.5),
    }


def reference(x, ev_norm, ev_w_in, ev_b_forget, ev_w_s, ev_b_s, ev_g_v, ev_g_q, ev_g_k, ev_w_out,
              od_norm, od_w_in, od_conv_w, od_w_pool, od_pool_scale, od_w_out,
              moe_norm, moe_w_group, moe_b_group, moe_w_router, moe_b_router,
              moe_w_gate, moe_w_up, moe_w_down):
    for layer in range(DEPTH):
        j = layer // 2
        if layer % 2 == 0:
            x = x + even_mixer(rms_norm(x, ev_norm[j]), ev_w_in[j], ev_b_forget[j], ev_w_s[j], ev_b_s[j],
                               ev_g_v[j], ev_g_q[j], ev_g_k[j], ev_w_out[j])
        else:
            x = x + odd_mixer(rms_norm(x, od_norm[j]), od_w_in[j], od_conv_w[j], od_w_pool[j],
                              od_pool_scale[j], od_w_out[j])
        x = x + hier_moe(rms_norm(x, moe_norm[layer]), moe_w_group[layer], moe_b_group[layer],
                         moe_w_router[layer], moe_b_router[layer], moe_w_gate[layer],
                         moe_w_up[layer], moe_w_down[layer])
    return x
```

```python
import functools

import jax
import jax.numpy as jnp
from jax import lax
from jax.experimental import pallas as pl
from jax.experimental.pallas import tpu as pltpu

F32 = jnp.float32
BF16 = jnp.bfloat16

D_MODEL = 1024
EPS = 1e-6
CHUNK = 128
GMLP_GROUPS = 4
GMLP_HEAD = 64
GMLP_WIDTH = 256
FOX_HEADS = 12
HEAD_DIM = 64
FOX_WIDTH = 768
CONV_WIDTH = 512
CONV_K = 3
POOL_WINDOWS = (2, 4, 8, 16)
POOL_GROUP = 128
POOL_WIDTH = 512
N_GROUPS = 4
EXPERTS_PER_GROUP = 4
N_EXPERTS = 16
EXPERT_HIDDEN = 256

LANES = 128
V7X_VMEM_LIMIT_BYTES = 56 * 1024 * 1024

TM = 512
TQ = 256
TM_MOE = 512
EVEN_PROJ_PAD = 2944
NEG = -1e30


def _cparams(sem):
    return pltpu.CompilerParams(dimension_semantics=sem,
                                vmem_limit_bytes=V7X_VMEM_LIMIT_BYTES)


def _dot(a, b):
    return jnp.dot(a, b, preferred_element_type=F32)


def _split_bf16(x, terms):
    parts = []
    r = x
    for _ in range(terms):
        p = r.astype(BF16)
        parts.append(p)
        r = r - p.astype(F32)
    return parts


def _dot_split(x, w_bf16, terms=2, w_left=False):
    acc = None
    for p in _split_bf16(x, terms):
        d = _dot(w_bf16, p) if w_left else _dot(p, w_bf16)
        acc = d if acc is None else acc + d
    return acc


def _rms_rows(x, g):
    ms = jnp.mean(x * x, axis=-1, keepdims=True)
    return x * lax.rsqrt(ms + EPS) * g


def _head_rms(x, bd, g):
    outs = []
    for c in range(x.shape[1] // 256):
        xc = x[:, c * 256:(c + 1) * 256]
        ms = _dot_split(xc * xc, bd)
        outs.append(xc * lax.rsqrt(ms + EPS))
    y = outs[0] if len(outs) == 1 else jnp.concatenate(outs, axis=1)
    return y * g


def _log_sigmoid(x):
    return -(jnp.maximum(-x, 0.0) + jnp.log(1.0 + jnp.exp(-jnp.abs(x))))


def _router(xn_bf16, wr_ref, br_ref):
    r = _dot(xn_bf16, wr_ref[...]) + br_ref[...]
    tm = r.shape[0]
    lane = lax.broadcasted_iota(jnp.int32, (tm, LANES), 1).astype(F32)
    is_g = (lane >= float(N_EXPERTS)) & (lane < float(N_EXPERTS + N_GROUPS))
    gl = jnp.where(is_g, r, NEG)
    gmax = jnp.max(gl, axis=1, keepdims=True)
    gidx = jnp.min(jnp.where(gl == gmax, lane, 999.0), axis=1, keepdims=True) - float(N_EXPERTS)
    gsum = jnp.sum(jnp.where(is_g, jnp.exp(gl - gmax), 0.0), axis=1, keepdims=True)
    gp = 1.0 / gsum
    lo = gidx * float(EXPERTS_PER_GROUP)
    sel = (lane >= lo) & (lane < lo + float(EXPERTS_PER_GROUP))
    el = jnp.where(sel, r, NEG)
    emax = jnp.max(el, axis=1, keepdims=True)
    ee = jnp.where(sel, jnp.exp(el - emax), 0.0)
    ep = ee / jnp.sum(ee, axis=1, keepdims=True)
    ep = jnp.where(sel, ep, -1.0)
    p1 = jnp.max(ep, axis=1, keepdims=True)
    i1 = jnp.min(jnp.where(ep == p1, lane, 999.0), axis=1, keepdims=True)
    ep2 = jnp.where(lane == i1, -1.0, ep)
    p2 = jnp.max(ep2, axis=1, keepdims=True)
    i2 = jnp.min(jnp.where(ep2 == p2, lane, 999.0), axis=1, keepdims=True)
    den = p1 + p2
    return (jnp.where(lane == i1, gp * (p1 / den), 0.0)
            + jnp.where(lane == i2, gp * (p2 / den), 0.0))


def _even_in_kernel(tiles_per_seq, x_ref, g_ref, w_ref, bf_ref, wtril_ref, bmat_ref, gv_ref,
                    gq_ref, gk_ref, bd_ref, ltri_ref,
                    a_ref, q_ref, k_ref, v_ref, f_ref, carry_ref):
    i = pl.program_id(0)
    tm = x_ref.shape[0]
    xn = _rms_rows(x_ref[...], g_ref[...]).astype(BF16)
    h = _dot(xn, w_ref[...])
    u = h[:, 0:256]
    v = h[:, 256:512]
    q = h[:, 512:1280]
    k = h[:, 1280:2048]
    val = h[:, 2048:2816]
    f = h[:, 2816:2944]

    gu = jax.nn.gelu(u)
    vn = _head_rms(jax.nn.gelu(v), bd_ref[...], gv_ref[...])
    lane = lax.broadcasted_iota(jnp.int32, (tm, LANES), 1)
    lo_half = lane < GMLP_HEAD
    bmat = bmat_ref[...]
    pair_out = []
    for pr in range(2):
        vp = vn[:, pr * 128:(pr + 1) * 128]
        v_lo = jnp.where(lo_half, vp, 0.0).astype(BF16)
        v_hi = jnp.where(lo_half, 0.0, vp).astype(BF16)
        chunks = []
        for c in range(tm // CHUNK):
            rs = slice(c * CHUNK, (c + 1) * CHUNK)
            s = (_dot(wtril_ref[2 * pr], v_lo[rs]) + _dot(wtril_ref[2 * pr + 1], v_hi[rs])
                 + bmat[:, pr * 128:(pr + 1) * 128])
            chunks.append(s)
        pair_out.append(jnp.concatenate(chunks, axis=0))
    s_all = jnp.concatenate(pair_out, axis=1)
    a_ref[...] = (gu * s_all).astype(BF16)

    bd = bd_ref[...]
    q_ref[...] = (_head_rms(q, bd, gq_ref[...]) * (HEAD_DIM ** -0.5)).astype(BF16)
    k_ref[...] = _head_rms(k, bd, gk_ref[...]).astype(BF16)
    v_ref[...] = val.astype(BF16)

    @pl.when(i % tiles_per_seq == 0)
    def _():
        carry_ref[...] = jnp.zeros_like(carry_ref)

    logf = _log_sigmoid(f + bf_ref[...])
    cum = _dot_split(logf, ltri_ref[...], terms=3, w_left=True) + carry_ref[0:1, :]
    f_ref[...] = cum
    carry_ref[0:1, :] = cum[tm - 1:tm, :]


def _even_in(x2, g, w, bf, wtril, bmat, gv, gq, gk, bd, ltri, seq):
    t = x2.shape[0]
    n = t // TM
    full = lambda a: pl.BlockSpec(a.shape, lambda i: (0,) * a.ndim)
    row = lambda wdt: pl.BlockSpec((TM, wdt), lambda i: (i, 0))
    return pl.pallas_call(
        functools.partial(_even_in_kernel, seq // TM),
        grid=(n,),
        in_specs=[row(D_MODEL), full(g), full(w), full(bf), full(wtril), full(bmat), full(gv),
                  full(gq), full(gk), full(bd), full(ltri)],
        out_specs=[row(GMLP_WIDTH), row(FOX_WIDTH), row(FOX_WIDTH), row(FOX_WIDTH), row(LANES)],
        out_shape=[jax.ShapeDtypeStruct((t, GMLP_WIDTH), BF16),
                   jax.ShapeDtypeStruct((t, FOX_WIDTH), BF16),
                   jax.ShapeDtypeStruct((t, FOX_WIDTH), BF16),
                   jax.ShapeDtypeStruct((t, FOX_WIDTH), BF16),
                   jax.ShapeDtypeStruct((t, LANES), F32)],
        scratch_shapes=[pltpu.VMEM((8, LANES), F32)],
        compiler_params=_cparams(("arbitrary",)),
        name="even_in",
    )(x2, g, w, bf, wtril, bmat, gv, gq, gk, bd, ltri)


def _fox_kernel(q_ref, k_ref, v_ref, fk_ref, fq_ref, o_ref, m_sc, l_sc, acc_sc):
    pr = pl.program_id(1)
    i = pl.program_id(2)
    tq = q_ref.shape[0]
    q = q_ref[...]
    lane = lax.broadcasted_iota(jnp.int32, (tq, LANES), 1)
    lo_half = lane < HEAD_DIM
    zero = jnp.zeros_like(q)
    qh = (jnp.where(lo_half, q, zero), jnp.where(lo_half, zero, q))
    fq = fq_ref[...]
    ft = tuple(jnp.sum(jnp.where(lane == 2 * pr + hh, fq, 0.0), axis=1, keepdims=True)
               for hh in range(2))

    m_sc[...] = jnp.full_like(m_sc, NEG)
    l_sc[...] = jnp.zeros_like(l_sc)
    acc_sc[...] = jnp.zeros_like(acc_sc)

    def step(j, masked):
        start = pl.multiple_of(j * tq, tq)
        kb = k_ref[pl.ds(start, tq), :]
        vb = v_ref[pl.ds(start, tq), :]
        fk = fk_ref[0, 0, :, pl.ds(start, tq)]
        for hh in range(2):
            s = lax.dot_general(qh[hh], kb, (((1,), (1,)), ((), ())),
                                preferred_element_type=F32)
            s = s + ft[hh] - fk[hh:hh + 1, :]
            if masked:
                r_id = lax.broadcasted_iota(jnp.int32, (tq, tq), 0)
                c_id = lax.broadcasted_iota(jnp.int32, (tq, tq), 1)
                s = jnp.where(c_id <= r_id, s, NEG)
            m_prev = m_sc[hh]
            m_new = jnp.maximum(m_prev, jnp.max(s, axis=1, keepdims=True))
            alpha = jnp.exp(m_prev - m_new)
            p = jnp.exp(s - m_new)
            l_sc[hh] = alpha * l_sc[hh] + jnp.sum(p, axis=1, keepdims=True)
            acc_sc[hh] = alpha * acc_sc[hh] + _dot(p.astype(BF16), vb)
            m_sc[hh] = m_new

    def body(j, c):
        step(j, False)
        return c

    lax.fori_loop(0, i, body, 0)
    step(i, True)
    o0 = acc_sc[0] / l_sc[0]
    o1 = acc_sc[1] / l_sc[1]
    o_ref[...] = jnp.where(lo_half, o0, o1).astype(o_ref.dtype)


def _fox(q, k, v, fk, fq, bsz, seq):
    t = q.shape[0]
    nq = seq // TQ
    pairs = FOX_HEADS // 2
    return pl.pallas_call(
        _fox_kernel,
        grid=(bsz, pairs, nq),
        in_specs=[pl.BlockSpec((TQ, LANES), lambda b, p, i: (b * nq + i, p)),
                  pl.BlockSpec((seq, LANES), lambda b, p, i: (b, p)),
                  pl.BlockSpec((seq, LANES), lambda b, p, i: (b, p)),
                  pl.BlockSpec((1, 1, 2, seq), lambda b, p, i: (b, p, 0, 0)),
                  pl.BlockSpec((TQ, LANES), lambda b, p, i: (b * nq + i, 0))],
        out_specs=pl.BlockSpec((TQ, LANES), lambda b, p, i: (b * nq + i, p)),
        out_shape=jax.ShapeDtypeStruct((t, FOX_WIDTH), BF16),
        scratch_shapes=[pltpu.VMEM((2, TQ, 1), F32), pltpu.VMEM((2, TQ, 1), F32),
                        pltpu.VMEM((2, TQ, LANES), F32)],
        compiler_params=_cparams(("parallel", "parallel", "arbitrary")),
        name="fox_attention",
    )(q, k, v, fk, fq)


def _even_out_kernel(a_ref, o_ref, x_ref, wa_ref, wo_ref, gm_ref, wr_ref, br_ref,
                     x1_ref, xn_ref, comb_ref):
    x1 = x_ref[...] + _dot(a_ref[...], wa_ref[...]) + _dot(o_ref[...], wo_ref[...])
    x1_ref[...] = x1
    xn = _rms_rows(x1, gm_ref[...]).astype(BF16)
    xn_ref[...] = xn
    comb_ref[...] = _router(xn, wr_ref, br_ref)


def _even_out(a, o, x2, wa, wo, gm, wr, br):
    t = x2.shape[0]
    n = t // TM
    full = lambda arr: pl.BlockSpec(arr.shape, lambda i: (0,) * arr.ndim)
    row = lambda wdt: pl.BlockSpec((TM, wdt), lambda i: (i, 0))
    return pl.pallas_call(
        _even_out_kernel,
        grid=(n,),
        in_specs=[row(GMLP_WIDTH), row(FOX_WIDTH), row(D_MODEL), full(wa), full(wo), full(gm),
                  full(wr), full(br)],
        out_specs=[row(D_MODEL), row(D_MODEL), row(LANES)],
        out_shape=[jax.ShapeDtypeStruct((t, D_MODEL), F32),
                   jax.ShapeDtypeStruct((t, D_MODEL), BF16),
                   jax.ShapeDtypeStruct((t, LANES), F32)],
        compiler_params=_cparams(("parallel",)),
        name="even_out",
    )(a, o, x2, wa, wo, gm, wr, br)


def _odd_kernel(tiles_per_seq, x_ref, g_ref, w_ref, cw_ref, wp_ref, ps_ref, wo_ref, gm_ref,
                wr_ref, br_ref, x1_ref, xn_ref, comb_ref, zbuf, pbuf):
    i = pl.program_id(0)
    tm = x_ref.shape[0]
    zpad = zbuf.shape[0] - tm
    ppad = pbuf.shape[0] - tm
    x = x_ref[...]
    xn = _rms_rows(x, g_ref[...]).astype(BF16)
    h = _dot(xn, w_ref[...])
    bg = h[:, 0:512]
    cg = h[:, 512:1024]
    hc = h[:, 1024:1536]
    p = h[:, 1536:2048]

    @pl.when(i % tiles_per_seq == 0)
    def _():
        zbuf[0:zpad, :] = jnp.zeros((zpad, CONV_WIDTH), F32)
        pbuf[0:ppad, :] = jnp.zeros((ppad, POOL_WIDTH), F32)

    z = cg * hc
    zbuf[zpad:zpad + tm, :] = z
    cw = cw_ref[...]
    y = (cw[0:1, :] * zbuf[zpad - 2:zpad - 2 + tm, :]
         + cw[1:2, :] * zbuf[zpad - 1:zpad - 1 + tm, :]
         + cw[2:3, :] * z)
    c = bg * y
    zbuf[0:zpad, :] = zbuf[tm:tm + zpad, :]

    pbuf[ppad:ppad + tm, :] = p
    pos = ((i % tiles_per_seq) * tm
           + lax.broadcasted_iota(jnp.int32, (tm, POOL_GROUP), 0)).astype(F32) + 1.0
    acc = x + _dot(c.astype(BF16), wo_ref[0:CONV_WIDTH, :])
    ps = ps_ref[...]
    for gi, win in enumerate(POOL_WINDOWS):
        ls = slice(gi * POOL_GROUP, (gi + 1) * POOL_GROUP)
        pg = p[:, ls]
        sm = pg
        for sh in range(1, win):
            sm = sm + pbuf[ppad - sh:ppad - sh + tm, ls]
        cnt = jnp.minimum(pos, float(win))
        pooled = sm / cnt - pg
        yg = _dot(pooled.astype(BF16), wp_ref[gi]) * ps[:, ls]
        acc = acc + _dot(yg.astype(BF16),
                         wo_ref[CONV_WIDTH + gi * POOL_GROUP:CONV_WIDTH + (gi + 1) * POOL_GROUP, :])
    pbuf[0:ppad, :] = pbuf[tm:tm + ppad, :]

    x1_ref[...] = acc
    xn2 = _rms_rows(acc, gm_ref[...]).astype(BF16)
    xn_ref[...] = xn2
    comb_ref[...] = _router(xn2, wr_ref, br_ref)


def _odd(x2, g, w, cw, wp, ps, wo, gm, wr, br, seq):
    t = x2.shape[0]
    n = t // TM
    full = lambda arr: pl.BlockSpec(arr.shape, lambda i: (0,) * arr.ndim)
    row = lambda wdt: pl.BlockSpec((TM, wdt), lambda i: (i, 0))
    return pl.pallas_call(
        functools.partial(_odd_kernel, seq // TM),
        grid=(n,),
        in_specs=[row(D_MODEL), full(g), full(w), full(cw), full(wp), full(ps), full(wo),
                  full(gm), full(wr), full(br)],
        out_specs=[row(D_MODEL), row(D_MODEL), row(LANES)],
        out_shape=[jax.ShapeDtypeStruct((t, D_MODEL), F32),
                   jax.ShapeDtypeStruct((t, D_MODEL), BF16),
                   jax.ShapeDtypeStruct((t, LANES), F32)],
        scratch_shapes=[pltpu.VMEM((TM + 8, CONV_WIDTH), F32),
                        pltpu.VMEM((TM + 16, POOL_WIDTH), F32)],
        compiler_params=_cparams(("arbitrary",)),
        name="odd_mixer",
    )(x2, g, w, cw, wp, ps, wo, gm, wr, br)


def _moe_kernel(xn_ref, wgu_ref, wd_ref, comb_ref, x1_ref, o_ref):
    e = pl.program_id(1)

    @pl.when(e == 0)
    def _():
        o_ref[...] = x1_ref[...]

    tm = xn_ref.shape[0]
    h = _dot(xn_ref[...], wgu_ref[0])
    gate = h[:, 0:EXPERT_HIDDEN]
    up = h[:, EXPERT_HIDDEN:2 * EXPERT_HIDDEN]
    lane = lax.broadcasted_iota(jnp.int32, (tm, LANES), 1)
    cw = jnp.sum(jnp.where(lane == e, comb_ref[...], 0.0), axis=1, keepdims=True)
    hid = (gate * jax.nn.sigmoid(gate)) * up * cw
    o_ref[...] += _dot(hid.astype(BF16), wd_ref[0])


def _moe(xn, wgu, wd, comb, x1):
    t = xn.shape[0]
    n = t // TM_MOE
    return pl.pallas_call(
        _moe_kernel,
        grid=(n, N_EXPERTS),
        in_specs=[pl.BlockSpec((TM_MOE, D_MODEL), lambda i, e: (i, 0)),
                  pl.BlockSpec((1, D_MODEL, 2 * EXPERT_HIDDEN), lambda i, e: (e, 0, 0)),
                  pl.BlockSpec((1, EXPERT_HIDDEN, D_MODEL), lambda i, e: (e, 0, 0)),
                  pl.BlockSpec((TM_MOE, LANES), lambda i, e: (i, 0)),
                  pl.BlockSpec((TM_MOE, D_MODEL), lambda i, e: (i, 0))],
        out_specs=pl.BlockSpec((TM_MOE, D_MODEL), lambda i, e: (i, 0)),
        out_shape=jax.ShapeDtypeStruct((t, D_MODEL), F32),
        compiler_params=_cparams(("parallel", "arbitrary")),
        name="moe",
    )(xn, wgu, wd, comb, x1)


def _router_params(w_group, b_group, w_router, b_router):
    wr = jnp.concatenate([w_router.reshape(D_MODEL, N_EXPERTS), w_group], axis=1)
    wr = jnp.pad(wr, ((0, 0), (0, LANES - wr.shape[1]))).astype(BF16)
    br = jnp.concatenate([b_router.reshape(N_EXPERTS), b_group])
    br = jnp.pad(br, (0, LANES - br.shape[0])).reshape(1, LANES).astype(F32)
    return wr, br


def _moe_params(w_gate, w_up, w_down):
    return jnp.concatenate([w_gate, w_up], axis=-1).astype(BF16), w_down.astype(BF16)


def kernel(x, ev_norm, ev_w_in, ev_b_forget, ev_w_s, ev_b_s, ev_g_v, ev_g_q, ev_g_k, ev_w_out,
           od_norm, od_w_in, od_conv_w, od_w_pool, od_pool_scale, od_w_out,
           moe_norm, moe_w_group, moe_b_group, moe_w_router, moe_b_router,
           moe_w_gate, moe_w_up, moe_w_down):
    bsz, seq, d = x.shape
    t = bsz * seq
    x2 = x.reshape(t, d)

    w_in = jnp.pad(ev_w_in[0], ((0, 0), (0, EVEN_PROJ_PAD - ev_w_in.shape[2]))).astype(BF16)
    bf = jnp.pad(ev_b_forget[0], (0, LANES - FOX_HEADS)).reshape(1, LANES)
    tril = jnp.tril(jnp.ones((CHUNK, CHUNK), F32))
    wtril = (ev_w_s[0] * tril).astype(BF16)
    bmat = jnp.repeat(ev_b_s[0].T, GMLP_HEAD, axis=1)
    gv = ev_g_v[0].reshape(1, GMLP_WIDTH)
    gq = jnp.tile(ev_g_q[0], FOX_HEADS).reshape(1, FOX_WIDTH)
    gk = jnp.tile(ev_g_k[0], FOX_HEADS).reshape(1, FOX_WIDTH)
    blk = jnp.arange(256) // HEAD_DIM
    bd = jnp.where(blk[:, None] == blk[None, :], 1.0 / HEAD_DIM, 0.0).astype(BF16)
    ltri = jnp.tril(jnp.ones((TM, TM), F32)).astype(BF16)

    a, qn, kn, vv, fcum = _even_in(x2, ev_norm[0].reshape(1, d), w_in, bf, wtril, bmat, gv, gq,
                                   gk, bd, ltri, seq)
    fk = fcum[:, :FOX_HEADS].reshape(bsz, seq, FOX_HEADS // 2, 2).transpose(0, 2, 3, 1)
    o = _fox(qn, kn, vv, fk, fcum, bsz, seq)

    wr0, br0 = _router_params(moe_w_group[0], moe_b_group[0], moe_w_router[0], moe_b_router[0])
    w_out0 = ev_w_out[0].astype(BF16)
    x1, xn1, comb1 = _even_out(a, o, x2, w_out0[:GMLP_WIDTH], w_out0[GMLP_WIDTH:],
                               moe_norm[0].reshape(1, d), wr0, br0)
    wgu0, wd0 = _moe_params(moe_w_gate[0], moe_w_up[0], moe_w_down[0])
    xa = _moe(xn1, wgu0, wd0, comb1, x1)

    wr1, br1 = _router_params(moe_w_group[1], moe_b_group[1], moe_w_router[1], moe_b_router[1])
    x3, xn3, comb3 = _odd(xa, od_norm[0].reshape(1, d), od_w_in[0].astype(BF16), od_conv_w[0],
                          od_w_pool[0].astype(BF16), od_pool_scale[0].reshape(1, POOL_WIDTH),
                          od_w_out[0].astype(BF16), moe_norm[1].reshape(1, d), wr1, br1, seq)
    wgu1, wd1 = _moe_params(moe_w_gate[1], moe_w_up[1], moe_w_down[1])
    xb = _moe(xn3, wgu1, wd1, comb3, x3)
    return xb.reshape(bsz, seq, d)
```

```python
import functools

import jax
import jax.numpy as jnp
from jax import lax
from jax.experimental import pallas as pl
from jax.experimental.pallas import tpu as pltpu

F32 = jnp.float32
BF16 = jnp.bfloat16

D_MODEL = 1024
EPS = 1e-6
CHUNK = 128
GMLP_GROUPS = 4
GMLP_HEAD = 64
GMLP_WIDTH = 256
FOX_HEADS = 12
HEAD_DIM = 64
FOX_WIDTH = 768
CONV_WIDTH = 512
CONV_K = 3
POOL_WINDOWS = (2, 4, 8, 16)
POOL_GROUP = 128
POOL_WIDTH = 512
N_GROUPS = 4
EXPERTS_PER_GROUP = 4
N_EXPERTS = 16
EXPERT_HIDDEN = 256

LANES = 128
V7X_VMEM_LIMIT_BYTES = 56 * 1024 * 1024

TM = 512
TQ = 256
TK = 512
TM_MOE = 512
EVEN_PROJ_PAD = 2944
NEG = -1e30
LOG2E = 1.4426950408889634
FOX_FAST_BOUND_LOG2 = 50.0


def _cparams(sem):
    return pltpu.CompilerParams(dimension_semantics=sem,
                                vmem_limit_bytes=V7X_VMEM_LIMIT_BYTES)


def _dot(a, b):
    return jnp.dot(a, b, preferred_element_type=F32)


def _split_bf16(x, terms):
    parts = []
    r = x
    for _ in range(terms):
        p = r.astype(BF16)
        parts.append(p)
        r = r - p.astype(F32)
    return parts


def _dot_split(x, w_bf16, terms=2, w_left=False):
    acc = None
    for p in _split_bf16(x, terms):
        d = _dot(w_bf16, p) if w_left else _dot(p, w_bf16)
        acc = d if acc is None else acc + d
    return acc


def _rms_rows(x, g):
    ms = jnp.mean(x * x, axis=-1, keepdims=True)
    return x * lax.rsqrt(ms + EPS) * g


def _head_rms(x, bd, g):
    outs = []
    for c in range(x.shape[1] // 256):
        xc = x[:, c * 256:(c + 1) * 256]
        ms = _dot_split(xc * xc, bd)
        outs.append(xc * lax.rsqrt(ms + EPS))
    y = outs[0] if len(outs) == 1 else jnp.concatenate(outs, axis=1)
    return y * g


def _log_sigmoid(x):
    return -(jnp.maximum(-x, 0.0) + jnp.log(1.0 + jnp.exp(-jnp.abs(x))))


def _router(xn_bf16, wr_ref, br_ref):
    r = _dot(xn_bf16, wr_ref[...]) + br_ref[...]
    tm = r.shape[0]
    lane = lax.broadcasted_iota(jnp.int32, (tm, LANES), 1).astype(F32)
    is_g = (lane >= float(N_EXPERTS)) & (lane < float(N_EXPERTS + N_GROUPS))
    gl = jnp.where(is_g, r, NEG)
    gmax = jnp.max(gl, axis=1, keepdims=True)
    gidx = jnp.min(jnp.where(gl == gmax, lane, 999.0), axis=1, keepdims=True) - float(N_EXPERTS)
    gsum = jnp.sum(jnp.where(is_g, jnp.exp(gl - gmax), 0.0), axis=1, keepdims=True)
    gp = 1.0 / gsum
    lo = gidx * float(EXPERTS_PER_GROUP)
    sel = (lane >= lo) & (lane < lo + float(EXPERTS_PER_GROUP))
    el = jnp.where(sel, r, NEG)
    emax = jnp.max(el, axis=1, keepdims=True)
    ee = jnp.where(sel, jnp.exp(el - emax), 0.0)
    ep = ee / jnp.sum(ee, axis=1, keepdims=True)
    ep = jnp.where(sel, ep, -1.0)
    p1 = jnp.max(ep, axis=1, keepdims=True)
    i1 = jnp.min(jnp.where(ep == p1, lane, 999.0), axis=1, keepdims=True)
    ep2 = jnp.where(lane == i1, -1.0, ep)
    p2 = jnp.max(ep2, axis=1, keepdims=True)
    i2 = jnp.min(jnp.where(ep2 == p2, lane, 999.0), axis=1, keepdims=True)
    den = p1 + p2
    return (jnp.where(lane == i1, gp * (p1 / den), 0.0)
            + jnp.where(lane == i2, gp * (p2 / den), 0.0))


def _even_in_kernel(tiles_per_seq, x_ref, g_ref, w_ref, bf_ref, wtril_ref, bmat_ref, gv_ref,
                    gq_ref, gk_ref, bd_ref, ltri_ref,
                    a_ref, q_ref, k_ref, v_ref, f_ref, fq_ref, carry_ref):
    i = pl.program_id(0)
    tm = x_ref.shape[0]
    xn = _rms_rows(x_ref[...], g_ref[...]).astype(BF16)
    h = _dot(xn, w_ref[...])
    u = h[:, 0:256]
    v = h[:, 256:512]
    q = h[:, 512:1280]
    k = h[:, 1280:2048]
    val = h[:, 2048:2816]
    f = h[:, 2816:2944]

    gu = jax.nn.gelu(u)
    vn = _head_rms(jax.nn.gelu(v), bd_ref[...], gv_ref[...])
    lane = lax.broadcasted_iota(jnp.int32, (tm, LANES), 1)
    lo_half = lane < GMLP_HEAD
    bmat = bmat_ref[...]
    pair_out = []
    for pr in range(2):
        vp = vn[:, pr * 128:(pr + 1) * 128]
        v_lo = jnp.where(lo_half, vp, 0.0).astype(BF16)
        v_hi = jnp.where(lo_half, 0.0, vp).astype(BF16)
        chunks = []
        for c in range(tm // CHUNK):
            rs = slice(c * CHUNK, (c + 1) * CHUNK)
            s = (_dot(wtril_ref[2 * pr], v_lo[rs]) + _dot(wtril_ref[2 * pr + 1], v_hi[rs])
                 + bmat[:, pr * 128:(pr + 1) * 128])
            chunks.append(s)
        pair_out.append(jnp.concatenate(chunks, axis=0))
    s_all = jnp.concatenate(pair_out, axis=1)
    a_ref[...] = (gu * s_all).astype(BF16)

    bd = bd_ref[...]
    qn = _head_rms(q, bd, gq_ref[...]) * (HEAD_DIM ** -0.5 * LOG2E)
    for pr in range(FOX_HEADS // 2):
        ls = slice(pr * LANES, (pr + 1) * LANES)
        q_ref[pr] = qn[:, ls].T.astype(BF16)
        v_ref[pr] = val[:, ls].T.astype(BF16)
    k_ref[...] = _head_rms(k, bd, gk_ref[...]).astype(BF16)

    @pl.when(i % tiles_per_seq == 0)
    def _():
        carry_ref[...] = jnp.zeros_like(carry_ref)

    logf = _log_sigmoid(f + bf_ref[...])
    cum = _dot_split(logf, ltri_ref[...], terms=3, w_left=True) + carry_ref[0:1, :]
    carry_ref[0:1, :] = cum[tm - 1:tm, :]
    cum2 = cum * LOG2E
    fq_ref[...] = cum2
    hi, mid, lo = (p.astype(F32) for p in _split_bf16(cum2, 3))
    f_ref[...] = jnp.where(
        lane < FOX_HEADS, hi,
        jnp.where(lane < 2 * FOX_HEADS, mid,
                  jnp.where(lane < 3 * FOX_HEADS, lo,
                            jnp.where(lane < 3 * FOX_HEADS + 3, 1.0, 0.0)))).astype(BF16)


def _even_in(x2, g, w, bf, wtril, bmat, gv, gq, gk, bd, ltri, seq):
    t = x2.shape[0]
    n = t // TM
    full = lambda a: pl.BlockSpec(a.shape, lambda i: (0,) * a.ndim)
    row = lambda wdt: pl.BlockSpec((TM, wdt), lambda i: (i, 0))
    pairs = FOX_HEADS // 2
    colT = pl.BlockSpec((pairs, LANES, TM), lambda i: (0, 0, i))
    return pl.pallas_call(
        functools.partial(_even_in_kernel, seq // TM),
        grid=(n,),
        in_specs=[row(D_MODEL), full(g), full(w), full(bf), full(wtril), full(bmat), full(gv),
                  full(gq), full(gk), full(bd), full(ltri)],
        out_specs=[row(GMLP_WIDTH), colT, row(FOX_WIDTH), colT, row(LANES), row(LANES)],
        out_shape=[jax.ShapeDtypeStruct((t, GMLP_WIDTH), BF16),
                   jax.ShapeDtypeStruct((pairs, LANES, t), BF16),
                   jax.ShapeDtypeStruct((t, FOX_WIDTH), BF16),
                   jax.ShapeDtypeStruct((pairs, LANES, t), BF16),
                   jax.ShapeDtypeStruct((t, LANES), BF16),
                   jax.ShapeDtypeStruct((t, LANES), F32)],
        scratch_shapes=[pltpu.VMEM((8, LANES), F32)],
        compiler_params=_cparams(("arbitrary",)),
        name="even_in",
    )(x2, g, w, bf, wtril, bmat, gv, gq, gk, bd, ltri)


def _fox_query_operand(qt, pr, hh, shift_terms=None):
    tq = qt.shape[1]
    row = lax.broadcasted_iota(jnp.int32, (LANES, tq), 0)
    head = 2 * pr + hh
    in_head = (row >= hh * HEAD_DIM) & (row < (hh + 1) * HEAD_DIM)
    f_rows = (row == head) | (row == head + FOX_HEADS) | (row == head + 2 * FOX_HEADS)
    extra = jnp.where(f_rows, -1.0, 0.0)
    if shift_terms is not None:
        for n, term in enumerate(shift_terms):
            extra = jnp.where(row == 3 * FOX_HEADS + n, term, extra)
    return jnp.concatenate([jnp.where(in_head, qt, 0.0).astype(BF16), extra.astype(BF16)],
                           axis=0)


def _fox_online_kernel(q_ref, k_ref, f_ref, v_ref, o_ref, m_sc, l_sc, acc_sc):
    pr = pl.program_id(1)
    i = pl.program_id(2)
    tq = q_ref.shape[2]
    qt = q_ref[0].astype(F32)
    rhs = [_fox_query_operand(qt, pr, hh) for hh in range(2)]

    m_sc[...] = jnp.full_like(m_sc, NEG)
    l_sc[...] = jnp.zeros_like(l_sc)
    acc_sc[...] = jnp.zeros_like(acc_sc)

    def step(j, masked):
        start = pl.multiple_of(j * tq, tq)
        kaug = jnp.concatenate([k_ref[pl.ds(start, tq), :], f_ref[pl.ds(start, tq), :]],
                               axis=1)
        for hh in range(2):
            s = _dot(kaug, rhs[hh])
            if masked:
                r_id = lax.broadcasted_iota(jnp.int32, (tq, tq), 0)
                c_id = lax.broadcasted_iota(jnp.int32, (tq, tq), 1)
                s = jnp.where(r_id <= c_id, s, NEG)
            m_prev = m_sc[hh]
            m_new = jnp.maximum(m_prev, jnp.max(s, axis=0, keepdims=True))
            alpha = jnp.exp2(m_prev - m_new)
            p = jnp.exp2(s - m_new)
            l_sc[hh] = alpha * l_sc[hh] + jnp.sum(p, axis=0, keepdims=True)
            vt = v_ref[0, hh * HEAD_DIM:(hh + 1) * HEAD_DIM, pl.ds(start, tq)]
            acc_sc[hh] = alpha * acc_sc[hh] + _dot(vt, p.astype(BF16))
            m_sc[hh] = m_new

    def body(j, c):
        step(j, False)
        return c

    lax.fori_loop(0, i, body, 0)
    step(i, True)
    ot = jnp.concatenate([acc_sc[0] / l_sc[0], acc_sc[1] / l_sc[1]], axis=0)
    o_ref[...] = ot.T.astype(o_ref.dtype)


def _fox_fixed_kernel(b_ref, q_ref, k_ref, f_ref, v_ref, fq_ref, o_ref,
                      rhs_sc, z_sc, acc_sc):
    pr = pl.program_id(1)
    i = pl.program_id(2)
    tq = q_ref.shape[2]
    tk = z_sc.shape[2]
    qt = q_ref[0].astype(F32)
    fqt = fq_ref[...].T
    row = lax.broadcasted_iota(jnp.int32, (LANES, tq), 0)
    for hh in range(2):
        ft = jnp.sum(jnp.where(row == 2 * pr + hh, fqt, 0.0), axis=0, keepdims=True)
        shift = [p.astype(F32) for p in _split_bf16(ft - b_ref[...], 3)]
        rhs_sc[hh] = _fox_query_operand(qt, pr, hh, shift)

    acc_sc[...] = jnp.zeros_like(acc_sc)
    ones = jnp.ones((acc_sc.shape[1] - HEAD_DIM, tk), BF16)

    def scores(j, slot):
        start = pl.multiple_of(j * tk, tk)
        kaug = jnp.concatenate([k_ref[pl.ds(start, tk), :], f_ref[pl.ds(start, tk), :]],
                               axis=1)
        for hh in range(2):
            z_sc[slot, hh] = _dot(kaug, rhs_sc[hh])

    def consume(j, slot, masked):
        start = pl.multiple_of(j * tk, tk)
        for hh in range(2):
            z = z_sc[slot, hh]
            if masked:
                s_id = start + lax.broadcasted_iota(jnp.int32, (tk, tq), 0)
                t_id = i * tq + lax.broadcasted_iota(jnp.int32, (tk, tq), 1)
                z = jnp.where(s_id <= t_id, z, NEG)
            p = jnp.exp2(z).astype(BF16)
            vt = jnp.concatenate(
                [v_ref[0, hh * HEAD_DIM:(hh + 1) * HEAD_DIM, pl.ds(start, tk)], ones], axis=0)
            acc_sc[hh] += _dot(vt, p)

    n_blocks = (i * tq + tq + tk - 1) // tk
    n_pairs = (n_blocks - 1) // 2
    scores(0, 0)

    def body(kk, c):
        j = 2 * kk
        scores(j + 1, 1)
        consume(j, 0, False)
        scores(j + 2, 0)
        consume(j + 1, 1, False)
        return c

    lax.fori_loop(0, n_pairs, body, 0)
    j_rest = 2 * n_pairs

    @pl.when(n_blocks - j_rest == 1)
    def _():
        consume(j_rest, 0, True)

    @pl.when(n_blocks - j_rest == 2)
    def _():
        scores(j_rest + 1, 1)
        consume(j_rest, 0, False)
        consume(j_rest + 1, 1, True)

    outs = []
    for hh in range(2):
        acc = acc_sc[hh]
        outs.append(acc[0:HEAD_DIM] / acc[HEAD_DIM:HEAD_DIM + 1])
    o_ref[...] = jnp.concatenate(outs, axis=0).T.astype(o_ref.dtype)


def _fox(qt, k, f3, vt, fq, bound, bsz, seq):
    t = k.shape[0]
    nq = seq // TQ
    pairs = FOX_HEADS // 2
    common_in = [pl.BlockSpec((1, LANES, TQ), lambda b, p, i: (p, 0, b * nq + i)),
                 pl.BlockSpec((seq, LANES), lambda b, p, i: (b, p)),
                 pl.BlockSpec((seq, LANES), lambda b, p, i: (b, 0)),
                 pl.BlockSpec((1, LANES, seq), lambda b, p, i: (p, 0, b))]
    out_spec = pl.BlockSpec((TQ, LANES), lambda b, p, i: (b * nq + i, p))
    out_shape = jax.ShapeDtypeStruct((t, FOX_WIDTH), BF16)
    stats = pltpu.VMEM((2, 1, TQ), F32)
    acc = pltpu.VMEM((2, HEAD_DIM, TQ), F32)
    sem = _cparams(("parallel", "parallel", "arbitrary"))

    def fixed(brow):
        return pl.pallas_call(
            _fox_fixed_kernel,
            grid=(bsz, pairs, nq),
            in_specs=[pl.BlockSpec((1, TQ), lambda b, p, i: (0, 0))] + common_in
            + [pl.BlockSpec((TQ, LANES), lambda b, p, i: (b * nq + i, 0))],
            out_specs=out_spec, out_shape=out_shape,
            scratch_shapes=[pltpu.VMEM((2, 2 * LANES, TQ), BF16),
                            pltpu.VMEM((2, 2, TK, TQ), F32),
                            pltpu.VMEM((2, HEAD_DIM + 16, TQ), F32)],
            compiler_params=sem, name="fox_attention_fixed",
        )(brow, qt, k, f3, vt, fq)

    def online(brow):
        del brow
        return pl.pallas_call(
            _fox_online_kernel,
            grid=(bsz, pairs, nq),
            in_specs=common_in, out_specs=out_spec, out_shape=out_shape,
            scratch_shapes=[stats, stats, acc], compiler_params=sem,
            name="fox_attention_online",
        )(qt, k, f3, vt)

    brow = jnp.full((1, TQ), bound, F32)
    return lax.cond(bound <= FOX_FAST_BOUND_LOG2, fixed, online, brow)


def _even_out_kernel(a_ref, o_ref, x_ref, wa_ref, wo_ref, gm_ref, wr_ref, br_ref,
                     x1_ref, xn_ref, comb_ref):
    x1 = x_ref[...] + _dot(a_ref[...], wa_ref[...]) + _dot(o_ref[...], wo_ref[...])
    x1_ref[...] = x1
    xn = _rms_rows(x1, gm_ref[...]).astype(BF16)
    xn_ref[...] = xn
    comb_ref[...] = _router(xn, wr_ref, br_ref)


def _even_out(a, o, x2, wa, wo, gm, wr, br):
    t = x2.shape[0]
    n = t // TM
    full = lambda arr: pl.BlockSpec(arr.shape, lambda i: (0,) * arr.ndim)
    row = lambda wdt: pl.BlockSpec((TM, wdt), lambda i: (i, 0))
    return pl.pallas_call(
        _even_out_kernel,
        grid=(n,),
        in_specs=[row(GMLP_WIDTH), row(FOX_WIDTH), row(D_MODEL), full(wa), full(wo), full(gm),
                  full(wr), full(br)],
        out_specs=[row(D_MODEL), row(D_MODEL), row(LANES)],
        out_shape=[jax.ShapeDtypeStruct((t, D_MODEL), F32),
                   jax.ShapeDtypeStruct((t, D_MODEL), BF16),
                   jax.ShapeDtypeStruct((t, LANES), F32)],
        compiler_params=_cparams(("parallel",)),
        name="even_out",
    )(a, o, x2, wa, wo, gm, wr, br)


def _odd_kernel(tiles_per_seq, x_ref, g_ref, w_ref, cw_ref, wp_ref, ps_ref, wo_ref, gm_ref,
                wr_ref, br_ref, x1_ref, xn_ref, comb_ref, zbuf, pbuf):
    i = pl.program_id(0)
    tm = x_ref.shape[0]
    zpad = zbuf.shape[0] - tm
    ppad = pbuf.shape[0] - tm
    x = x_ref[...]
    xn = _rms_rows(x, g_ref[...]).astype(BF16)
    h = _dot(xn, w_ref[...])
    bg = h[:, 0:512]
    cg = h[:, 512:1024]
    hc = h[:, 1024:1536]
    p = h[:, 1536:2048]

    @pl.when(i % tiles_per_seq == 0)
    def _():
        zbuf[0:zpad, :] = jnp.zeros((zpad, CONV_WIDTH), F32)
        pbuf[0:ppad, :] = jnp.zeros((ppad, POOL_WIDTH), F32)

    z = cg * hc
    zbuf[zpad:zpad + tm, :] = z
    cw = cw_ref[...]
    y = (cw[0:1, :] * zbuf[zpad - 2:zpad - 2 + tm, :]
         + cw[1:2, :] * zbuf[zpad - 1:zpad - 1 + tm, :]
         + cw[2:3, :] * z)
    c = bg * y
    zbuf[0:zpad, :] = zbuf[tm:tm + zpad, :]

    pbuf[ppad:ppad + tm, :] = p
    pos = ((i % tiles_per_seq) * tm
           + lax.broadcasted_iota(jnp.int32, (tm, POOL_GROUP), 0)).astype(F32) + 1.0
    acc = x + _dot(c.astype(BF16), wo_ref[0:CONV_WIDTH, :])
    ps = ps_ref[...]
    for gi, win in enumerate(POOL_WINDOWS):
        ls = slice(gi * POOL_GROUP, (gi + 1) * POOL_GROUP)
        pg = p[:, ls]
        sm = pg
        for sh in range(1, win):
            sm = sm + pbuf[ppad - sh:ppad - sh + tm, ls]
        cnt = jnp.minimum(pos, float(win))
        pooled = sm / cnt - pg
        yg = _dot(pooled.astype(BF16), wp_ref[gi]) * ps[:, ls]
        acc = acc + _dot(yg.astype(BF16),
                         wo_ref[CONV_WIDTH + gi * POOL_GROUP:CONV_WIDTH + (gi + 1) * POOL_GROUP, :])
    pbuf[0:ppad, :] = pbuf[tm:tm + ppad, :]

    x1_ref[...] = acc
    xn2 = _rms_rows(acc, gm_ref[...]).astype(BF16)
    xn_ref[...] = xn2
    comb_ref[...] = _router(xn2, wr_ref, br_ref)


def _odd(x2, g, w, cw, wp, ps, wo, gm, wr, br, seq):
    t = x2.shape[0]
    n = t // TM
    full = lambda arr: pl.BlockSpec(arr.shape, lambda i: (0,) * arr.ndim)
    row = lambda wdt: pl.BlockSpec((TM, wdt), lambda i: (i, 0))
    return pl.pallas_call(
        functools.partial(_odd_kernel, seq // TM),
        grid=(n,),
        in_specs=[row(D_MODEL), full(g), full(w), full(cw), full(wp), full(ps), full(wo),
                  full(gm), full(wr), full(br)],
        out_specs=[row(D_MODEL), row(D_MODEL), row(LANES)],
        out_shape=[jax.ShapeDtypeStruct((t, D_MODEL), F32),
                   jax.ShapeDtypeStruct((t, D_MODEL), BF16),
                   jax.ShapeDtypeStruct((t, LANES), F32)],
        scratch_shapes=[pltpu.VMEM((TM + 8, CONV_WIDTH), F32),
                        pltpu.VMEM((TM + 16, POOL_WIDTH), F32)],
        compiler_params=_cparams(("arbitrary",)),
        name="odd_mixer",
    )(x2, g, w, cw, wp, ps, wo, gm, wr, br)


def _moe_kernel(xn_ref, wgu_ref, wd_ref, comb_ref, x1_ref, o_ref):
    e = pl.program_id(1)

    @pl.when(e == 0)
    def _():
        o_ref[...] = x1_ref[...]

    tm = xn_ref.shape[0]
    h = _dot(xn_ref[...], wgu_ref[0])
    gate = h[:, 0:EXPERT_HIDDEN]
    up = h[:, EXPERT_HIDDEN:2 * EXPERT_HIDDEN]
    lane = lax.broadcasted_iota(jnp.int32, (tm, LANES), 1)
    cw = jnp.sum(jnp.where(lane == e, comb_ref[...], 0.0), axis=1, keepdims=True)
    hid = (gate * jax.nn.sigmoid(gate)) * up * cw
    o_ref[...] += _dot(hid.astype(BF16), wd_ref[0])


def _moe(xn, wgu, wd, comb, x1):
    t = xn.shape[0]
    n = t // TM_MOE
    return pl.pallas_call(
        _moe_kernel,
        grid=(n, N_EXPERTS),
        in_specs=[pl.BlockSpec((TM_MOE, D_MODEL), lambda i, e: (i, 0)),
                  pl.BlockSpec((1, D_MODEL, 2 * EXPERT_HIDDEN), lambda i, e: (e, 0, 0)),
                  pl.BlockSpec((1, EXPERT_HIDDEN, D_MODEL), lambda i, e: (e, 0, 0)),
                  pl.BlockSpec((TM_MOE, LANES), lambda i, e: (i, 0)),
                  pl.BlockSpec((TM_MOE, D_MODEL), lambda i, e: (i, 0))],
        out_specs=pl.BlockSpec((TM_MOE, D_MODEL), lambda i, e: (i, 0)),
        out_shape=jax.ShapeDtypeStruct((t, D_MODEL), F32),
        compiler_params=_cparams(("parallel", "arbitrary")),
        name="moe",
    )(xn, wgu, wd, comb, x1)


def _router_params(w_group, b_group, w_router, b_router):
    wr = jnp.concatenate([w_router.reshape(D_MODEL, N_EXPERTS), w_group], axis=1)
    wr = jnp.pad(wr, ((0, 0), (0, LANES - wr.shape[1]))).astype(BF16)
    br = jnp.concatenate([b_router.reshape(N_EXPERTS), b_group])
    br = jnp.pad(br, (0, LANES - br.shape[0])).reshape(1, LANES).astype(F32)
    return wr, br


def _moe_params(w_gate, w_up, w_down):
    return jnp.concatenate([w_gate, w_up], axis=-1).astype(BF16), w_down.astype(BF16)


def kernel(x, ev_norm, ev_w_in, ev_b_forget, ev_w_s, ev_b_s, ev_g_v, ev_g_q, ev_g_k, ev_w_out,
           od_norm, od_w_in, od_conv_w, od_w_pool, od_pool_scale, od_w_out,
           moe_norm, moe_w_group, moe_b_group, moe_w_router, moe_b_router,
           moe_w_gate, moe_w_up, moe_w_down):
    bsz, seq, d = x.shape
    t = bsz * seq
    x2 = x.reshape(t, d)

    n_main = 2 * GMLP_WIDTH + 3 * FOX_WIDTH
    w_f = ev_w_in[0][:, n_main:]
    w_in = jnp.concatenate([ev_w_in[0][:, :n_main], w_f, w_f, w_f,
                            jnp.zeros((d, LANES - 3 * FOX_HEADS), F32)], axis=1).astype(BF16)
    bf = jnp.pad(jnp.tile(ev_b_forget[0], 3), (0, LANES - 3 * FOX_HEADS)).reshape(1, LANES)
    tril = jnp.tril(jnp.ones((CHUNK, CHUNK), F32))
    wtril = (ev_w_s[0] * tril).astype(BF16)
    bmat = jnp.repeat(ev_b_s[0].T, GMLP_HEAD, axis=1)
    gv = ev_g_v[0].reshape(1, GMLP_WIDTH)
    gq = jnp.tile(ev_g_q[0], FOX_HEADS).reshape(1, FOX_WIDTH)
    gk = jnp.tile(ev_g_k[0], FOX_HEADS).reshape(1, FOX_WIDTH)
    blk = jnp.arange(256) // HEAD_DIM
    bd = jnp.where(blk[:, None] == blk[None, :], 1.0 / HEAD_DIM, 0.0).astype(BF16)
    ltri = jnp.tril(jnp.ones((TM, TM), F32)).astype(BF16)

    a, qt, kn, vt, f3, fq = _even_in(x2, ev_norm[0].reshape(1, d), w_in, bf, wtril, bmat, gv, gq,
                                     gk, bd, ltri, seq)
    bound = (HEAD_DIM ** 0.5 * LOG2E * 1.01) * jnp.max(jnp.abs(ev_g_q[0])) * jnp.max(
        jnp.abs(ev_g_k[0]))
    o = _fox(qt, kn, f3, vt, fq, bound, bsz, seq)

    wr0, br0 = _router_params(moe_w_group[0], moe_b_group[0], moe_w_router[0], moe_b_router[0])
    w_out0 = ev_w_out[0].astype(BF16)
    x1, xn1, comb1 = _even_out(a, o, x2, w_out0[:GMLP_WIDTH], w_out0[GMLP_WIDTH:],
                               moe_norm[0].reshape(1, d), wr0, br0)
    wgu0, wd0 = _moe_params(moe_w_gate[0], moe_w_up[0], moe_w_down[0])
    xa = _moe(xn1, wgu0, wd0, comb1, x1)

    wr1, br1 = _router_params(moe_w_group[1], moe_b_group[1], moe_w_router[1], moe_b_router[1])
    x3, xn3, comb3 = _odd(xa, od_norm[0].reshape(1, d), od_w_in[0].astype(BF16), od_conv_w[0],
                          od_w_pool[0].astype(BF16), od_pool_scale[0].reshape(1, POOL_WIDTH),
                          od_w_out[0].astype(BF16), moe_norm[1].reshape(1, d), wr1, br1, seq)
    wgu1, wd1 = _moe_params(moe_w_gate[1], moe_w_up[1], moe_w_down[1])
    xb = _moe(xn3, wgu1, wd1, comb3, x3)
    return xb.reshape(bsz, seq, d)
```

```python
import functools

import jax
import jax.numpy as jnp
from jax import lax
from jax.experimental import pallas as pl
from jax.experimental.pallas import tpu as pltpu

F32 = jnp.float32
BF16 = jnp.bfloat16

D_MODEL = 1024
EPS = 1e-6
CHUNK = 128
GMLP_GROUPS = 4
GMLP_HEAD = 64
GMLP_WIDTH = 256
FOX_HEADS = 12
HEAD_DIM = 64
FOX_WIDTH = 768
CONV_WIDTH = 512
CONV_K = 3
POOL_WINDOWS = (2, 4, 8, 16)
POOL_GROUP = 128
POOL_WIDTH = 512
N_GROUPS = 4
EXPERTS_PER_GROUP = 4
N_EXPERTS = 16
EXPERT_HIDDEN = 256

LANES = 128
V7X_VMEM_LIMIT_BYTES = 56 * 1024 * 1024

TM = 512
TQ = 512
TK = 512
TQ_ONLINE = 256
TM_MOE = 512
EVEN_PROJ_PAD = 2944
NEG = -1e30
LOG2E = 1.4426950408889634
FOX_FAST_BOUND_LOG2 = 50.0


def _cparams(sem):
    return pltpu.CompilerParams(dimension_semantics=sem,
                                vmem_limit_bytes=V7X_VMEM_LIMIT_BYTES)


def _dot(a, b):
    return jnp.dot(a, b, preferred_element_type=F32)


def _split_bf16(x, terms):
    parts = []
    r = x
    for _ in range(terms):
        p = r.astype(BF16)
        parts.append(p)
        r = r - p.astype(F32)
    return parts


def _dot_split(x, w_bf16, terms=2, w_left=False):
    acc = None
    for p in _split_bf16(x, terms):
        d = _dot(w_bf16, p) if w_left else _dot(p, w_bf16)
        acc = d if acc is None else acc + d
    return acc


def _rms_rows(x, g):
    ms = jnp.mean(x * x, axis=-1, keepdims=True)
    return x * lax.rsqrt(ms + EPS) * g


def _head_rms(x, bd, g):
    outs = []
    for c in range(x.shape[1] // 256):
        xc = x[:, c * 256:(c + 1) * 256]
        ms = _dot_split(xc * xc, bd)
        outs.append(xc * lax.rsqrt(ms + EPS))
    y = outs[0] if len(outs) == 1 else jnp.concatenate(outs, axis=1)
    return y * g


def _log_sigmoid(x):
    return -(jnp.maximum(-x, 0.0) + jnp.log(1.0 + jnp.exp(-jnp.abs(x))))


def _router(xn_bf16, wr_ref, br_ref):
    r = _dot(xn_bf16, wr_ref[...]) + br_ref[...]
    tm = r.shape[0]
    lane = lax.broadcasted_iota(jnp.int32, (tm, LANES), 1).astype(F32)
    is_g = (lane >= float(N_EXPERTS)) & (lane < float(N_EXPERTS + N_GROUPS))
    gl = jnp.where(is_g, r, NEG)
    gmax = jnp.max(gl, axis=1, keepdims=True)
    gidx = jnp.min(jnp.where(gl == gmax, lane, 999.0), axis=1, keepdims=True) - float(N_EXPERTS)
    gsum = jnp.sum(jnp.where(is_g, jnp.exp(gl - gmax), 0.0), axis=1, keepdims=True)
    gp = 1.0 / gsum
    lo = gidx * float(EXPERTS_PER_GROUP)
    sel = (lane >= lo) & (lane < lo + float(EXPERTS_PER_GROUP))
    el = jnp.where(sel, r, NEG)
    emax = jnp.max(el, axis=1, keepdims=True)
    ee = jnp.where(sel, jnp.exp(el - emax), 0.0)
    ep = ee / jnp.sum(ee, axis=1, keepdims=True)
    ep = jnp.where(sel, ep, -1.0)
    p1 = jnp.max(ep, axis=1, keepdims=True)
    i1 = jnp.min(jnp.where(ep == p1, lane, 999.0), axis=1, keepdims=True)
    ep2 = jnp.where(lane == i1, -1.0, ep)
    p2 = jnp.max(ep2, axis=1, keepdims=True)
    i2 = jnp.min(jnp.where(ep2 == p2, lane, 999.0), axis=1, keepdims=True)
    den = p1 + p2
    return (jnp.where(lane == i1, gp * (p1 / den), 0.0)
            + jnp.where(lane == i2, gp * (p2 / den), 0.0))


def _even_in_kernel(tiles_per_seq, x_ref, g_ref, w_ref, bf_ref, wtril_ref, bmat_ref, gv_ref,
                    gq_ref, gk_ref, bd_ref, ltri_ref,
                    a_ref, q_ref, k_ref, v_ref, f_ref, fq_ref, carry_ref):
    i = pl.program_id(0)
    tm = x_ref.shape[0]
    xn = _rms_rows(x_ref[...], g_ref[...]).astype(BF16)
    h = _dot(xn, w_ref[...])
    u = h[:, 0:256]
    v = h[:, 256:512]
    q = h[:, 512:1280]
    k = h[:, 1280:2048]
    val = h[:, 2048:2816]
    f = h[:, 2816:2944]

    gu = jax.nn.gelu(u)
    vn = _head_rms(jax.nn.gelu(v), bd_ref[...], gv_ref[...])
    lane = lax.broadcasted_iota(jnp.int32, (tm, LANES), 1)
    lo_half = lane < GMLP_HEAD
    bmat = bmat_ref[...]
    pair_out = []
    for pr in range(2):
        vp = vn[:, pr * 128:(pr + 1) * 128]
        v_lo = jnp.where(lo_half, vp, 0.0).astype(BF16)
        v_hi = jnp.where(lo_half, 0.0, vp).astype(BF16)
        chunks = []
        for c in range(tm // CHUNK):
            rs = slice(c * CHUNK, (c + 1) * CHUNK)
            s = (_dot(wtril_ref[2 * pr], v_lo[rs]) + _dot(wtril_ref[2 * pr + 1], v_hi[rs])
                 + bmat[:, pr * 128:(pr + 1) * 128])
            chunks.append(s)
        pair_out.append(jnp.concatenate(chunks, axis=0))
    s_all = jnp.concatenate(pair_out, axis=1)
    a_ref[...] = (gu * s_all).astype(BF16)

    bd = bd_ref[...]
    qn = _head_rms(q, bd, gq_ref[...]) * (HEAD_DIM ** -0.5 * LOG2E)
    for pr in range(FOX_HEADS // 2):
        ls = slice(pr * LANES, (pr + 1) * LANES)
        q_ref[pr] = qn[:, ls].T.astype(BF16)
        v_ref[pr] = val[:, ls].T.astype(BF16)
    k_ref[...] = _head_rms(k, bd, gk_ref[...]).astype(BF16)

    @pl.when(i % tiles_per_seq == 0)
    def _():
        carry_ref[...] = jnp.zeros_like(carry_ref)

    logf = _log_sigmoid(f + bf_ref[...])
    cum = _dot_split(logf, ltri_ref[...], terms=3, w_left=True) + carry_ref[0:1, :]
    carry_ref[0:1, :] = cum[tm - 1:tm, :]
    cum2 = cum * LOG2E
    fq_ref[...] = cum2
    hi, mid, lo = (p.astype(F32) for p in _split_bf16(cum2, 3))
    f_ref[...] = jnp.where(
        lane < FOX_HEADS, hi,
        jnp.where(lane < 2 * FOX_HEADS, mid,
                  jnp.where(lane < 3 * FOX_HEADS, lo,
                            jnp.where(lane < 3 * FOX_HEADS + 3, 1.0, 0.0)))).astype(BF16)


def _even_in(x2, g, w, bf, wtril, bmat, gv, gq, gk, bd, ltri, seq):
    t = x2.shape[0]
    n = t // TM
    full = lambda a: pl.BlockSpec(a.shape, lambda i: (0,) * a.ndim)
    row = lambda wdt: pl.BlockSpec((TM, wdt), lambda i: (i, 0))
    pairs = FOX_HEADS // 2
    colT = pl.BlockSpec((pairs, LANES, TM), lambda i: (0, 0, i))
    return pl.pallas_call(
        functools.partial(_even_in_kernel, seq // TM),
        grid=(n,),
        in_specs=[row(D_MODEL), full(g), full(w), full(bf), full(wtril), full(bmat), full(gv),
                  full(gq), full(gk), full(bd), full(ltri)],
        out_specs=[row(GMLP_WIDTH), colT, row(FOX_WIDTH), colT, row(LANES), row(LANES)],
        out_shape=[jax.ShapeDtypeStruct((t, GMLP_WIDTH), BF16),
                   jax.ShapeDtypeStruct((pairs, LANES, t), BF16),
                   jax.ShapeDtypeStruct((t, FOX_WIDTH), BF16),
                   jax.ShapeDtypeStruct((pairs, LANES, t), BF16),
                   jax.ShapeDtypeStruct((t, LANES), BF16),
                   jax.ShapeDtypeStruct((t, LANES), F32)],
        scratch_shapes=[pltpu.VMEM((8, LANES), F32)],
        compiler_params=_cparams(("arbitrary",)),
        name="even_in",
    )(x2, g, w, bf, wtril, bmat, gv, gq, gk, bd, ltri)


def _fox_query_operand(qt, pr, hh, shift_terms=None):
    tq = qt.shape[1]
    row = lax.broadcasted_iota(jnp.int32, (LANES, tq), 0)
    head = 2 * pr + hh
    in_head = (row >= hh * HEAD_DIM) & (row < (hh + 1) * HEAD_DIM)
    f_rows = (row == head) | (row == head + FOX_HEADS) | (row == head + 2 * FOX_HEADS)
    extra = jnp.where(f_rows, -1.0, 0.0)
    if shift_terms is not None:
        for n, term in enumerate(shift_terms):
            extra = jnp.where(row == 3 * FOX_HEADS + n, term, extra)
    return jnp.concatenate([jnp.where(in_head, qt, 0.0).astype(BF16), extra.astype(BF16)],
                           axis=0)


def _fox_online_kernel(q_ref, k_ref, f_ref, v_ref, o_ref, m_sc, l_sc, acc_sc):
    pr = pl.program_id(1)
    i = pl.program_id(2)
    tq = q_ref.shape[2]
    qt = q_ref[0].astype(F32)
    rhs = [_fox_query_operand(qt, pr, hh) for hh in range(2)]

    m_sc[...] = jnp.full_like(m_sc, NEG)
    l_sc[...] = jnp.zeros_like(l_sc)
    acc_sc[...] = jnp.zeros_like(acc_sc)

    def step(j, masked):
        start = pl.multiple_of(j * tq, tq)
        kaug = jnp.concatenate([k_ref[pl.ds(start, tq), :], f_ref[pl.ds(start, tq), :]],
                               axis=1)
        for hh in range(2):
            s = _dot(kaug, rhs[hh])
            if masked:
                r_id = lax.broadcasted_iota(jnp.int32, (tq, tq), 0)
                c_id = lax.broadcasted_iota(jnp.int32, (tq, tq), 1)
                s = jnp.where(r_id <= c_id, s, NEG)
            m_prev = m_sc[hh]
            m_new = jnp.maximum(m_prev, jnp.max(s, axis=0, keepdims=True))
            alpha = jnp.exp2(m_prev - m_new)
            p = jnp.exp2(s - m_new)
            l_sc[hh] = alpha * l_sc[hh] + jnp.sum(p, axis=0, keepdims=True)
            vt = v_ref[0, hh * HEAD_DIM:(hh + 1) * HEAD_DIM, pl.ds(start, tq)]
            acc_sc[hh] = alpha * acc_sc[hh] + _dot(vt, p.astype(BF16))
            m_sc[hh] = m_new

    def body(j, c):
        step(j, False)
        return c

    lax.fori_loop(0, i, body, 0)
    step(i, True)
    ot = jnp.concatenate([acc_sc[0] / l_sc[0], acc_sc[1] / l_sc[1]], axis=0)
    o_ref[...] = ot.T.astype(o_ref.dtype)


def _fox_fixed_kernel(b_ref, q_ref, k_ref, f_ref, v_ref, fq_ref, o_ref,
                      rhs_sc, z_sc, acc_sc):
    pr = pl.program_id(1)
    i = pl.program_id(2)
    tq = q_ref.shape[2]
    tk = z_sc.shape[2]
    qt = q_ref[0].astype(F32)
    fqt = fq_ref[...].T
    row = lax.broadcasted_iota(jnp.int32, (LANES, tq), 0)
    for hh in range(2):
        ft = jnp.sum(jnp.where(row == 2 * pr + hh, fqt, 0.0), axis=0, keepdims=True)
        shift = [p.astype(F32) for p in _split_bf16(ft - b_ref[...], 3)]
        rhs_sc[hh] = _fox_query_operand(qt, pr, hh, shift)

    acc_sc[...] = jnp.zeros_like(acc_sc)
    ones = jnp.ones((acc_sc.shape[1] - HEAD_DIM, tk), BF16)

    def scores(j, slot):
        start = pl.multiple_of(j * tk, tk)
        kaug = jnp.concatenate([k_ref[pl.ds(start, tk), :], f_ref[pl.ds(start, tk), :]],
                               axis=1)
        for hh in range(2):
            z_sc[slot, hh] = _dot(kaug, rhs_sc[hh])

    def consume(j, slot, masked):
        start = pl.multiple_of(j * tk, tk)
        for hh in range(2):
            z = z_sc[slot, hh]
            if masked:
                s_id = start + lax.broadcasted_iota(jnp.int32, (tk, tq), 0)
                t_id = i * tq + lax.broadcasted_iota(jnp.int32, (tk, tq), 1)
                z = jnp.where(s_id <= t_id, z, NEG)
            p = jnp.exp2(z).astype(BF16)
            vt = jnp.concatenate(
                [v_ref[0, hh * HEAD_DIM:(hh + 1) * HEAD_DIM, pl.ds(start, tk)], ones], axis=0)
            acc_sc[hh] += _dot(vt, p)

    n_blocks = (i * tq + tq + tk - 1) // tk
    n_pairs = (n_blocks - 1) // 2
    scores(0, 0)

    def body(kk, c):
        j = 2 * kk
        scores(j + 1, 1)
        consume(j, 0, False)
        scores(j + 2, 0)
        consume(j + 1, 1, False)
        return c

    lax.fori_loop(0, n_pairs, body, 0)
    j_rest = 2 * n_pairs

    @pl.when(n_blocks - j_rest == 1)
    def _():
        consume(j_rest, 0, True)

    @pl.when(n_blocks - j_rest == 2)
    def _():
        scores(j_rest + 1, 1)
        consume(j_rest, 0, False)
        consume(j_rest + 1, 1, True)

    outs = []
    for hh in range(2):
        acc = acc_sc[hh]
        outs.append(acc[0:HEAD_DIM] / acc[HEAD_DIM:HEAD_DIM + 1])
    o_ref[...] = jnp.concatenate(outs, axis=0).T.astype(o_ref.dtype)


def _fox(qt, k, f3, vt, fq, bound, bsz, seq):
    t = k.shape[0]
    pairs = FOX_HEADS // 2
    out_shape = jax.ShapeDtypeStruct((t, FOX_WIDTH), BF16)
    sem = _cparams(("parallel", "parallel", "arbitrary"))

    def common(tq):
        nq = seq // tq
        in_specs = [pl.BlockSpec((1, LANES, tq), lambda b, p, i: (p, 0, b * nq + i)),
                    pl.BlockSpec((seq, LANES), lambda b, p, i: (b, p)),
                    pl.BlockSpec((seq, LANES), lambda b, p, i: (b, 0)),
                    pl.BlockSpec((1, LANES, seq), lambda b, p, i: (p, 0, b))]
        return nq, in_specs, pl.BlockSpec((tq, LANES), lambda b, p, i: (b * nq + i, p))

    def fixed(brow):
        nq, in_specs, out_spec = common(TQ)
        return pl.pallas_call(
            _fox_fixed_kernel,
            grid=(bsz, pairs, nq),
            in_specs=[pl.BlockSpec((1, TQ), lambda b, p, i: (0, 0))] + in_specs
            + [pl.BlockSpec((TQ, LANES), lambda b, p, i: (b * nq + i, 0))],
            out_specs=out_spec, out_shape=out_shape,
            scratch_shapes=[pltpu.VMEM((2, 2 * LANES, TQ), BF16),
                            pltpu.VMEM((2, 2, TK, TQ), F32),
                            pltpu.VMEM((2, HEAD_DIM + 16, TQ), F32)],
            compiler_params=sem, name="fox_attention_fixed",
        )(brow, qt, k, f3, vt, fq)

    def online(brow):
        del brow
        nq, in_specs, out_spec = common(TQ_ONLINE)
        stats = pltpu.VMEM((2, 1, TQ_ONLINE), F32)
        return pl.pallas_call(
            _fox_online_kernel,
            grid=(bsz, pairs, nq),
            in_specs=in_specs, out_specs=out_spec, out_shape=out_shape,
            scratch_shapes=[stats, stats, pltpu.VMEM((2, HEAD_DIM, TQ_ONLINE), F32)],
            compiler_params=sem, name="fox_attention_online",
        )(qt, k, f3, vt)

    brow = jnp.full((1, TQ), bound, F32)
    return lax.cond(bound <= FOX_FAST_BOUND_LOG2, fixed, online, brow)


def _even_out_kernel(a_ref, o_ref, x_ref, wa_ref, wo_ref, gm_ref, wr_ref, br_ref,
                     x1_ref, xn_ref, comb_ref):
    x1 = x_ref[...] + _dot(a_ref[...], wa_ref[...]) + _dot(o_ref[...], wo_ref[...])
    x1_ref[...] = x1
    xn = _rms_rows(x1, gm_ref[...]).astype(BF16)
    xn_ref[...] = xn
    comb_ref[...] = _router(xn, wr_ref, br_ref)


def _even_out(a, o, x2, wa, wo, gm, wr, br):
    t = x2.shape[0]
    n = t // TM
    full = lambda arr: pl.BlockSpec(arr.shape, lambda i: (0,) * arr.ndim)
    row = lambda wdt: pl.BlockSpec((TM, wdt), lambda i: (i, 0))
    return pl.pallas_call(
        _even_out_kernel,
        grid=(n,),
        in_specs=[row(GMLP_WIDTH), row(FOX_WIDTH), row(D_MODEL), full(wa), full(wo), full(gm),
                  full(wr), full(br)],
        out_specs=[row(D_MODEL), row(D_MODEL), row(LANES)],
        out_shape=[jax.ShapeDtypeStruct((t, D_MODEL), F32),
                   jax.ShapeDtypeStruct((t, D_MODEL), BF16),
                   jax.ShapeDtypeStruct((t, LANES), F32)],
        compiler_params=_cparams(("parallel",)),
        name="even_out",
    )(a, o, x2, wa, wo, gm, wr, br)


def _odd_kernel(tiles_per_seq, x_ref, g_ref, w_ref, cw_ref, wp_ref, ps_ref, wo_ref, gm_ref,
                wr_ref, br_ref, x1_ref, xn_ref, comb_ref, zbuf, pbuf):
    i = pl.program_id(0)
    tm = x_ref.shape[0]
    zpad = zbuf.shape[0] - tm
    ppad = pbuf.shape[0] - tm
    x = x_ref[...]
    xn = _rms_rows(x, g_ref[...]).astype(BF16)
    h = _dot(xn, w_ref[...])
    bg = h[:, 0:512]
    cg = h[:, 512:1024]
    hc = h[:, 1024:1536]
    p = h[:, 1536:2048]

    @pl.when(i % tiles_per_seq == 0)
    def _():
        zbuf[0:zpad, :] = jnp.zeros((zpad, CONV_WIDTH), F32)
        pbuf[0:ppad, :] = jnp.zeros((ppad, POOL_WIDTH), F32)

    z = cg * hc
    zbuf[zpad:zpad + tm, :] = z
    cw = cw_ref[...]
    y = (cw[0:1, :] * zbuf[zpad - 2:zpad - 2 + tm, :]
         + cw[1:2, :] * zbuf[zpad - 1:zpad - 1 + tm, :]
         + cw[2:3, :] * z)
    c = bg * y
    zbuf[0:zpad, :] = zbuf[tm:tm + zpad, :]

    pbuf[ppad:ppad + tm, :] = p
    pos = ((i % tiles_per_seq) * tm
           + lax.broadcasted_iota(jnp.int32, (tm, POOL_GROUP), 0)).astype(F32) + 1.0
    ps = ps_ref[...]
    pooled_out = []
    for gi, win in enumerate(POOL_WINDOWS):
        ls = slice(gi * POOL_GROUP, (gi + 1) * POOL_GROUP)
        pg = p[:, ls]
        sm = pg
        for sh in range(1, win):
            sm = sm + pbuf[ppad - sh:ppad - sh + tm, ls]
        cnt = jnp.minimum(pos, float(win))
        pooled = sm / cnt - pg
        pooled_out.append((_dot(pooled.astype(BF16), wp_ref[gi]) * ps[:, ls]).astype(BF16))
    pbuf[0:ppad, :] = pbuf[tm:tm + ppad, :]
    mix = jnp.concatenate([c.astype(BF16)] + pooled_out, axis=1)
    acc = x + _dot(mix, wo_ref[...])

    x1_ref[...] = acc
    xn2 = _rms_rows(acc, gm_ref[...]).astype(BF16)
    xn_ref[...] = xn2
    comb_ref[...] = _router(xn2, wr_ref, br_ref)


def _odd(x2, g, w, cw, wp, ps, wo, gm, wr, br, seq):
    t = x2.shape[0]
    n = t // TM
    full = lambda arr: pl.BlockSpec(arr.shape, lambda i: (0,) * arr.ndim)
    row = lambda wdt: pl.BlockSpec((TM, wdt), lambda i: (i, 0))
    return pl.pallas_call(
        functools.partial(_odd_kernel, seq // TM),
        grid=(n,),
        in_specs=[row(D_MODEL), full(g), full(w), full(cw), full(wp), full(ps), full(wo),
                  full(gm), full(wr), full(br)],
        out_specs=[row(D_MODEL), row(D_MODEL), row(LANES)],
        out_shape=[jax.ShapeDtypeStruct((t, D_MODEL), F32),
                   jax.ShapeDtypeStruct((t, D_MODEL), BF16),
                   jax.ShapeDtypeStruct((t, LANES), F32)],
        scratch_shapes=[pltpu.VMEM((TM + 8, CONV_WIDTH), F32),
                        pltpu.VMEM((TM + 16, POOL_WIDTH), F32)],
        compiler_params=_cparams(("arbitrary",)),
        name="odd_mixer",
    )(x2, g, w, cw, wp, ps, wo, gm, wr, br)


def _moe_kernel(xn_ref, wg_ref, wu_ref, wd_ref, comb_ref, x1_ref, o_ref):
    g = pl.program_id(1)

    @pl.when(g == 0)
    def _():
        o_ref[...] = x1_ref[...]

    tm = xn_ref.shape[0]
    xn = xn_ref[...]
    lane = lax.broadcasted_iota(jnp.int32, (tm, LANES), 1)
    comb = comb_ref[...]
    hid = []
    for e in range(EXPERTS_PER_GROUP):
        gate = _dot(xn, wg_ref[e])
        up = _dot(xn, wu_ref[e])
        cw = jnp.sum(jnp.where(lane == g * EXPERTS_PER_GROUP + e, comb, 0.0), axis=1,
                     keepdims=True)
        hid.append(((gate * jax.nn.sigmoid(gate)) * up * cw).astype(BF16))
    o_ref[...] += _dot(jnp.concatenate(hid, axis=1), wd_ref[0])


def _moe(xn, wg, wu, wd, comb, x1):
    t = xn.shape[0]
    n = t // TM_MOE
    gh = EXPERTS_PER_GROUP * EXPERT_HIDDEN
    wblk = pl.BlockSpec((EXPERTS_PER_GROUP, D_MODEL, EXPERT_HIDDEN), lambda i, g: (g, 0, 0))
    return pl.pallas_call(
        _moe_kernel,
        grid=(n, N_GROUPS),
        in_specs=[pl.BlockSpec((TM_MOE, D_MODEL), lambda i, g: (i, 0)),
                  wblk, wblk,
                  pl.BlockSpec((1, gh, D_MODEL), lambda i, g: (g, 0, 0)),
                  pl.BlockSpec((TM_MOE, LANES), lambda i, g: (i, 0)),
                  pl.BlockSpec((TM_MOE, D_MODEL), lambda i, g: (i, 0))],
        out_specs=pl.BlockSpec((TM_MOE, D_MODEL), lambda i, g: (i, 0)),
        out_shape=jax.ShapeDtypeStruct((t, D_MODEL), F32),
        compiler_params=_cparams(("parallel", "arbitrary")),
        name="moe",
    )(xn, wg, wu, wd, comb, x1)


def _router_params(w_group, b_group, w_router, b_router):
    wr = jnp.concatenate([w_router.reshape(D_MODEL, N_EXPERTS), w_group], axis=1)
    wr = jnp.pad(wr, ((0, 0), (0, LANES - wr.shape[1]))).astype(BF16)
    br = jnp.concatenate([b_router.reshape(N_EXPERTS), b_group])
    br = jnp.pad(br, (0, LANES - br.shape[0])).reshape(1, LANES).astype(F32)
    return wr, br


def _moe_params(w_gate, w_up, w_down):
    wd = w_down.astype(BF16).reshape(N_GROUPS, EXPERTS_PER_GROUP * EXPERT_HIDDEN, D_MODEL)
    return w_gate.astype(BF16), w_up.astype(BF16), wd


def kernel(x, ev_norm, ev_w_in, ev_b_forget, ev_w_s, ev_b_s, ev_g_v, ev_g_q, ev_g_k, ev_w_out,
           od_norm, od_w_in, od_conv_w, od_w_pool, od_pool_scale, od_w_out,
           moe_norm, moe_w_group, moe_b_group, moe_w_router, moe_b_router,
           moe_w_gate, moe_w_up, moe_w_down):
    bsz, seq, d = x.shape
    t = bsz * seq
    x2 = x.reshape(t, d)

    n_main = 2 * GMLP_WIDTH + 3 * FOX_WIDTH
    w_f = ev_w_in[0][:, n_main:]
    w_in = jnp.concatenate([ev_w_in[0][:, :n_main], w_f, w_f, w_f,
                            jnp.zeros((d, LANES - 3 * FOX_HEADS), F32)], axis=1).astype(BF16)
    bf = jnp.pad(jnp.tile(ev_b_forget[0], 3), (0, LANES - 3 * FOX_HEADS)).reshape(1, LANES)
    tril = jnp.tril(jnp.ones((CHUNK, CHUNK), F32))
    wtril = (ev_w_s[0] * tril).astype(BF16)
    bmat = jnp.repeat(ev_b_s[0].T, GMLP_HEAD, axis=1)
    gv = ev_g_v[0].reshape(1, GMLP_WIDTH)
    gq = jnp.tile(ev_g_q[0], FOX_HEADS).reshape(1, FOX_WIDTH)
    gk = jnp.tile(ev_g_k[0], FOX_HEADS).reshape(1, FOX_WIDTH)
    blk = jnp.arange(256) // HEAD_DIM
    bd = jnp.where(blk[:, None] == blk[None, :], 1.0 / HEAD_DIM, 0.0).astype(BF16)
    ltri = jnp.tril(jnp.ones((TM, TM), F32)).astype(BF16)

    a, qt, kn, vt, f3, fq = _even_in(x2, ev_norm[0].reshape(1, d), w_in, bf, wtril, bmat, gv, gq,
                                     gk, bd, ltri, seq)
    bound = (HEAD_DIM ** 0.5 * LOG2E * 1.01) * jnp.max(jnp.abs(ev_g_q[0])) * jnp.max(
        jnp.abs(ev_g_k[0]))
    o = _fox(qt, kn, f3, vt, fq, bound, bsz, seq)

    wr0, br0 = _router_params(moe_w_group[0], moe_b_group[0], moe_w_router[0], moe_b_router[0])
    w_out0 = ev_w_out[0].astype(BF16)
    x1, xn1, comb1 = _even_out(a, o, x2, w_out0[:GMLP_WIDTH], w_out0[GMLP_WIDTH:],
                               moe_norm[0].reshape(1, d), wr0, br0)
    xa = _moe(xn1, *_moe_params(moe_w_gate[0], moe_w_up[0], moe_w_down[0]), comb1, x1)

    wr1, br1 = _router_params(moe_w_group[1], moe_b_group[1], moe_w_router[1], moe_b_router[1])
    x3, xn3, comb3 = _odd(xa, od_norm[0].reshape(1, d), od_w_in[0].astype(BF16), od_conv_w[0],
                          od_w_pool[0].astype(BF16), od_pool_scale[0].reshape(1, POOL_WIDTH),
                          od_w_out[0].astype(BF16), moe_norm[1].reshape(1, d), wr1, br1, seq)
    xb = _moe(xn3, *_moe_params(moe_w_gate[1], moe_w_up[1], moe_w_down[1]), comb3, x3)
    return xb.reshape(bsz, seq, d)
```

```python
import functools

import jax
import jax.numpy as jnp
from jax import lax
from jax.experimental import pallas as pl
from jax.experimental.pallas import tpu as pltpu

F32 = jnp.float32
BF16 = jnp.bfloat16

D_MODEL = 1024
EPS = 1e-6
CHUNK = 128
GMLP_GROUPS = 4
GMLP_HEAD = 64
GMLP_WIDTH = 256
FOX_HEADS = 12
HEAD_DIM = 64
FOX_WIDTH = 768
CONV_WIDTH = 512
CONV_K = 3
POOL_WINDOWS = (2, 4, 8, 16)
POOL_GROUP = 128
POOL_WIDTH = 512
N_GROUPS = 4
EXPERTS_PER_GROUP = 4
N_EXPERTS = 16
EXPERT_HIDDEN = 256

LANES = 128
V7X_VMEM_LIMIT_BYTES = 56 * 1024 * 1024

TM = 512
TQ = 512
TQ_ONLINE = 256
TM_MOE = 512
EVEN_PROJ_PAD = 2944
NEG = -1e30
LOG2E = 1.4426950408889634
FOX_FAST_BOUND_LOG2 = 50.0


def _cparams(sem):
    return pltpu.CompilerParams(dimension_semantics=sem,
                                vmem_limit_bytes=V7X_VMEM_LIMIT_BYTES)


def _dot(a, b):
    return jnp.dot(a, b, preferred_element_type=F32)


def _split_bf16(x, terms):
    parts = []
    r = x
    for _ in range(terms):
        p = r.astype(BF16)
        parts.append(p)
        r = r - p.astype(F32)
    return parts


def _dot_split(x, w_bf16, terms=2, w_left=False):
    acc = None
    for p in _split_bf16(x, terms):
        d = _dot(w_bf16, p) if w_left else _dot(p, w_bf16)
        acc = d if acc is None else acc + d
    return acc


def _rms_rows(x, g):
    ms = jnp.mean(x * x, axis=-1, keepdims=True)
    return x * lax.rsqrt(ms + EPS) * g


def _head_rms(x, bd, g):
    outs = []
    for c in range(x.shape[1] // 256):
        xc = x[:, c * 256:(c + 1) * 256]
        ms = _dot_split(xc * xc, bd)
        outs.append(xc * lax.rsqrt(ms + EPS))
    y = outs[0] if len(outs) == 1 else jnp.concatenate(outs, axis=1)
    return y * g


def _log_sigmoid(x):
    return -(jnp.maximum(-x, 0.0) + jnp.log(1.0 + jnp.exp(-jnp.abs(x))))


def _router(xn_bf16, wr_ref, br_ref):
    r = _dot(xn_bf16, wr_ref[...]) + br_ref[...]
    tm = r.shape[0]
    lane = lax.broadcasted_iota(jnp.int32, (tm, LANES), 1).astype(F32)
    is_g = (lane >= float(N_EXPERTS)) & (lane < float(N_EXPERTS + N_GROUPS))
    gl = jnp.where(is_g, r, NEG)
    gmax = jnp.max(gl, axis=1, keepdims=True)
    gidx = jnp.min(jnp.where(gl == gmax, lane, 999.0), axis=1, keepdims=True) - float(N_EXPERTS)
    gsum = jnp.sum(jnp.where(is_g, jnp.exp(gl - gmax), 0.0), axis=1, keepdims=True)
    gp = 1.0 / gsum
    lo = gidx * float(EXPERTS_PER_GROUP)
    sel = (lane >= lo) & (lane < lo + float(EXPERTS_PER_GROUP))
    el = jnp.where(sel, r, NEG)
    emax = jnp.max(el, axis=1, keepdims=True)
    ee = jnp.where(sel, jnp.exp(el - emax), 0.0)
    ep = ee / jnp.sum(ee, axis=1, keepdims=True)
    ep = jnp.where(sel, ep, -1.0)
    p1 = jnp.max(ep, axis=1, keepdims=True)
    i1 = jnp.min(jnp.where(ep == p1, lane, 999.0), axis=1, keepdims=True)
    ep2 = jnp.where(lane == i1, -1.0, ep)
    p2 = jnp.max(ep2, axis=1, keepdims=True)
    i2 = jnp.min(jnp.where(ep2 == p2, lane, 999.0), axis=1, keepdims=True)
    den = p1 + p2
    return (jnp.where(lane == i1, gp * (p1 / den), 0.0)
            + jnp.where(lane == i2, gp * (p2 / den), 0.0))


def _even_in_kernel(tiles_per_seq, x_ref, g_ref, w_ref, bf_ref, wtril_ref, bmat_ref, gv_ref,
                    gq_ref, gk_ref, bd_ref, ltri_ref,
                    a_ref, q_ref, k_ref, v_ref, f_ref, fq_ref, carry_ref):
    i = pl.program_id(0)
    tm = x_ref.shape[0]
    xn = _rms_rows(x_ref[...], g_ref[...]).astype(BF16)
    h = _dot(xn, w_ref[...])
    u = h[:, 0:256]
    v = h[:, 256:512]
    q = h[:, 512:1280]
    k = h[:, 1280:2048]
    val = h[:, 2048:2816]
    f = h[:, 2816:2944]

    gu = jax.nn.gelu(u)
    vn = _head_rms(jax.nn.gelu(v), bd_ref[...], gv_ref[...])
    lane = lax.broadcasted_iota(jnp.int32, (tm, LANES), 1)
    lo_half = lane < GMLP_HEAD
    bmat = bmat_ref[...]
    pair_out = []
    for pr in range(2):
        vp = vn[:, pr * 128:(pr + 1) * 128]
        v_lo = jnp.where(lo_half, vp, 0.0).astype(BF16)
        v_hi = jnp.where(lo_half, 0.0, vp).astype(BF16)
        chunks = []
        for c in range(tm // CHUNK):
            rs = slice(c * CHUNK, (c + 1) * CHUNK)
            s = (_dot(wtril_ref[2 * pr], v_lo[rs]) + _dot(wtril_ref[2 * pr + 1], v_hi[rs])
                 + bmat[:, pr * 128:(pr + 1) * 128])
            chunks.append(s)
        pair_out.append(jnp.concatenate(chunks, axis=0))
    s_all = jnp.concatenate(pair_out, axis=1)
    a_ref[...] = (gu * s_all).astype(BF16)

    bd = bd_ref[...]
    qn = _head_rms(q, bd, gq_ref[...]) * (HEAD_DIM ** -0.5 * LOG2E)
    for pr in range(FOX_HEADS // 2):
        ls = slice(pr * LANES, (pr + 1) * LANES)
        q_ref[pr] = qn[:, ls].T.astype(BF16)
        v_ref[pr] = val[:, ls].T.astype(BF16)
    k_ref[...] = _head_rms(k, bd, gk_ref[...]).astype(BF16)

    @pl.when(i % tiles_per_seq == 0)
    def _():
        carry_ref[...] = jnp.zeros_like(carry_ref)

    logf = _log_sigmoid(f + bf_ref[...])
    cum = _dot_split(logf, ltri_ref[...], terms=3, w_left=True) + carry_ref[0:1, :]
    carry_ref[0:1, :] = cum[tm - 1:tm, :]
    cum2 = cum * LOG2E
    fq_ref[...] = cum2
    hi, mid, lo = (p.astype(F32) for p in _split_bf16(cum2, 3))
    f_ref[...] = jnp.where(
        lane < FOX_HEADS, hi,
        jnp.where(lane < 2 * FOX_HEADS, mid,
                  jnp.where(lane < 3 * FOX_HEADS, lo,
                            jnp.where(lane < 3 * FOX_HEADS + 3, 1.0, 0.0)))).astype(BF16)


def _even_in(x2, g, w, bf, wtril, bmat, gv, gq, gk, bd, ltri, seq):
    t = x2.shape[0]
    n = t // TM
    full = lambda a: pl.BlockSpec(a.shape, lambda i: (0,) * a.ndim)
    row = lambda wdt: pl.BlockSpec((TM, wdt), lambda i: (i, 0))
    pairs = FOX_HEADS // 2
    colT = pl.BlockSpec((pairs, LANES, TM), lambda i: (0, 0, i))
    return pl.pallas_call(
        functools.partial(_even_in_kernel, seq // TM),
        grid=(n,),
        in_specs=[row(D_MODEL), full(g), full(w), full(bf), full(wtril), full(bmat), full(gv),
                  full(gq), full(gk), full(bd), full(ltri)],
        out_specs=[row(GMLP_WIDTH), colT, row(FOX_WIDTH), colT, row(LANES), row(LANES)],
        out_shape=[jax.ShapeDtypeStruct((t, GMLP_WIDTH), BF16),
                   jax.ShapeDtypeStruct((pairs, LANES, t), BF16),
                   jax.ShapeDtypeStruct((t, FOX_WIDTH), BF16),
                   jax.ShapeDtypeStruct((pairs, LANES, t), BF16),
                   jax.ShapeDtypeStruct((t, LANES), BF16),
                   jax.ShapeDtypeStruct((t, LANES), F32)],
        scratch_shapes=[pltpu.VMEM((8, LANES), F32)],
        compiler_params=_cparams(("arbitrary",)),
        name="even_in",
    )(x2, g, w, bf, wtril, bmat, gv, gq, gk, bd, ltri)


def _fox_query_operand(qt, pr, hh, shift_terms=None):
    tq = qt.shape[1]
    row = lax.broadcasted_iota(jnp.int32, (LANES, tq), 0)
    head = 2 * pr + hh
    in_head = (row >= hh * HEAD_DIM) & (row < (hh + 1) * HEAD_DIM)
    f_rows = (row == head) | (row == head + FOX_HEADS) | (row == head + 2 * FOX_HEADS)
    extra = jnp.where(f_rows, -1.0, 0.0)
    if shift_terms is not None:
        for n, term in enumerate(shift_terms):
            extra = jnp.where(row == 3 * FOX_HEADS + n, term, extra)
    return jnp.concatenate([jnp.where(in_head, qt, 0.0).astype(BF16), extra.astype(BF16)],
                           axis=0)


def _fox_online_kernel(q_ref, k_ref, f_ref, v_ref, o_ref, m_sc, l_sc, acc_sc):
    pr = pl.program_id(1)
    i = pl.program_id(2)
    tq = q_ref.shape[2]
    qt = q_ref[0].astype(F32)
    rhs = [_fox_query_operand(qt, pr, hh) for hh in range(2)]

    m_sc[...] = jnp.full_like(m_sc, NEG)
    l_sc[...] = jnp.zeros_like(l_sc)
    acc_sc[...] = jnp.zeros_like(acc_sc)

    def step(j, masked):
        start = pl.multiple_of(j * tq, tq)
        kaug = jnp.concatenate([k_ref[pl.ds(start, tq), :], f_ref[pl.ds(start, tq), :]],
                               axis=1)
        for hh in range(2):
            s = _dot(kaug, rhs[hh])
            if masked:
                r_id = lax.broadcasted_iota(jnp.int32, (tq, tq), 0)
                c_id = lax.broadcasted_iota(jnp.int32, (tq, tq), 1)
                s = jnp.where(r_id <= c_id, s, NEG)
            m_prev = m_sc[hh]
            m_new = jnp.maximum(m_prev, jnp.max(s, axis=0, keepdims=True))
            alpha = jnp.exp2(m_prev - m_new)
            p = jnp.exp2(s - m_new)
            l_sc[hh] = alpha * l_sc[hh] + jnp.sum(p, axis=0, keepdims=True)
            vt = v_ref[0, hh * HEAD_DIM:(hh + 1) * HEAD_DIM, pl.ds(start, tq)]
            acc_sc[hh] = alpha * acc_sc[hh] + _dot(vt, p.astype(BF16))
            m_sc[hh] = m_new

    def body(j, c):
        step(j, False)
        return c

    lax.fori_loop(0, i, body, 0)
    step(i, True)
    ot = jnp.concatenate([acc_sc[0] / l_sc[0], acc_sc[1] / l_sc[1]], axis=0)
    o_ref[...] = ot.T.astype(o_ref.dtype)


def _fox_fixed_kernel(b_ref, q_ref, k_ref, f_ref, v_ref, fq_ref, o_ref,
                      rhs_sc, z_sc, acc_sc, mask_sc):
    pr = pl.program_id(1)
    tq = z_sc.shape[3]
    tk = z_sc.shape[2]
    nq = rhs_sc.shape[0]
    row = lax.broadcasted_iota(jnp.int32, (LANES, tq), 0)
    for i in range(nq):
        qt = q_ref[0, :, i * tq:(i + 1) * tq].astype(F32)
        fqt = fq_ref[i * tq:(i + 1) * tq, :].T
        for hh in range(2):
            ft = jnp.sum(jnp.where(row == 2 * pr + hh, fqt, 0.0), axis=0, keepdims=True)
            shift = [p.astype(F32) for p in _split_bf16(ft - b_ref[...], 3)]
            rhs_sc[i, hh] = _fox_query_operand(qt, pr, hh, shift)

    mask_sc[...] = jnp.where(lax.broadcasted_iota(jnp.int32, (tk, tq), 0)
                             > lax.broadcasted_iota(jnp.int32, (tk, tq), 1), NEG, 0.0)
    acc_sc[...] = jnp.zeros_like(acc_sc)
    ones = jnp.ones((acc_sc.shape[2] - HEAD_DIM, tk), BF16)

    def scores(i, j, slot):
        start = pl.multiple_of(j * tk, tk)
        kaug = jnp.concatenate([k_ref[pl.ds(start, tk), :], f_ref[pl.ds(start, tk), :]],
                               axis=1)
        for hh in range(2):
            z_sc[slot, hh] = _dot(kaug, rhs_sc[i, hh])

    def consume(i, j, slot, diagonal):
        start = pl.multiple_of(j * tk, tk)
        outs = []
        for hh in range(2):
            z = z_sc[slot, hh]
            if diagonal:
                z = z + mask_sc[...]
            p = jnp.exp2(z).astype(BF16)
            vt = jnp.concatenate(
                [v_ref[0, hh * HEAD_DIM:(hh + 1) * HEAD_DIM, pl.ds(start, tk)], ones], axis=0)
            acc = acc_sc[i, hh] + _dot(vt, p)
            if diagonal:
                outs.append(acc[0:HEAD_DIM] / acc[HEAD_DIM:HEAD_DIM + 1])
            else:
                acc_sc[i, hh] = acc
        if diagonal:
            o_ref[pl.ds(pl.multiple_of(i * tq, tq), tq), :] = (
                jnp.concatenate(outs, axis=0).T.astype(o_ref.dtype))

    def after_offdiag(i, j):
        wrap = j + 1 == i
        i2 = jnp.where(wrap, i + 1, i)
        j2 = jnp.where(wrap, 0, j + 1)
        done = i2 >= nq
        return jnp.where(done, 0, i2), jnp.where(done, 0, j2)

    scores(1, 0, 0)

    def offdiag_body(_, item):
        i0, j0 = item
        i1, j1 = after_offdiag(i0, j0)
        scores(i1, j1, 1)
        consume(i0, j0, 0, False)
        i2, j2 = after_offdiag(i1, j1)
        scores(i2, j2, 0)
        consume(i1, j1, 1, False)
        return i2, j2

    lax.fori_loop(0, nq * (nq - 1) // 4, offdiag_body, (jnp.int32(1), jnp.int32(0)))

    def diag_body(m, c):
        i0 = 2 * m
        scores(i0 + 1, i0 + 1, 1)
        consume(i0, i0, 0, True)
        i2 = jnp.minimum(i0 + 2, nq - 1)
        scores(i2, i2, 0)
        consume(i0 + 1, i0 + 1, 1, True)
        return c

    lax.fori_loop(0, nq // 2, diag_body, 0)


def _fox(qt, k, f3, vt, fq, bound, bsz, seq):
    t = k.shape[0]
    pairs = FOX_HEADS // 2
    out_shape = jax.ShapeDtypeStruct((t, FOX_WIDTH), BF16)
    sem = _cparams(("parallel", "parallel", "arbitrary"))

    def common(tq):
        nq = seq // tq
        in_specs = [pl.BlockSpec((1, LANES, tq), lambda b, p, i: (p, 0, b * nq + i)),
                    pl.BlockSpec((seq, LANES), lambda b, p, i: (b, p)),
                    pl.BlockSpec((seq, LANES), lambda b, p, i: (b, 0)),
                    pl.BlockSpec((1, LANES, seq), lambda b, p, i: (p, 0, b))]
        return nq, in_specs, pl.BlockSpec((tq, LANES), lambda b, p, i: (b * nq + i, p))

    def fixed(brow):
        nq = seq // TQ
        assert nq % 2 == 0 and (nq * (nq - 1)) % 4 == 0
        return pl.pallas_call(
            _fox_fixed_kernel,
            grid=(bsz, pairs),
            in_specs=[pl.BlockSpec((1, TQ), lambda b, p: (0, 0)),
                      pl.BlockSpec((1, LANES, seq), lambda b, p: (p, 0, b)),
                      pl.BlockSpec((seq, LANES), lambda b, p: (b, p)),
                      pl.BlockSpec((seq, LANES), lambda b, p: (b, 0)),
                      pl.BlockSpec((1, LANES, seq), lambda b, p: (p, 0, b)),
                      pl.BlockSpec((seq, LANES), lambda b, p: (b, 0))],
            out_specs=pl.BlockSpec((seq, LANES), lambda b, p: (b, p)),
            out_shape=out_shape,
            scratch_shapes=[pltpu.VMEM((nq, 2, 2 * LANES, TQ), BF16),
                            pltpu.VMEM((2, 2, TQ, TQ), F32),
                            pltpu.VMEM((nq, 2, HEAD_DIM + 16, TQ), F32),
                            pltpu.VMEM((TQ, TQ), F32)],
            compiler_params=_cparams(("parallel", "parallel")), name="fox_attention_fixed",
        )(brow, qt, k, f3, vt, fq)

    def online(brow):
        del brow
        nq, in_specs, out_spec = common(TQ_ONLINE)
        stats = pltpu.VMEM((2, 1, TQ_ONLINE), F32)
        return pl.pallas_call(
            _fox_online_kernel,
            grid=(bsz, pairs, nq),
            in_specs=in_specs, out_specs=out_spec, out_shape=out_shape,
            scratch_shapes=[stats, stats, pltpu.VMEM((2, HEAD_DIM, TQ_ONLINE), F32)],
            compiler_params=sem, name="fox_attention_online",
        )(qt, k, f3, vt)

    brow = jnp.full((1, TQ), bound, F32)
    return lax.cond(bound <= FOX_FAST_BOUND_LOG2, fixed, online, brow)


def _even_out_kernel(a_ref, o_ref, x_ref, w_ref, gm_ref, wr_ref, br_ref,
                     x1_ref, xn_ref, comb_ref):
    x1 = x_ref[...] + _dot(jnp.concatenate([a_ref[...], o_ref[...]], axis=1), w_ref[...])
    x1_ref[...] = x1
    xn = _rms_rows(x1, gm_ref[...]).astype(BF16)
    xn_ref[...] = xn
    comb_ref[...] = _router(xn, wr_ref, br_ref)


def _even_out(a, o, x2, w, gm, wr, br):
    t = x2.shape[0]
    n = t // TM
    full = lambda arr: pl.BlockSpec(arr.shape, lambda i: (0,) * arr.ndim)
    row = lambda wdt: pl.BlockSpec((TM, wdt), lambda i: (i, 0))
    return pl.pallas_call(
        _even_out_kernel,
        grid=(n,),
        in_specs=[row(GMLP_WIDTH), row(FOX_WIDTH), row(D_MODEL), full(w), full(gm),
                  full(wr), full(br)],
        out_specs=[row(D_MODEL), row(D_MODEL), row(LANES)],
        out_shape=[jax.ShapeDtypeStruct((t, D_MODEL), F32),
                   jax.ShapeDtypeStruct((t, D_MODEL), BF16),
                   jax.ShapeDtypeStruct((t, LANES), F32)],
        compiler_params=_cparams(("parallel",)),
        name="even_out",
    )(a, o, x2, w, gm, wr, br)


def _odd_kernel(tiles_per_seq, x_ref, g_ref, w_ref, cw_ref, wp_ref, ps_ref, wo_ref, gm_ref,
                wr_ref, br_ref, x1_ref, xn_ref, comb_ref, zbuf, pbuf):
    i = pl.program_id(0)
    tm = x_ref.shape[0]
    zpad = zbuf.shape[0] - tm
    ppad = pbuf.shape[0] - tm
    x = x_ref[...]
    xn = _rms_rows(x, g_ref[...]).astype(BF16)
    h = _dot(xn, w_ref[...])
    bg = h[:, 0:512]
    cg = h[:, 512:1024]
    hc = h[:, 1024:1536]
    p = h[:, 1536:2048]

    @pl.when(i % tiles_per_seq == 0)
    def _():
        zbuf[0:zpad, :] = jnp.zeros((zpad, CONV_WIDTH), F32)
        pbuf[0:ppad, :] = jnp.zeros((ppad, POOL_WIDTH), F32)

    z = cg * hc
    zbuf[zpad:zpad + tm, :] = z
    cw = cw_ref[...]
    y = (cw[0:1, :] * zbuf[zpad - 2:zpad - 2 + tm, :]
         + cw[1:2, :] * zbuf[zpad - 1:zpad - 1 + tm, :]
         + cw[2:3, :] * z)
    c = bg * y
    zbuf[0:zpad, :] = zbuf[tm:tm + zpad, :]

    pbuf[ppad:ppad + tm, :] = p
    pos = ((i % tiles_per_seq) * tm
           + lax.broadcasted_iota(jnp.int32, (tm, POOL_GROUP), 0)).astype(F32) + 1.0
    ps = ps_ref[...]
    pooled_out = []
    for gi, win in enumerate(POOL_WINDOWS):
        ls = slice(gi * POOL_GROUP, (gi + 1) * POOL_GROUP)
        pg = p[:, ls]
        sm = pg
        for sh in range(1, win):
            sm = sm + pbuf[ppad - sh:ppad - sh + tm, ls]
        cnt = jnp.minimum(pos, float(win))
        pooled = sm / cnt - pg
        pooled_out.append((_dot(pooled.astype(BF16), wp_ref[gi]) * ps[:, ls]).astype(BF16))
    pbuf[0:ppad, :] = pbuf[tm:tm + ppad, :]
    mix = jnp.concatenate([c.astype(BF16)] + pooled_out, axis=1)
    acc = x + _dot(mix, wo_ref[...])

    x1_ref[...] = acc
    xn2 = _rms_rows(acc, gm_ref[...]).astype(BF16)
    xn_ref[...] = xn2
    comb_ref[...] = _router(xn2, wr_ref, br_ref)


def _odd(x2, g, w, cw, wp, ps, wo, gm, wr, br, seq):
    t = x2.shape[0]
    n = t // TM
    full = lambda arr: pl.BlockSpec(arr.shape, lambda i: (0,) * arr.ndim)
    row = lambda wdt: pl.BlockSpec((TM, wdt), lambda i: (i, 0))
    return pl.pallas_call(
        functools.partial(_odd_kernel, seq // TM),
        grid=(n,),
        in_specs=[row(D_MODEL), full(g), full(w), full(cw), full(wp), full(ps), full(wo),
                  full(gm), full(wr), full(br)],
        out_specs=[row(D_MODEL), row(D_MODEL), row(LANES)],
        out_shape=[jax.ShapeDtypeStruct((t, D_MODEL), F32),
                   jax.ShapeDtypeStruct((t, D_MODEL), BF16),
                   jax.ShapeDtypeStruct((t, LANES), F32)],
        scratch_shapes=[pltpu.VMEM((TM + 8, CONV_WIDTH), F32),
                        pltpu.VMEM((TM + 16, POOL_WIDTH), F32)],
        compiler_params=_cparams(("arbitrary",)),
        name="odd_mixer",
    )(x2, g, w, cw, wp, ps, wo, gm, wr, br)


def _moe_kernel(xn_ref, wg_ref, wu_ref, wd_ref, comb_ref, x1_ref, o_ref):
    g = pl.program_id(1)

    @pl.when(g == 0)
    def _():
        o_ref[...] = x1_ref[...]

    tm = xn_ref.shape[0]
    xn = xn_ref[...]
    lane = lax.broadcasted_iota(jnp.int32, (tm, LANES), 1)
    comb = comb_ref[...]
    hid = []
    for e in range(EXPERTS_PER_GROUP):
        gate = _dot(xn, wg_ref[e])
        up = _dot(xn, wu_ref[e])
        cw = jnp.sum(jnp.where(lane == g * EXPERTS_PER_GROUP + e, comb, 0.0), axis=1,
                     keepdims=True)
        hid.append(((gate * jax.nn.sigmoid(gate)) * up * cw).astype(BF16))
    o_ref[...] += _dot(jnp.concatenate(hid, axis=1), wd_ref[0])


def _moe(xn, wg, wu, wd, comb, x1, layer):
    t = xn.shape[0]
    n = t // TM_MOE
    gh = EXPERTS_PER_GROUP * EXPERT_HIDDEN
    wblk = pl.BlockSpec((None, EXPERTS_PER_GROUP, D_MODEL, EXPERT_HIDDEN),
                        lambda i, g: (layer, g, 0, 0))
    return pl.pallas_call(
        _moe_kernel,
        grid=(n, N_GROUPS),
        in_specs=[pl.BlockSpec((TM_MOE, D_MODEL), lambda i, g: (i, 0)),
                  wblk, wblk,
                  pl.BlockSpec((None, 1, gh, D_MODEL), lambda i, g: (layer, g, 0, 0)),
                  pl.BlockSpec((TM_MOE, LANES), lambda i, g: (i, 0)),
                  pl.BlockSpec((TM_MOE, D_MODEL), lambda i, g: (i, 0))],
        out_specs=pl.BlockSpec((TM_MOE, D_MODEL), lambda i, g: (i, 0)),
        out_shape=jax.ShapeDtypeStruct((t, D_MODEL), F32),
        compiler_params=_cparams(("parallel", "arbitrary")),
        name="moe",
    )(xn, wg, wu, wd, comb, x1)


def _router_params(w_group, b_group, w_router, b_router):
    wr = jnp.concatenate([w_router.reshape(D_MODEL, N_EXPERTS), w_group], axis=1)
    wr = jnp.pad(wr, ((0, 0), (0, LANES - wr.shape[1]))).astype(BF16)
    br = jnp.concatenate([b_router.reshape(N_EXPERTS), b_group])
    br = jnp.pad(br, (0, LANES - br.shape[0])).reshape(1, LANES).astype(F32)
    return wr, br


def _moe_params(w_gate, w_up, w_down):
    wd = w_down.astype(BF16).reshape(w_down.shape[0], N_GROUPS,
                                     EXPERTS_PER_GROUP * EXPERT_HIDDEN, D_MODEL)
    return w_gate.astype(BF16), w_up.astype(BF16), wd


def kernel(x, ev_norm, ev_w_in, ev_b_forget, ev_w_s, ev_b_s, ev_g_v, ev_g_q, ev_g_k, ev_w_out,
           od_norm, od_w_in, od_conv_w, od_w_pool, od_pool_scale, od_w_out,
           moe_norm, moe_w_group, moe_b_group, moe_w_router, moe_b_router,
           moe_w_gate, moe_w_up, moe_w_down):
    bsz, seq, d = x.shape
    t = bsz * seq
    x2 = x.reshape(t, d)

    n_main = 2 * GMLP_WIDTH + 3 * FOX_WIDTH
    w_f = ev_w_in[0][:, n_main:]
    w_in = jnp.concatenate([ev_w_in[0][:, :n_main], w_f, w_f, w_f,
                            jnp.zeros((d, LANES - 3 * FOX_HEADS), F32)], axis=1).astype(BF16)
    bf = jnp.pad(jnp.tile(ev_b_forget[0], 3), (0, LANES - 3 * FOX_HEADS)).reshape(1, LANES)
    tril = jnp.tril(jnp.ones((CHUNK, CHUNK), F32))
    wtril = (ev_w_s[0] * tril).astype(BF16)
    bmat = jnp.repeat(ev_b_s[0].T, GMLP_HEAD, axis=1)
    gv = ev_g_v[0].reshape(1, GMLP_WIDTH)
    gq = jnp.tile(ev_g_q[0], FOX_HEADS).reshape(1, FOX_WIDTH)
    gk = jnp.tile(ev_g_k[0], FOX_HEADS).reshape(1, FOX_WIDTH)
    blk = jnp.arange(256) // HEAD_DIM
    bd = jnp.where(blk[:, None] == blk[None, :], 1.0 / HEAD_DIM, 0.0).astype(BF16)
    ltri = jnp.tril(jnp.ones((TM, TM), F32)).astype(BF16)

    a, qt, kn, vt, f3, fq = _even_in(x2, ev_norm[0].reshape(1, d), w_in, bf, wtril, bmat, gv, gq,
                                     gk, bd, ltri, seq)
    bound = (HEAD_DIM ** 0.5 * LOG2E * 1.01) * jnp.max(jnp.abs(ev_g_q[0])) * jnp.max(
        jnp.abs(ev_g_k[0]))
    o = _fox(qt, kn, f3, vt, fq, bound, bsz, seq)

    wr0, br0 = _router_params(moe_w_group[0], moe_b_group[0], moe_w_router[0], moe_b_router[0])
    w_out0 = ev_w_out[0].astype(BF16)
    x1, xn1, comb1 = _even_out(a, o, x2, w_out0,
                               moe_norm[0].reshape(1, d), wr0, br0)
    moe_w = _moe_params(moe_w_gate, moe_w_up, moe_w_down)
    xa = _moe(xn1, *moe_w, comb1, x1, 0)

    wr1, br1 = _router_params(moe_w_group[1], moe_b_group[1], moe_w_router[1], moe_b_router[1])
    x3, xn3, comb3 = _odd(xa, od_norm[0].reshape(1, d), od_w_in[0].astype(BF16), od_conv_w[0],
                          od_w_pool[0].astype(BF16), od_pool_scale[0].reshape(1, POOL_WIDTH),
                          od_w_out[0].astype(BF16), moe_norm[1].reshape(1, d), wr1, br1, seq)
    xb = _moe(xn3, *moe_w, comb3, x3, 1)
    return xb.reshape(bsz, seq, d)
```

```python
import functools

import jax
import jax.numpy as jnp
from jax import lax
from jax.experimental import pallas as pl
from jax.experimental.pallas import tpu as pltpu

F32 = jnp.float32
BF16 = jnp.bfloat16

D_MODEL = 1024
EPS = 1e-6
CHUNK = 128
GMLP_GROUPS = 4
GMLP_HEAD = 64
GMLP_WIDTH = 256
FOX_HEADS = 12
HEAD_DIM = 64
FOX_WIDTH = 768
CONV_WIDTH = 512
CONV_K = 3
POOL_WINDOWS = (2, 4, 8, 16)
POOL_GROUP = 128
POOL_WIDTH = 512
N_GROUPS = 4
EXPERTS_PER_GROUP = 4
N_EXPERTS = 16
EXPERT_HIDDEN = 256

LANES = 128
V7X_VMEM_LIMIT_BYTES = 56 * 1024 * 1024

TM = 512
TQ = 512
TQ_ONLINE = 256
TM_MOE = 1024
SUBTILES = 2
EVEN_MAIN = 2 * GMLP_WIDTH + 3 * FOX_WIDTH
NEG = -1e30
LOG2E = 1.4426950408889634
FOX_FAST_BOUND_LOG2 = 50.0


def _cparams(sem):
    return pltpu.CompilerParams(dimension_semantics=sem,
                                vmem_limit_bytes=V7X_VMEM_LIMIT_BYTES)


def _dot(a, b):
    return jnp.dot(a, b, preferred_element_type=F32)


def _split_bf16(x, terms):
    parts = []
    r = x
    for _ in range(terms):
        p = r.astype(BF16)
        parts.append(p)
        r = r - p.astype(F32)
    return parts


def _dot_split(x, w_bf16, terms=2, w_left=False):
    acc = None
    for p in _split_bf16(x, terms):
        d = _dot(w_bf16, p) if w_left else _dot(p, w_bf16)
        acc = d if acc is None else acc + d
    return acc


def _rms_rows(x, g):
    ms = jnp.mean(x * x, axis=-1, keepdims=True)
    return x * lax.rsqrt(ms + EPS) * g


def _head_rms(x, bd, g):
    outs = []
    for c in range(x.shape[1] // 256):
        xc = x[:, c * 256:(c + 1) * 256]
        ms = _dot_split(xc * xc, bd)
        outs.append(xc * lax.rsqrt(ms + EPS))
    y = outs[0] if len(outs) == 1 else jnp.concatenate(outs, axis=1)
    return y * g


def _log_sigmoid(x):
    return -(jnp.maximum(-x, 0.0) + jnp.log(1.0 + jnp.exp(-jnp.abs(x))))


def _router(xn_bf16, wr_ref, br_ref):
    r = _dot(xn_bf16, wr_ref[...]) + br_ref[...]
    tm = r.shape[0]
    rt = r.T
    er = rt[0:N_EXPERTS]
    gr = rt[N_EXPERTS:N_EXPERTS + 8]
    grow = lax.broadcasted_iota(jnp.int32, (8, tm), 0).astype(F32)
    erow = lax.broadcasted_iota(jnp.int32, (N_EXPERTS, tm), 0).astype(F32)
    is_g = grow < float(N_GROUPS)
    gl = jnp.where(is_g, gr, NEG)
    gmax = jnp.max(gl, axis=0, keepdims=True)
    gidx = jnp.min(jnp.where(gl == gmax, grow, 999.0), axis=0, keepdims=True)
    gsum = jnp.sum(jnp.where(is_g, jnp.exp(gl - gmax), 0.0), axis=0, keepdims=True)
    gp = 1.0 / gsum
    lo = gidx * float(EXPERTS_PER_GROUP)
    sel = (erow >= lo) & (erow < lo + float(EXPERTS_PER_GROUP))
    el = jnp.where(sel, er, NEG)
    emax = jnp.max(el, axis=0, keepdims=True)
    ee = jnp.where(sel, jnp.exp(el - emax), 0.0)
    ep = ee / jnp.sum(ee, axis=0, keepdims=True)
    ep = jnp.where(sel, ep, -1.0)
    p1 = jnp.max(ep, axis=0, keepdims=True)
    i1 = jnp.min(jnp.where(ep == p1, erow, 999.0), axis=0, keepdims=True)
    ep2 = jnp.where(erow == i1, -1.0, ep)
    p2 = jnp.max(ep2, axis=0, keepdims=True)
    i2 = jnp.min(jnp.where(ep2 == p2, erow, 999.0), axis=0, keepdims=True)
    den = p1 + p2
    comb_t = (jnp.where(erow == i1, gp * (p1 / den), 0.0)
              + jnp.where(erow == i2, gp * (p2 / den), 0.0))
    comb_t = jnp.concatenate([comb_t, jnp.zeros((LANES - N_EXPERTS, tm), F32)], axis=0)
    return comb_t.T


def _even_in_kernel(tiles_per_seq, x_ref, g_ref, w_ref, wf_ref, bf_ref, wtril_ref, bmat_ref,
                    gv_ref, gq_ref, gk_ref, bd_ref, ltri_ref,
                    a_ref, q_ref, k_ref, v_ref, f_ref, fq_ref, wbf_ref, carry_ref):
    i = pl.program_id(0)
    tm = x_ref.shape[0]

    @pl.when(i == 0)
    def _():
        wbf_ref[...] = w_ref[...].astype(BF16)

    xn = _rms_rows(x_ref[...], g_ref[...]).astype(BF16)
    h = _dot(xn, wbf_ref[...])
    u = h[:, 0:256]
    v = h[:, 256:512]
    q = h[:, 512:1280]
    k = h[:, 1280:2048]
    val = h[:, 2048:2816]
    f = _dot(xn, wf_ref[...])

    gu = jax.nn.gelu(u)
    vn = _head_rms(jax.nn.gelu(v), bd_ref[...], gv_ref[...])
    lane = lax.broadcasted_iota(jnp.int32, (tm, LANES), 1)
    lo_half = lane < GMLP_HEAD
    bmat = bmat_ref[...]
    pair_out = []
    for pr in range(2):
        vp = vn[:, pr * 128:(pr + 1) * 128]
        v_lo = jnp.where(lo_half, vp, 0.0).astype(BF16)
        v_hi = jnp.where(lo_half, 0.0, vp).astype(BF16)
        chunks = []
        for c in range(tm // CHUNK):
            rs = slice(c * CHUNK, (c + 1) * CHUNK)
            s = (_dot(wtril_ref[2 * pr], v_lo[rs]) + _dot(wtril_ref[2 * pr + 1], v_hi[rs])
                 + bmat[:, pr * 128:(pr + 1) * 128])
            chunks.append(s)
        pair_out.append(jnp.concatenate(chunks, axis=0))
    s_all = jnp.concatenate(pair_out, axis=1)
    a_ref[...] = (gu * s_all).astype(BF16)

    bd = bd_ref[...]
    qn = _head_rms(q, bd, gq_ref[...]) * (HEAD_DIM ** -0.5 * LOG2E)
    for pr in range(FOX_HEADS // 2):
        ls = slice(pr * LANES, (pr + 1) * LANES)
        q_ref[pr] = qn[:, ls].T.astype(BF16)
        v_ref[pr] = val[:, ls].T.astype(BF16)
    k_ref[...] = _head_rms(k, bd, gk_ref[...]).astype(BF16)

    @pl.when(i % tiles_per_seq == 0)
    def _():
        carry_ref[...] = jnp.zeros_like(carry_ref)

    logf = _log_sigmoid(f + bf_ref[...])
    cum = _dot_split(logf, ltri_ref[...], terms=3, w_left=True) + carry_ref[0:1, :]
    carry_ref[0:1, :] = cum[tm - 1:tm, :]
    cum2 = cum * LOG2E
    fq_ref[...] = cum2
    hi, mid, lo = (p.astype(F32) for p in _split_bf16(cum2, 3))
    f_ref[...] = jnp.where(
        lane < FOX_HEADS, hi,
        jnp.where(lane < 2 * FOX_HEADS, mid,
                  jnp.where(lane < 3 * FOX_HEADS, lo,
                            jnp.where(lane < 3 * FOX_HEADS + 3, 1.0, 0.0)))).astype(BF16)


def _even_in(x2, g, w_all, wf, bf, wtril, bmat, gv, gq, gk, bd, ltri, seq):
    t = x2.shape[0]
    n = t // TM
    full = lambda a: pl.BlockSpec(a.shape, lambda i: (0,) * a.ndim)
    row = lambda wdt: pl.BlockSpec((TM, wdt), lambda i: (i, 0))
    pairs = FOX_HEADS // 2
    colT = pl.BlockSpec((pairs, LANES, TM), lambda i: (0, 0, i))
    w_main = pl.BlockSpec((None, D_MODEL, EVEN_MAIN), lambda i: (0, 0, 0),
                          pipeline_mode=pl.Buffered(1))
    return pl.pallas_call(
        functools.partial(_even_in_kernel, seq // TM),
        grid=(n,),
        in_specs=[row(D_MODEL), full(g), w_main, full(wf), full(bf), full(wtril), full(bmat),
                  full(gv), full(gq), full(gk), full(bd), full(ltri)],
        out_specs=[row(GMLP_WIDTH), colT, row(FOX_WIDTH), colT, row(LANES), row(LANES)],
        out_shape=[jax.ShapeDtypeStruct((t, GMLP_WIDTH), BF16),
                   jax.ShapeDtypeStruct((pairs, LANES, t), BF16),
                   jax.ShapeDtypeStruct((t, FOX_WIDTH), BF16),
                   jax.ShapeDtypeStruct((pairs, LANES, t), BF16),
                   jax.ShapeDtypeStruct((t, LANES), BF16),
                   jax.ShapeDtypeStruct((t, LANES), F32)],
        scratch_shapes=[pltpu.VMEM((D_MODEL, EVEN_MAIN), BF16), pltpu.VMEM((8, LANES), F32)],
        compiler_params=_cparams(("arbitrary",)),
        name="even_in",
    )(x2, g, w_all, wf, bf, wtril, bmat, gv, gq, gk, bd, ltri)


def _fox_query_operand(qt, pr, hh, shift_terms=None):
    tq = qt.shape[1]
    row = lax.broadcasted_iota(jnp.int32, (LANES, tq), 0)
    head = 2 * pr + hh
    in_head = (row >= hh * HEAD_DIM) & (row < (hh + 1) * HEAD_DIM)
    f_rows = (row == head) | (row == head + FOX_HEADS) | (row == head + 2 * FOX_HEADS)
    extra = jnp.where(f_rows, -1.0, 0.0)
    if shift_terms is not None:
        for n, term in enumerate(shift_terms):
            extra = jnp.where(row == 3 * FOX_HEADS + n, term, extra)
    return jnp.concatenate([jnp.where(in_head, qt, 0.0).astype(BF16), extra.astype(BF16)],
                           axis=0)


def _fox_online_kernel(q_ref, k_ref, f_ref, v_ref, o_ref, m_sc, l_sc, acc_sc):
    pr = pl.program_id(1)
    i = pl.program_id(2)
    tq = q_ref.shape[2]
    qt = q_ref[0].astype(F32)
    rhs = [_fox_query_operand(qt, pr, hh) for hh in range(2)]

    m_sc[...] = jnp.full_like(m_sc, NEG)
    l_sc[...] = jnp.zeros_like(l_sc)
    acc_sc[...] = jnp.zeros_like(acc_sc)

    def step(j, masked):
        start = pl.multiple_of(j * tq, tq)
        kaug = jnp.concatenate([k_ref[pl.ds(start, tq), :], f_ref[pl.ds(start, tq), :]],
                               axis=1)
        for hh in range(2):
            s = _dot(kaug, rhs[hh])
            if masked:
                r_id = lax.broadcasted_iota(jnp.int32, (tq, tq), 0)
                c_id = lax.broadcasted_iota(jnp.int32, (tq, tq), 1)
                s = jnp.where(r_id <= c_id, s, NEG)
            m_prev = m_sc[hh]
            m_new = jnp.maximum(m_prev, jnp.max(s, axis=0, keepdims=True))
            alpha = jnp.exp2(m_prev - m_new)
            p = jnp.exp2(s - m_new)
            l_sc[hh] = alpha * l_sc[hh] + jnp.sum(p, axis=0, keepdims=True)
            vt = v_ref[0, hh * HEAD_DIM:(hh + 1) * HEAD_DIM, pl.ds(start, tq)]
            acc_sc[hh] = alpha * acc_sc[hh] + _dot(vt, p.astype(BF16))
            m_sc[hh] = m_new

    def body(j, c):
        step(j, False)
        return c

    lax.fori_loop(0, i, body, 0)
    step(i, True)
    ot = jnp.concatenate([acc_sc[0] / l_sc[0], acc_sc[1] / l_sc[1]], axis=0)
    o_ref[...] = ot.T.astype(o_ref.dtype)


def _fox_fixed_kernel(b_ref, q_ref, k_ref, f_ref, v_ref, fq_ref, o_ref,
                      rhs_sc, z_sc, acc_sc, mask_sc):
    pr = pl.program_id(1)
    tq = z_sc.shape[3]
    tk = z_sc.shape[2]
    nq = rhs_sc.shape[0]
    row = lax.broadcasted_iota(jnp.int32, (LANES, tq), 0)
    for i in range(nq):
        qt = q_ref[0, :, i * tq:(i + 1) * tq].astype(F32)
        fqt = fq_ref[i * tq:(i + 1) * tq, :].T
        for hh in range(2):
            ft = jnp.sum(jnp.where(row == 2 * pr + hh, fqt, 0.0), axis=0, keepdims=True)
            shift = [p.astype(F32) for p in _split_bf16(ft - b_ref[...], 3)]
            rhs_sc[i, hh] = _fox_query_operand(qt, pr, hh, shift)

    mask_sc[...] = jnp.where(lax.broadcasted_iota(jnp.int32, (tk, tq), 0)
                             > lax.broadcasted_iota(jnp.int32, (tk, tq), 1), NEG, 0.0)
    acc_sc[...] = jnp.zeros_like(acc_sc)
    ones = jnp.ones((acc_sc.shape[2] - HEAD_DIM, tk), BF16)

    def scores(i, j, slot):
        start = pl.multiple_of(j * tk, tk)
        kaug = jnp.concatenate([k_ref[pl.ds(start, tk), :], f_ref[pl.ds(start, tk), :]],
                               axis=1)
        for hh in range(2):
            z_sc[slot, hh] = _dot(kaug, rhs_sc[i, hh])

    def consume(i, j, slot, diagonal):
        start = pl.multiple_of(j * tk, tk)
        outs = []
        for hh in range(2):
            z = z_sc[slot, hh]
            if diagonal:
                z = z + mask_sc[...]
            p = jnp.exp2(z).astype(BF16)
            vt = jnp.concatenate(
                [v_ref[0, hh * HEAD_DIM:(hh + 1) * HEAD_DIM, pl.ds(start, tk)], ones], axis=0)
            acc = acc_sc[i, hh] + _dot(vt, p)
            if diagonal:
                outs.append(acc[0:HEAD_DIM] / acc[HEAD_DIM:HEAD_DIM + 1])
            else:
                acc_sc[i, hh] = acc
        if diagonal:
            o_ref[pl.ds(pl.multiple_of(i * tq, tq), tq), :] = (
                jnp.concatenate(outs, axis=0).T.astype(o_ref.dtype))

    def after_offdiag(i, j):
        wrap = j + 1 == i
        i2 = jnp.where(wrap, i + 1, i)
        j2 = jnp.where(wrap, 0, j + 1)
        done = i2 >= nq
        return jnp.where(done, 0, i2), jnp.where(done, 0, j2)

    scores(1, 0, 0)

    def offdiag_body(_, item):
        i0, j0 = item
        i1, j1 = after_offdiag(i0, j0)
        scores(i1, j1, 1)
        consume(i0, j0, 0, False)
        i2, j2 = after_offdiag(i1, j1)
        scores(i2, j2, 0)
        consume(i1, j1, 1, False)
        return i2, j2

    lax.fori_loop(0, nq * (nq - 1) // 4, offdiag_body, (jnp.int32(1), jnp.int32(0)))

    def diag_body(m, c):
        i0 = 2 * m
        scores(i0 + 1, i0 + 1, 1)
        consume(i0, i0, 0, True)
        i2 = jnp.minimum(i0 + 2, nq - 1)
        scores(i2, i2, 0)
        consume(i0 + 1, i0 + 1, 1, True)
        return c

    lax.fori_loop(0, nq // 2, diag_body, 0)


def _fox(qt, k, f3, vt, fq, bound, bsz, seq):
    t = k.shape[0]
    pairs = FOX_HEADS // 2
    out_shape = jax.ShapeDtypeStruct((t, FOX_WIDTH), BF16)
    sem = _cparams(("parallel", "parallel", "arbitrary"))

    def common(tq):
        nq = seq // tq
        in_specs = [pl.BlockSpec((1, LANES, tq), lambda b, p, i: (p, 0, b * nq + i)),
                    pl.BlockSpec((seq, LANES), lambda b, p, i: (b, p)),
                    pl.BlockSpec((seq, LANES), lambda b, p, i: (b, 0)),
                    pl.BlockSpec((1, LANES, seq), lambda b, p, i: (p, 0, b))]
        return nq, in_specs, pl.BlockSpec((tq, LANES), lambda b, p, i: (b * nq + i, p))

    def fixed(brow):
        nq = seq // TQ
        assert nq % 2 == 0 and (nq * (nq - 1)) % 4 == 0
        return pl.pallas_call(
            _fox_fixed_kernel,
            grid=(bsz, pairs),
            in_specs=[pl.BlockSpec((1, TQ), lambda b, p: (0, 0)),
                      pl.BlockSpec((1, LANES, seq), lambda b, p: (p, 0, b)),
                      pl.BlockSpec((seq, LANES), lambda b, p: (b, p)),
                      pl.BlockSpec((seq, LANES), lambda b, p: (b, 0)),
                      pl.BlockSpec((1, LANES, seq), lambda b, p: (p, 0, b)),
                      pl.BlockSpec((seq, LANES), lambda b, p: (b, 0))],
            out_specs=pl.BlockSpec((seq, LANES), lambda b, p: (b, p)),
            out_shape=out_shape,
            scratch_shapes=[pltpu.VMEM((nq, 2, 2 * LANES, TQ), BF16),
                            pltpu.VMEM((2, 2, TQ, TQ), F32),
                            pltpu.VMEM((nq, 2, HEAD_DIM + 16, TQ), F32),
                            pltpu.VMEM((TQ, TQ), F32)],
            compiler_params=_cparams(("parallel", "parallel")), name="fox_attention_fixed",
        )(brow, qt, k, f3, vt, fq)

    def online(brow):
        del brow
        nq, in_specs, out_spec = common(TQ_ONLINE)
        stats = pltpu.VMEM((2, 1, TQ_ONLINE), F32)
        return pl.pallas_call(
            _fox_online_kernel,
            grid=(bsz, pairs, nq),
            in_specs=in_specs, out_specs=out_spec, out_shape=out_shape,
            scratch_shapes=[stats, stats, pltpu.VMEM((2, HEAD_DIM, TQ_ONLINE), F32)],
            compiler_params=sem, name="fox_attention_online",
        )(qt, k, f3, vt)

    brow = jnp.full((1, TQ), bound, F32)
    return lax.cond(bound <= FOX_FAST_BOUND_LOG2, fixed, online, brow)


def _even_out_kernel(a_ref, o_ref, x_ref, w_ref, gm_ref, wr_ref, br_ref,
                     x1_ref, xn_ref, comb_ref):
    sub = x_ref.shape[0] // SUBTILES
    for s in range(SUBTILES):
        rs = slice(s * sub, (s + 1) * sub)
        x1 = x_ref[rs, :] + _dot(jnp.concatenate([a_ref[rs, :], o_ref[rs, :]], axis=1),
                                 w_ref[...])
        x1_ref[rs, :] = x1
        xn = _rms_rows(x1, gm_ref[...]).astype(BF16)
        xn_ref[rs, :] = xn
        comb_ref[rs, :] = _router(xn, wr_ref, br_ref)


def _even_out(a, o, x2, w, gm, wr, br):
    t = x2.shape[0]
    n = t // TM
    full = lambda arr: pl.BlockSpec(arr.shape, lambda i: (0,) * arr.ndim)
    row = lambda wdt: pl.BlockSpec((TM, wdt), lambda i: (i, 0))
    return pl.pallas_call(
        _even_out_kernel,
        grid=(n,),
        in_specs=[row(GMLP_WIDTH), row(FOX_WIDTH), row(D_MODEL), full(w), full(gm),
                  full(wr), full(br)],
        out_specs=[row(D_MODEL), row(D_MODEL), row(LANES)],
        out_shape=[jax.ShapeDtypeStruct((t, D_MODEL), F32),
                   jax.ShapeDtypeStruct((t, D_MODEL), BF16),
                   jax.ShapeDtypeStruct((t, LANES), F32)],
        compiler_params=_cparams(("parallel",)),
        name="even_out",
    )(a, o, x2, w, gm, wr, br)


def _odd_kernel(tiles_per_seq, x_ref, g_ref, w_ref, cw_ref, wp_ref, ps_ref, wo_ref, gm_ref,
                wr_ref, br_ref, x1_ref, xn_ref, comb_ref, zbuf, pbuf):
    i = pl.program_id(0)
    tm = x_ref.shape[0]
    zpad = zbuf.shape[0] - tm
    ppad = pbuf.shape[0] - tm
    @pl.when(i % tiles_per_seq == 0)
    def _():
        zbuf[0:zpad, :] = jnp.zeros((zpad, CONV_WIDTH), F32)
        pbuf[0:ppad, :] = jnp.zeros((ppad, POOL_WIDTH), F32)

    cw = cw_ref[...]
    ps = ps_ref[...]
    sub = tm // SUBTILES
    for s in range(SUBTILES):
        r0 = s * sub
        rs = slice(r0, r0 + sub)
        x = x_ref[rs, :]
        xn = _rms_rows(x, g_ref[...]).astype(BF16)
        h = _dot(xn, w_ref[...])
        bg = h[:, 0:512]
        cg = h[:, 512:1024]
        hc = h[:, 1024:1536]
        p = h[:, 1536:2048]

        z = cg * hc
        zbuf[zpad + r0:zpad + r0 + sub, :] = z
        y = (cw[0:1, :] * zbuf[zpad + r0 - 2:zpad + r0 - 2 + sub, :]
             + cw[1:2, :] * zbuf[zpad + r0 - 1:zpad + r0 - 1 + sub, :]
             + cw[2:3, :] * z)
        c = bg * y

        pbuf[ppad + r0:ppad + r0 + sub, :] = p
        pos = ((i % tiles_per_seq) * tm + r0
               + lax.broadcasted_iota(jnp.int32, (sub, POOL_GROUP), 0)).astype(F32) + 1.0
        pooled_out = []
        for gi, win in enumerate(POOL_WINDOWS):
            ls = slice(gi * POOL_GROUP, (gi + 1) * POOL_GROUP)
            pg = p[:, ls]
            sm = pg
            for sh in range(1, win):
                sm = sm + pbuf[ppad + r0 - sh:ppad + r0 - sh + sub, ls]
            cnt = jnp.minimum(pos, float(win))
            pooled = sm / cnt - pg
            pooled_out.append((_dot(pooled.astype(BF16), wp_ref[gi]) * ps[:, ls]).astype(BF16))
        mix = jnp.concatenate([c.astype(BF16)] + pooled_out, axis=1)
        acc = x + _dot(mix, wo_ref[...])

        x1_ref[rs, :] = acc
        xn2 = _rms_rows(acc, gm_ref[...]).astype(BF16)
        xn_ref[rs, :] = xn2
        comb_ref[rs, :] = _router(xn2, wr_ref, br_ref)

    zbuf[0:zpad, :] = zbuf[tm:tm + zpad, :]
    pbuf[0:ppad, :] = pbuf[tm:tm + ppad, :]


def _odd(x2, g, w, cw, wp, ps, wo, gm, wr, br, seq):
    t = x2.shape[0]
    n = t // TM
    full = lambda arr: pl.BlockSpec(arr.shape, lambda i: (0,) * arr.ndim)
    row = lambda wdt: pl.BlockSpec((TM, wdt), lambda i: (i, 0))
    return pl.pallas_call(
        functools.partial(_odd_kernel, seq // TM),
        grid=(n,),
        in_specs=[row(D_MODEL), full(g), full(w), full(cw), full(wp), full(ps), full(wo),
                  full(gm), full(wr), full(br)],
        out_specs=[row(D_MODEL), row(D_MODEL), row(LANES)],
        out_shape=[jax.ShapeDtypeStruct((t, D_MODEL), F32),
                   jax.ShapeDtypeStruct((t, D_MODEL), BF16),
                   jax.ShapeDtypeStruct((t, LANES), F32)],
        scratch_shapes=[pltpu.VMEM((TM + 8, CONV_WIDTH), F32),
                        pltpu.VMEM((TM + 16, POOL_WIDTH), F32)],
        compiler_params=_cparams(("arbitrary",)),
        name="odd_mixer",
    )(x2, g, w, cw, wp, ps, wo, gm, wr, br)


def _moe_kernel(xn_ref, wg_ref, wu_ref, wd_ref, comb_ref, x1_ref, o_ref):
    g = pl.program_id(1)

    @pl.when(g == 0)
    def _():
        o_ref[...] = x1_ref[...]

    tm = xn_ref.shape[0]
    xn = xn_ref[...]
    lane = lax.broadcasted_iota(jnp.int32, (tm, LANES), 1)
    comb = comb_ref[...]
    hid = []
    for e in range(EXPERTS_PER_GROUP):
        gate = _dot(xn, wg_ref[e])
        up = _dot(xn, wu_ref[e])
        cw = jnp.sum(jnp.where(lane == g * EXPERTS_PER_GROUP + e, comb, 0.0), axis=1,
                     keepdims=True)
        hid.append(((gate * jax.nn.sigmoid(gate)) * up * cw).astype(BF16))
    o_ref[...] += _dot(jnp.concatenate(hid, axis=1), wd_ref[0])


def _moe(xn, wg, wu, wd, comb, x1, layer):
    t = xn.shape[0]
    n = t // TM_MOE
    gh = EXPERTS_PER_GROUP * EXPERT_HIDDEN
    wblk = pl.BlockSpec((None, EXPERTS_PER_GROUP, D_MODEL, EXPERT_HIDDEN),
                        lambda i, g: (layer, g, 0, 0))
    return pl.pallas_call(
        _moe_kernel,
        grid=(n, N_GROUPS),
        in_specs=[pl.BlockSpec((TM_MOE, D_MODEL), lambda i, g: (i, 0)),
                  wblk, wblk,
                  pl.BlockSpec((None, 1, gh, D_MODEL), lambda i, g: (layer, g, 0, 0)),
                  pl.BlockSpec((TM_MOE, LANES), lambda i, g: (i, 0)),
                  pl.BlockSpec((TM_MOE, D_MODEL), lambda i, g: (i, 0))],
        out_specs=pl.BlockSpec((TM_MOE, D_MODEL), lambda i, g: (i, 0)),
        out_shape=jax.ShapeDtypeStruct((t, D_MODEL), F32),
        compiler_params=_cparams(("parallel", "arbitrary")),
        name="moe",
    )(xn, wg, wu, wd, comb, x1)


def _router_params(w_group, b_group, w_router, b_router):
    wr = jnp.concatenate([w_router.reshape(D_MODEL, N_EXPERTS), w_group], axis=1)
    wr = jnp.pad(wr, ((0, 0), (0, LANES - wr.shape[1]))).astype(BF16)
    br = jnp.concatenate([b_router.reshape(N_EXPERTS), b_group])
    br = jnp.pad(br, (0, LANES - br.shape[0])).reshape(1, LANES).astype(F32)
    return wr, br


def _moe_params(w_gate, w_up, w_down):
    wd = w_down.astype(BF16).reshape(w_down.shape[0], N_GROUPS,
                                     EXPERTS_PER_GROUP * EXPERT_HIDDEN, D_MODEL)
    return w_gate.astype(BF16), w_up.astype(BF16), wd


def kernel(x, ev_norm, ev_w_in, ev_b_forget, ev_w_s, ev_b_s, ev_g_v, ev_g_q, ev_g_k, ev_w_out,
           od_norm, od_w_in, od_conv_w, od_w_pool, od_pool_scale, od_w_out,
           moe_norm, moe_w_group, moe_b_group, moe_w_router, moe_b_router,
           moe_w_gate, moe_w_up, moe_w_down):
    bsz, seq, d = x.shape
    t = bsz * seq
    x2 = x.reshape(t, d)

    w_f = ev_w_in[0][:, EVEN_MAIN:]
    w_f3 = jnp.pad(jnp.tile(w_f, (1, 3)), ((0, 0), (0, LANES - 3 * FOX_HEADS))).astype(BF16)
    bf = jnp.pad(jnp.tile(ev_b_forget[0], 3), (0, LANES - 3 * FOX_HEADS)).reshape(1, LANES)
    tril = jnp.tril(jnp.ones((CHUNK, CHUNK), F32))
    wtril = (ev_w_s[0] * tril).astype(BF16)
    bmat = jnp.repeat(ev_b_s[0].T, GMLP_HEAD, axis=1)
    gv = ev_g_v[0].reshape(1, GMLP_WIDTH)
    gq = jnp.tile(ev_g_q[0], FOX_HEADS).reshape(1, FOX_WIDTH)
    gk = jnp.tile(ev_g_k[0], FOX_HEADS).reshape(1, FOX_WIDTH)
    blk = jnp.arange(256) // HEAD_DIM
    bd = jnp.where(blk[:, None] == blk[None, :], 1.0 / HEAD_DIM, 0.0).astype(BF16)
    ltri = jnp.tril(jnp.ones((TM, TM), F32)).astype(BF16)

    a, qt, kn, vt, f3, fq = _even_in(x2, ev_norm[0].reshape(1, d), ev_w_in, w_f3, bf, wtril, bmat,
                                     gv, gq, gk, bd, ltri, seq)
    bound = (HEAD_DIM ** 0.5 * LOG2E * 1.01) * jnp.max(jnp.abs(ev_g_q[0])) * jnp.max(
        jnp.abs(ev_g_k[0]))
    o = _fox(qt, kn, f3, vt, fq, bound, bsz, seq)

    wr0, br0 = _router_params(moe_w_group[0], moe_b_group[0], moe_w_router[0], moe_b_router[0])
    w_out0 = ev_w_out[0].astype(BF16)
    x1, xn1, comb1 = _even_out(a, o, x2, w_out0,
                               moe_norm[0].reshape(1, d), wr0, br0)
    moe_w = _moe_params(moe_w_gate, moe_w_up, moe_w_down)
    xa = _moe(xn1, *moe_w, comb1, x1, 0)

    wr1, br1 = _router_params(moe_w_group[1], moe_b_group[1], moe_w_router[1], moe_b_router[1])
    x3, xn3, comb3 = _odd(xa, od_norm[0].reshape(1, d), od_w_in[0].astype(BF16), od_conv_w[0],
                          od_w_pool[0].astype(BF16), od_pool_scale[0].reshape(1, POOL_WIDTH),
                          od_w_out[0].astype(BF16), moe_norm[1].reshape(1, d), wr1, br1, seq)
    xb = _moe(xn3, *moe_w, comb3, x3, 1)
    return xb.reshape(bsz, seq, d)
```

```python
import functools

import jax
import jax.numpy as jnp
from jax import lax
from jax.experimental import pallas as pl
from jax.experimental.pallas import tpu as pltpu

F32 = jnp.float32
BF16 = jnp.bfloat16

D_MODEL = 1024
EPS = 1e-6
CHUNK = 128
GMLP_GROUPS = 4
GMLP_HEAD = 64
GMLP_WIDTH = 256
FOX_HEADS = 12
HEAD_DIM = 64
FOX_WIDTH = 768
CONV_WIDTH = 512
CONV_K = 3
POOL_WINDOWS = (2, 4, 8, 16)
POOL_GROUP = 128
POOL_WIDTH = 512
N_GROUPS = 4
EXPERTS_PER_GROUP = 4
N_EXPERTS = 16
EXPERT_HIDDEN = 256

LANES = 128
V7X_VMEM_LIMIT_BYTES = 56 * 1024 * 1024

TM = 512
TQ = 512
TQ_ONLINE = 256
TM_MOE = 1024
SUBTILES = 2
EVEN_MAIN = 2 * GMLP_WIDTH + 3 * FOX_WIDTH
NEG = -1e30
LOG2E = 1.4426950408889634
FOX_FAST_BOUND_LOG2 = 50.0


def _cparams(sem):
    return pltpu.CompilerParams(dimension_semantics=sem,
                                vmem_limit_bytes=V7X_VMEM_LIMIT_BYTES)


def _dot(a, b):
    return jnp.dot(a, b, preferred_element_type=F32)


def _split_bf16(x, terms):
    parts = []
    r = x
    for _ in range(terms):
        p = r.astype(BF16)
        parts.append(p)
        r = r - p.astype(F32)
    return parts


def _dot_split(x, w_bf16, terms=2, w_left=False):
    acc = None
    for p in _split_bf16(x, terms):
        d = _dot(w_bf16, p) if w_left else _dot(p, w_bf16)
        acc = d if acc is None else acc + d
    return acc


def _rms_rows(x, g):
    ms = jnp.mean(x * x, axis=-1, keepdims=True)
    return x * lax.rsqrt(ms + EPS) * g


def _head_rms(x, bd, g):
    outs = []
    for c in range(x.shape[1] // 256):
        xc = x[:, c * 256:(c + 1) * 256]
        ms = _dot_split(xc * xc, bd)
        outs.append(xc * lax.rsqrt(ms + EPS))
    y = outs[0] if len(outs) == 1 else jnp.concatenate(outs, axis=1)
    return y * g


def _log_sigmoid(x):
    return -(jnp.maximum(-x, 0.0) + jnp.log(1.0 + jnp.exp(-jnp.abs(x))))


def _router(xn_bf16, wr_ref, br_ref):
    r = _dot(xn_bf16, wr_ref[...]) + br_ref[...]
    tm = r.shape[0]
    rt = r.T
    er = rt[0:N_EXPERTS]
    gr = rt[N_EXPERTS:N_EXPERTS + 8]
    grow = lax.broadcasted_iota(jnp.int32, (8, tm), 0).astype(F32)
    erow = lax.broadcasted_iota(jnp.int32, (N_EXPERTS, tm), 0).astype(F32)
    is_g = grow < float(N_GROUPS)
    gl = jnp.where(is_g, gr, NEG)
    gmax = jnp.max(gl, axis=0, keepdims=True)
    gidx = jnp.min(jnp.where(gl == gmax, grow, 999.0), axis=0, keepdims=True)
    gsum = jnp.sum(jnp.where(is_g, jnp.exp(gl - gmax), 0.0), axis=0, keepdims=True)
    gp = 1.0 / gsum
    lo = gidx * float(EXPERTS_PER_GROUP)
    sel = (erow >= lo) & (erow < lo + float(EXPERTS_PER_GROUP))
    el = jnp.where(sel, er, NEG)
    emax = jnp.max(el, axis=0, keepdims=True)
    ee = jnp.where(sel, jnp.exp(el - emax), 0.0)
    ep = ee / jnp.sum(ee, axis=0, keepdims=True)
    ep = jnp.where(sel, ep, -1.0)
    p1 = jnp.max(ep, axis=0, keepdims=True)
    i1 = jnp.min(jnp.where(ep == p1, erow, 999.0), axis=0, keepdims=True)
    ep2 = jnp.where(erow == i1, -1.0, ep)
    p2 = jnp.max(ep2, axis=0, keepdims=True)
    i2 = jnp.min(jnp.where(ep2 == p2, erow, 999.0), axis=0, keepdims=True)
    den = p1 + p2
    comb_t = (jnp.where(erow == i1, gp * (p1 / den), 0.0)
              + jnp.where(erow == i2, gp * (p2 / den), 0.0))
    comb_t = jnp.concatenate([comb_t, jnp.zeros((LANES - N_EXPERTS, tm), F32)], axis=0)
    return comb_t.T


def _even_in_kernel(tiles_per_seq, x_ref, g_ref, w_ref, wf_ref, bf_ref, wtril_ref, bmat_ref,
                    gv_ref, gq_ref, gk_ref, bd_ref, ltri_ref,
                    a_ref, q_ref, k_ref, v_ref, f_ref, fq_ref, wbf_ref, carry_ref):
    i = pl.program_id(0)
    tm = x_ref.shape[0]

    @pl.when(i == 0)
    def _():
        for c in range(0, w_ref.shape[0], 256):
            wbf_ref[:, c:c + 256] = w_ref[c:c + 256, :].T.astype(BF16)

    xn = _rms_rows(x_ref[...], g_ref[...]).astype(BF16)
    h = _dot(xn, wbf_ref[...])
    u = h[:, 0:256]
    v = h[:, 256:512]
    q = h[:, 512:1280]
    k = h[:, 1280:2048]
    val = h[:, 2048:2816]
    f = _dot(xn, wf_ref[...])

    gu = jax.nn.gelu(u)
    vn = _head_rms(jax.nn.gelu(v), bd_ref[...], gv_ref[...])
    lane = lax.broadcasted_iota(jnp.int32, (tm, LANES), 1)
    lo_half = lane < GMLP_HEAD
    bmat = bmat_ref[...]
    pair_out = []
    for pr in range(2):
        vp = vn[:, pr * 128:(pr + 1) * 128]
        v_lo = jnp.where(lo_half, vp, 0.0).astype(BF16)
        v_hi = jnp.where(lo_half, 0.0, vp).astype(BF16)
        chunks = []
        for c in range(tm // CHUNK):
            rs = slice(c * CHUNK, (c + 1) * CHUNK)
            s = (_dot(wtril_ref[2 * pr], v_lo[rs]) + _dot(wtril_ref[2 * pr + 1], v_hi[rs])
                 + bmat[:, pr * 128:(pr + 1) * 128])
            chunks.append(s)
        pair_out.append(jnp.concatenate(chunks, axis=0))
    s_all = jnp.concatenate(pair_out, axis=1)
    a_ref[...] = (gu * s_all).astype(BF16)

    bd = bd_ref[...]
    qn = _head_rms(q, bd, gq_ref[...]) * (HEAD_DIM ** -0.5 * LOG2E)
    for pr in range(FOX_HEADS // 2):
        ls = slice(pr * LANES, (pr + 1) * LANES)
        q_ref[pr] = qn[:, ls].T.astype(BF16)
        v_ref[pr] = val[:, ls].T.astype(BF16)
    k_ref[...] = _head_rms(k, bd, gk_ref[...]).astype(BF16)

    @pl.when(i % tiles_per_seq == 0)
    def _():
        carry_ref[...] = jnp.zeros_like(carry_ref)

    logf = _log_sigmoid(f + bf_ref[...])
    cum = _dot_split(logf, ltri_ref[...], terms=3, w_left=True) + carry_ref[0:1, :]
    carry_ref[0:1, :] = cum[tm - 1:tm, :]
    cum2 = cum * LOG2E
    fq_ref[...] = cum2
    hi, mid, lo = (p.astype(F32) for p in _split_bf16(cum2, 3))
    f_ref[...] = jnp.where(
        lane < FOX_HEADS, hi,
        jnp.where(lane < 2 * FOX_HEADS, mid,
                  jnp.where(lane < 3 * FOX_HEADS, lo,
                            jnp.where(lane < 3 * FOX_HEADS + 3, 1.0, 0.0)))).astype(BF16)


def _even_in(x2, g, w_all, wf, bf, wtril, bmat, gv, gq, gk, bd, ltri, seq):
    t = x2.shape[0]
    n = t // TM
    full = lambda a: pl.BlockSpec(a.shape, lambda i: (0,) * a.ndim)
    row = lambda wdt: pl.BlockSpec((TM, wdt), lambda i: (i, 0))
    pairs = FOX_HEADS // 2
    colT = pl.BlockSpec((pairs, LANES, TM), lambda i: (0, 0, i))
    w_main = pl.BlockSpec((EVEN_MAIN, D_MODEL), lambda i: (0, 0), pipeline_mode=pl.Buffered(1))
    return pl.pallas_call(
        functools.partial(_even_in_kernel, seq // TM),
        grid=(n,),
        in_specs=[row(D_MODEL), full(g), w_main, full(wf), full(bf), full(wtril), full(bmat),
                  full(gv), full(gq), full(gk), full(bd), full(ltri)],
        out_specs=[row(GMLP_WIDTH), colT, row(FOX_WIDTH), colT, row(LANES), row(LANES)],
        out_shape=[jax.ShapeDtypeStruct((t, GMLP_WIDTH), BF16),
                   jax.ShapeDtypeStruct((pairs, LANES, t), BF16),
                   jax.ShapeDtypeStruct((t, FOX_WIDTH), BF16),
                   jax.ShapeDtypeStruct((pairs, LANES, t), BF16),
                   jax.ShapeDtypeStruct((t, LANES), BF16),
                   jax.ShapeDtypeStruct((t, LANES), F32)],
        scratch_shapes=[pltpu.VMEM((D_MODEL, EVEN_MAIN), BF16), pltpu.VMEM((8, LANES), F32)],
        compiler_params=_cparams(("arbitrary",)),
        name="even_in",
    )(x2, g, w_all, wf, bf, wtril, bmat, gv, gq, gk, bd, ltri)


def _fox_query_operand(qt, pr, hh, shift_terms=None):
    tq = qt.shape[1]
    row = lax.broadcasted_iota(jnp.int32, (LANES, tq), 0)
    head = 2 * pr + hh
    in_head = (row >= hh * HEAD_DIM) & (row < (hh + 1) * HEAD_DIM)
    f_rows = (row == head) | (row == head + FOX_HEADS) | (row == head + 2 * FOX_HEADS)
    extra = jnp.where(f_rows, -1.0, 0.0)
    if shift_terms is not None:
        for n, term in enumerate(shift_terms):
            extra = jnp.where(row == 3 * FOX_HEADS + n, term, extra)
    return jnp.concatenate([jnp.where(in_head, qt, 0.0).astype(BF16), extra.astype(BF16)],
                           axis=0)


def _fox_online_kernel(q_ref, k_ref, f_ref, v_ref, o_ref, m_sc, l_sc, acc_sc):
    pr = pl.program_id(1)
    i = pl.program_id(2)
    tq = q_ref.shape[2]
    qt = q_ref[0].astype(F32)
    rhs = [_fox_query_operand(qt, pr, hh) for hh in range(2)]

    m_sc[...] = jnp.full_like(m_sc, NEG)
    l_sc[...] = jnp.zeros_like(l_sc)
    acc_sc[...] = jnp.zeros_like(acc_sc)

    def step(j, masked):
        start = pl.multiple_of(j * tq, tq)
        kaug = jnp.concatenate([k_ref[pl.ds(start, tq), :], f_ref[pl.ds(start, tq), :]],
                               axis=1)
        for hh in range(2):
            s = _dot(kaug, rhs[hh])
            if masked:
                r_id = lax.broadcasted_iota(jnp.int32, (tq, tq), 0)
                c_id = lax.broadcasted_iota(jnp.int32, (tq, tq), 1)
                s = jnp.where(r_id <= c_id, s, NEG)
            m_prev = m_sc[hh]
            m_new = jnp.maximum(m_prev, jnp.max(s, axis=0, keepdims=True))
            alpha = jnp.exp2(m_prev - m_new)
            p = jnp.exp2(s - m_new)
            l_sc[hh] = alpha * l_sc[hh] + jnp.sum(p, axis=0, keepdims=True)
            vt = v_ref[0, hh * HEAD_DIM:(hh + 1) * HEAD_DIM, pl.ds(start, tq)]
            acc_sc[hh] = alpha * acc_sc[hh] + _dot(vt, p.astype(BF16))
            m_sc[hh] = m_new

    def body(j, c):
        step(j, False)
        return c

    lax.fori_loop(0, i, body, 0)
    step(i, True)
    ot = jnp.concatenate([acc_sc[0] / l_sc[0], acc_sc[1] / l_sc[1]], axis=0)
    o_ref[...] = ot.T.astype(o_ref.dtype)


def _fox_fixed_kernel(b_ref, q_ref, k_ref, f_ref, v_ref, fq_ref, o_ref,
                      rhs_sc, z_sc, acc_sc, mask_sc):
    pr = pl.program_id(1)
    tq = z_sc.shape[3]
    tk = z_sc.shape[2]
    nq = rhs_sc.shape[0]
    row = lax.broadcasted_iota(jnp.int32, (LANES, tq), 0)
    for i in range(nq):
        qt = q_ref[0, :, i * tq:(i + 1) * tq].astype(F32)
        fqt = fq_ref[i * tq:(i + 1) * tq, :].T
        for hh in range(2):
            ft = jnp.sum(jnp.where(row == 2 * pr + hh, fqt, 0.0), axis=0, keepdims=True)
            shift = [p.astype(F32) for p in _split_bf16(ft - b_ref[...], 3)]
            rhs_sc[i, hh] = _fox_query_operand(qt, pr, hh, shift)

    mask_sc[...] = jnp.where(lax.broadcasted_iota(jnp.int32, (tk, tq), 0)
                             > lax.broadcasted_iota(jnp.int32, (tk, tq), 1), NEG, 0.0)
    acc_sc[...] = jnp.zeros_like(acc_sc)
    ones = jnp.ones((acc_sc.shape[2] - HEAD_DIM, tk), BF16)

    def scores(i, j, slot):
        start = pl.multiple_of(j * tk, tk)
        kaug = jnp.concatenate([k_ref[pl.ds(start, tk), :], f_ref[pl.ds(start, tk), :]],
                               axis=1)
        for hh in range(2):
            z_sc[slot, hh] = _dot(kaug, rhs_sc[i, hh])

    def consume(i, j, slot, diagonal):
        start = pl.multiple_of(j * tk, tk)
        outs = []
        for hh in range(2):
            z = z_sc[slot, hh]
            if diagonal:
                z = z + mask_sc[...]
            p = jnp.exp2(z).astype(BF16)
            vt = jnp.concatenate(
                [v_ref[0, hh * HEAD_DIM:(hh + 1) * HEAD_DIM, pl.ds(start, tk)], ones], axis=0)
            acc = acc_sc[i, hh] + _dot(vt, p)
            if diagonal:
                outs.append(acc[0:HEAD_DIM] / acc[HEAD_DIM:HEAD_DIM + 1])
            else:
                acc_sc[i, hh] = acc
        if diagonal:
            o_ref[pl.ds(pl.multiple_of(i * tq, tq), tq), :] = (
                jnp.concatenate(outs, axis=0).T.astype(o_ref.dtype))

    def after_offdiag(i, j):
        wrap = j + 1 == i
        i2 = jnp.where(wrap, i + 1, i)
        j2 = jnp.where(wrap, 0, j + 1)
        done = i2 >= nq
        return jnp.where(done, 0, i2), jnp.where(done, 0, j2)

    scores(1, 0, 0)

    def offdiag_body(_, item):
        i0, j0 = item
        i1, j1 = after_offdiag(i0, j0)
        scores(i1, j1, 1)
        consume(i0, j0, 0, False)
        i2, j2 = after_offdiag(i1, j1)
        scores(i2, j2, 0)
        consume(i1, j1, 1, False)
        return i2, j2

    lax.fori_loop(0, nq * (nq - 1) // 4, offdiag_body, (jnp.int32(1), jnp.int32(0)))

    def diag_body(m, c):
        i0 = 2 * m
        scores(i0 + 1, i0 + 1, 1)
        consume(i0, i0, 0, True)
        i2 = jnp.minimum(i0 + 2, nq - 1)
        scores(i2, i2, 0)
        consume(i0 + 1, i0 + 1, 1, True)
        return c

    lax.fori_loop(0, nq // 2, diag_body, 0)


def _fox(qt, k, f3, vt, fq, bound, bsz, seq):
    t = k.shape[0]
    pairs = FOX_HEADS // 2
    out_shape = jax.ShapeDtypeStruct((t, FOX_WIDTH), BF16)
    sem = _cparams(("parallel", "parallel", "arbitrary"))

    def common(tq):
        nq = seq // tq
        in_specs = [pl.BlockSpec((1, LANES, tq), lambda b, p, i: (p, 0, b * nq + i)),
                    pl.BlockSpec((seq, LANES), lambda b, p, i: (b, p)),
                    pl.BlockSpec((seq, LANES), lambda b, p, i: (b, 0)),
                    pl.BlockSpec((1, LANES, seq), lambda b, p, i: (p, 0, b))]
        return nq, in_specs, pl.BlockSpec((tq, LANES), lambda b, p, i: (b * nq + i, p))

    def fixed(brow):
        nq = seq // TQ
        assert nq % 2 == 0 and (nq * (nq - 1)) % 4 == 0
        return pl.pallas_call(
            _fox_fixed_kernel,
            grid=(bsz, pairs),
            in_specs=[pl.BlockSpec((1, TQ), lambda b, p: (0, 0)),
                      pl.BlockSpec((1, LANES, seq), lambda b, p: (p, 0, b)),
                      pl.BlockSpec((seq, LANES), lambda b, p: (b, p)),
                      pl.BlockSpec((seq, LANES), lambda b, p: (b, 0)),
                      pl.BlockSpec((1, LANES, seq), lambda b, p: (p, 0, b)),
                      pl.BlockSpec((seq, LANES), lambda b, p: (b, 0))],
            out_specs=pl.BlockSpec((seq, LANES), lambda b, p: (b, p)),
            out_shape=out_shape,
            scratch_shapes=[pltpu.VMEM((nq, 2, 2 * LANES, TQ), BF16),
                            pltpu.VMEM((2, 2, TQ, TQ), F32),
                            pltpu.VMEM((nq, 2, HEAD_DIM + 16, TQ), F32),
                            pltpu.VMEM((TQ, TQ), F32)],
            compiler_params=_cparams(("parallel", "parallel")), name="fox_attention_fixed",
        )(brow, qt, k, f3, vt, fq)

    def online(brow):
        del brow
        nq, in_specs, out_spec = common(TQ_ONLINE)
        stats = pltpu.VMEM((2, 1, TQ_ONLINE), F32)
        return pl.pallas_call(
            _fox_online_kernel,
            grid=(bsz, pairs, nq),
            in_specs=in_specs, out_specs=out_spec, out_shape=out_shape,
            scratch_shapes=[stats, stats, pltpu.VMEM((2, HEAD_DIM, TQ_ONLINE), F32)],
            compiler_params=sem, name="fox_attention_online",
        )(qt, k, f3, vt)

    brow = jnp.full((1, TQ), bound, F32)
    return lax.cond(bound <= FOX_FAST_BOUND_LOG2, fixed, online, brow)


def _even_out_kernel(a_ref, o_ref, x_ref, w_ref, gm_ref, wr_ref, br_ref,
                     x1_ref, xn_ref, comb_ref):
    sub = x_ref.shape[0] // SUBTILES
    for s in range(SUBTILES):
        rs = slice(s * sub, (s + 1) * sub)
        x1 = x_ref[rs, :] + _dot(jnp.concatenate([a_ref[rs, :], o_ref[rs, :]], axis=1),
                                 w_ref[...])
        x1_ref[rs, :] = x1
        xn = _rms_rows(x1, gm_ref[...]).astype(BF16)
        xn_ref[rs, :] = xn
        comb_ref[rs, :] = _router(xn, wr_ref, br_ref)


def _even_out(a, o, x2, w, gm, wr, br):
    t = x2.shape[0]
    n = t // TM
    full = lambda arr: pl.BlockSpec(arr.shape, lambda i: (0,) * arr.ndim)
    row = lambda wdt: pl.BlockSpec((TM, wdt), lambda i: (i, 0))
    return pl.pallas_call(
        _even_out_kernel,
        grid=(n,),
        in_specs=[row(GMLP_WIDTH), row(FOX_WIDTH), row(D_MODEL), full(w), full(gm),
                  full(wr), full(br)],
        out_specs=[row(D_MODEL), row(D_MODEL), row(LANES)],
        out_shape=[jax.ShapeDtypeStruct((t, D_MODEL), F32),
                   jax.ShapeDtypeStruct((t, D_MODEL), BF16),
                   jax.ShapeDtypeStruct((t, LANES), F32)],
        compiler_params=_cparams(("parallel",)),
        name="even_out",
    )(a, o, x2, w, gm, wr, br)


def _odd_kernel(tiles_per_seq, x_ref, g_ref, w_ref, cw_ref, wp_ref, ps_ref, wo_ref, gm_ref,
                wr_ref, br_ref, x1_ref, xn_ref, comb_ref, zbuf, pbuf):
    i = pl.program_id(0)
    tm = x_ref.shape[0]
    zpad = zbuf.shape[0] - tm
    ppad = pbuf.shape[0] - tm
    @pl.when(i % tiles_per_seq == 0)
    def _():
        zbuf[0:zpad, :] = jnp.zeros((zpad, CONV_WIDTH), F32)
        pbuf[0:ppad, :] = jnp.zeros((ppad, POOL_WIDTH), F32)

    cw = cw_ref[...]
    ps = ps_ref[...]
    sub = tm // SUBTILES
    for s in range(SUBTILES):
        r0 = s * sub
        rs = slice(r0, r0 + sub)
        x = x_ref[rs, :]
        xn = _rms_rows(x, g_ref[...]).astype(BF16)
        h = _dot(xn, w_ref[...])
        bg = h[:, 0:512]
        cg = h[:, 512:1024]
        hc = h[:, 1024:1536]
        p = h[:, 1536:2048]

        z = cg * hc
        zbuf[zpad + r0:zpad + r0 + sub, :] = z
        y = (cw[0:1, :] * zbuf[zpad + r0 - 2:zpad + r0 - 2 + sub, :]
             + cw[1:2, :] * zbuf[zpad + r0 - 1:zpad + r0 - 1 + sub, :]
             + cw[2:3, :] * z)
        c = bg * y

        pbuf[ppad + r0:ppad + r0 + sub, :] = p
        pos = ((i % tiles_per_seq) * tm + r0
               + lax.broadcasted_iota(jnp.int32, (sub, POOL_GROUP), 0)).astype(F32) + 1.0
        pooled_out = []
        for gi, win in enumerate(POOL_WINDOWS):
            ls = slice(gi * POOL_GROUP, (gi + 1) * POOL_GROUP)
            pg = p[:, ls]
            sm = pg
            for sh in range(1, win):
                sm = sm + pbuf[ppad + r0 - sh:ppad + r0 - sh + sub, ls]
            cnt = jnp.minimum(pos, float(win))
            pooled = sm / cnt - pg
            pooled_out.append((_dot(pooled.astype(BF16), wp_ref[gi]) * ps[:, ls]).astype(BF16))
        mix = jnp.concatenate([c.astype(BF16)] + pooled_out, axis=1)
        acc = x + _dot(mix, wo_ref[...])

        x1_ref[rs, :] = acc
        xn2 = _rms_rows(acc, gm_ref[...]).astype(BF16)
        xn_ref[rs, :] = xn2
        comb_ref[rs, :] = _router(xn2, wr_ref, br_ref)

    zbuf[0:zpad, :] = zbuf[tm:tm + zpad, :]
    pbuf[0:ppad, :] = pbuf[tm:tm + ppad, :]


def _odd(x2, g, w, cw, wp, ps, wo, gm, wr, br, seq):
    t = x2.shape[0]
    n = t // TM
    full = lambda arr: pl.BlockSpec(arr.shape, lambda i: (0,) * arr.ndim)
    row = lambda wdt: pl.BlockSpec((TM, wdt), lambda i: (i, 0))
    return pl.pallas_call(
        functools.partial(_odd_kernel, seq // TM),
        grid=(n,),
        in_specs=[row(D_MODEL), full(g), full(w), full(cw), full(wp), full(ps), full(wo),
                  full(gm), full(wr), full(br)],
        out_specs=[row(D_MODEL), row(D_MODEL), row(LANES)],
        out_shape=[jax.ShapeDtypeStruct((t, D_MODEL), F32),
                   jax.ShapeDtypeStruct((t, D_MODEL), BF16),
                   jax.ShapeDtypeStruct((t, LANES), F32)],
        scratch_shapes=[pltpu.VMEM((TM + 8, CONV_WIDTH), F32),
                        pltpu.VMEM((TM + 16, POOL_WIDTH), F32)],
        compiler_params=_cparams(("arbitrary",)),
        name="odd_mixer",
    )(x2, g, w, cw, wp, ps, wo, gm, wr, br)


def _moe_kernel(xn_ref, wg_ref, wu_ref, wd_ref, comb_ref, x1_ref, o_ref):
    g = pl.program_id(1)

    @pl.when(g == 0)
    def _():
        o_ref[...] = x1_ref[...]

    tm = xn_ref.shape[0]
    xn = xn_ref[...]
    lane = lax.broadcasted_iota(jnp.int32, (tm, LANES), 1)
    comb = comb_ref[...]
    hid = []
    for e in range(EXPERTS_PER_GROUP):
        gate = _dot(xn, wg_ref[e].astype(BF16))
        up = _dot(xn, wu_ref[e].astype(BF16))
        cw = jnp.sum(jnp.where(lane == g * EXPERTS_PER_GROUP + e, comb, 0.0), axis=1,
                     keepdims=True)
        hid.append(((gate * jax.nn.sigmoid(gate)) * up * cw).astype(BF16))
    o_ref[...] += _dot(jnp.concatenate(hid, axis=1), wd_ref[0].astype(BF16))


def _moe(xn, wg, wu, wd, comb, x1, layer):
    t = xn.shape[0]
    n = t // TM_MOE
    gh = EXPERTS_PER_GROUP * EXPERT_HIDDEN
    wblk = pl.BlockSpec((None, EXPERTS_PER_GROUP, D_MODEL, EXPERT_HIDDEN),
                        lambda i, g: (layer, g, 0, 0))
    return pl.pallas_call(
        _moe_kernel,
        grid=(n, N_GROUPS),
        in_specs=[pl.BlockSpec((TM_MOE, D_MODEL), lambda i, g: (i, 0)),
                  wblk, wblk,
                  pl.BlockSpec((None, 1, gh, D_MODEL), lambda i, g: (layer, g, 0, 0)),
                  pl.BlockSpec((TM_MOE, LANES), lambda i, g: (i, 0)),
                  pl.BlockSpec((TM_MOE, D_MODEL), lambda i, g: (i, 0))],
        out_specs=pl.BlockSpec((TM_MOE, D_MODEL), lambda i, g: (i, 0)),
        out_shape=jax.ShapeDtypeStruct((t, D_MODEL), F32),
        compiler_params=_cparams(("parallel", "arbitrary")),
        name="moe",
    )(xn, wg, wu, wd, comb, x1)


def _router_params(w_group, b_group, w_router, b_router):
    wr = jnp.concatenate([w_router.reshape(D_MODEL, N_EXPERTS), w_group], axis=1)
    wr = jnp.pad(wr, ((0, 0), (0, LANES - wr.shape[1]))).astype(BF16)
    br = jnp.concatenate([b_router.reshape(N_EXPERTS), b_group])
    br = jnp.pad(br, (0, LANES - br.shape[0])).reshape(1, LANES).astype(F32)
    return wr, br


def _moe_params(w_gate, w_up, w_down):
    wd = w_down.reshape(w_down.shape[0], N_GROUPS, EXPERTS_PER_GROUP * EXPERT_HIDDEN, D_MODEL)
    return w_gate, w_up, wd


def kernel(x, ev_norm, ev_w_in, ev_b_forget, ev_w_s, ev_b_s, ev_g_v, ev_g_q, ev_g_k, ev_w_out,
           od_norm, od_w_in, od_conv_w, od_w_pool, od_pool_scale, od_w_out,
           moe_norm, moe_w_group, moe_b_group, moe_w_router, moe_b_router,
           moe_w_gate, moe_w_up, moe_w_down):
    bsz, seq, d = x.shape
    t = bsz * seq
    x2 = x.reshape(t, d)

    w_f = ev_w_in[0][:, EVEN_MAIN:]
    w_f3 = jnp.pad(jnp.tile(w_f, (1, 3)), ((0, 0), (0, LANES - 3 * FOX_HEADS))).astype(BF16)
    bf = jnp.pad(jnp.tile(ev_b_forget[0], 3), (0, LANES - 3 * FOX_HEADS)).reshape(1, LANES)
    tril = jnp.tril(jnp.ones((CHUNK, CHUNK), F32))
    wtril = (ev_w_s[0] * tril).astype(BF16)
    bmat = jnp.repeat(ev_b_s[0].T, GMLP_HEAD, axis=1)
    gv = ev_g_v[0].reshape(1, GMLP_WIDTH)
    gq = jnp.tile(ev_g_q[0], FOX_HEADS).reshape(1, FOX_WIDTH)
    gk = jnp.tile(ev_g_k[0], FOX_HEADS).reshape(1, FOX_WIDTH)
    blk = jnp.arange(256) // HEAD_DIM
    bd = jnp.where(blk[:, None] == blk[None, :], 1.0 / HEAD_DIM, 0.0).astype(BF16)
    ltri = jnp.tril(jnp.ones((TM, TM), F32)).astype(BF16)

    a, qt, kn, vt, f3, fq = _even_in(x2, ev_norm[0].reshape(1, d), ev_w_in[0].T, w_f3, bf, wtril,
                                     bmat, gv, gq, gk, bd, ltri, seq)
    bound = (HEAD_DIM ** 0.5 * LOG2E * 1.01) * jnp.max(jnp.abs(ev_g_q[0])) * jnp.max(
        jnp.abs(ev_g_k[0]))
    o = _fox(qt, kn, f3, vt, fq, bound, bsz, seq)

    wr0, br0 = _router_params(moe_w_group[0], moe_b_group[0], moe_w_router[0], moe_b_router[0])
    w_out0 = ev_w_out[0].astype(BF16)
    x1, xn1, comb1 = _even_out(a, o, x2, w_out0,
                               moe_norm[0].reshape(1, d), wr0, br0)
    moe_w = _moe_params(moe_w_gate, moe_w_up, moe_w_down)
    xa = _moe(xn1, *moe_w, comb1, x1, 0)

    wr1, br1 = _router_params(moe_w_group[1], moe_b_group[1], moe_w_router[1], moe_b_router[1])
    x3, xn3, comb3 = _odd(xa, od_norm[0].reshape(1, d), od_w_in[0].astype(BF16), od_conv_w[0],
                          od_w_pool[0].astype(BF16), od_pool_scale[0].reshape(1, POOL_WIDTH),
                          od_w_out[0].astype(BF16), moe_norm[1].reshape(1, d), wr1, br1, seq)
    xb = _moe(xn3, *moe_w, comb3, x3, 1)
    return xb.reshape(bsz, seq, d)
```

```python
import functools

import jax
import jax.numpy as jnp
from jax import lax
from jax.experimental import pallas as pl
from jax.experimental.pallas import tpu as pltpu

F32 = jnp.float32
BF16 = jnp.bfloat16

D_MODEL = 1024
EPS = 1e-6
CHUNK = 128
GMLP_GROUPS = 4
GMLP_HEAD = 64
GMLP_WIDTH = 256
FOX_HEADS = 12
HEAD_DIM = 64
FOX_WIDTH = 768
CONV_WIDTH = 512
CONV_K = 3
POOL_WINDOWS = (2, 4, 8, 16)
POOL_GROUP = 128
POOL_WIDTH = 512
N_GROUPS = 4
EXPERTS_PER_GROUP = 4
N_EXPERTS = 16
EXPERT_HIDDEN = 256

LANES = 128
V7X_VMEM_LIMIT_BYTES = 56 * 1024 * 1024

TM = 512
TQ = 512
TQ_ONLINE = 256
TM_MOE = 1024
SUBTILES = 2
EVEN_MAIN = 2 * GMLP_WIDTH + 3 * FOX_WIDTH
NEG = -1e30
LOG2E = 1.4426950408889634
FOX_FAST_BOUND_LOG2 = 50.0


def _cparams(sem):
    return pltpu.CompilerParams(dimension_semantics=sem,
                                vmem_limit_bytes=V7X_VMEM_LIMIT_BYTES)


def _dot(a, b):
    return jnp.dot(a, b, preferred_element_type=F32)


def _split_bf16(x, terms):
    parts = []
    r = x
    for _ in range(terms):
        p = r.astype(BF16)
        parts.append(p)
        r = r - p.astype(F32)
    return parts


def _dot_split(x, w_bf16, terms=2, w_left=False):
    acc = None
    for p in _split_bf16(x, terms):
        d = _dot(w_bf16, p) if w_left else _dot(p, w_bf16)
        acc = d if acc is None else acc + d
    return acc


def _rms_rows(x, g):
    ms = jnp.mean(x * x, axis=-1, keepdims=True)
    return x * lax.rsqrt(ms + EPS) * g


def _head_rms(x, bd, g):
    outs = []
    for c in range(x.shape[1] // 256):
        xc = x[:, c * 256:(c + 1) * 256]
        ms = _dot_split(xc * xc, bd)
        outs.append(xc * lax.rsqrt(ms + EPS))
    y = outs[0] if len(outs) == 1 else jnp.concatenate(outs, axis=1)
    return y * g


def _log_sigmoid(x):
    return -(jnp.maximum(-x, 0.0) + jnp.log(1.0 + jnp.exp(-jnp.abs(x))))


def _router(xn_bf16, wr_ref, br_ref):
    r = _dot(xn_bf16, wr_ref[...]) + br_ref[...]
    tm = r.shape[0]
    rt = r.T
    er = rt[0:N_EXPERTS]
    gr = rt[N_EXPERTS:N_EXPERTS + 8]
    grow = lax.broadcasted_iota(jnp.int32, (8, tm), 0).astype(F32)
    erow = lax.broadcasted_iota(jnp.int32, (N_EXPERTS, tm), 0).astype(F32)
    is_g = grow < float(N_GROUPS)
    gl = jnp.where(is_g, gr, NEG)
    gmax = jnp.max(gl, axis=0, keepdims=True)
    gidx = jnp.min(jnp.where(gl == gmax, grow, 999.0), axis=0, keepdims=True)
    gsum = jnp.sum(jnp.where(is_g, jnp.exp(gl - gmax), 0.0), axis=0, keepdims=True)
    gp = 1.0 / gsum
    lo = gidx * float(EXPERTS_PER_GROUP)
    sel = (erow >= lo) & (erow < lo + float(EXPERTS_PER_GROUP))
    el = jnp.where(sel, er, NEG)
    emax = jnp.max(el, axis=0, keepdims=True)
    ee = jnp.where(sel, jnp.exp(el - emax), 0.0)
    ep = ee / jnp.sum(ee, axis=0, keepdims=True)
    ep = jnp.where(sel, ep, -1.0)
    p1 = jnp.max(ep, axis=0, keepdims=True)
    i1 = jnp.min(jnp.where(ep == p1, erow, 999.0), axis=0, keepdims=True)
    ep2 = jnp.where(erow == i1, -1.0, ep)
    p2 = jnp.max(ep2, axis=0, keepdims=True)
    i2 = jnp.min(jnp.where(ep2 == p2, erow, 999.0), axis=0, keepdims=True)
    den = p1 + p2
    comb_t = (jnp.where(erow == i1, gp * (p1 / den), 0.0)
              + jnp.where(erow == i2, gp * (p2 / den), 0.0))
    comb_t = jnp.concatenate([comb_t, jnp.zeros((LANES - N_EXPERTS, tm), F32)], axis=0)
    return comb_t.T


def _even_in_kernel(tiles_per_seq, x_ref, g_ref, w_ref, wf_ref, bf_ref, wtril_ref, bmat_ref,
                    gv_ref, gq_ref, gk_ref, bd_ref, ltri_ref,
                    a_ref, q_ref, k_ref, v_ref, f_ref, fq_ref, wbf_ref, carry_ref):
    i = pl.program_id(0)
    tm = x_ref.shape[0]

    @pl.when(i == 0)
    def _():
        for c in range(0, w_ref.shape[0], 256):
            wbf_ref[:, c:c + 256] = w_ref[c:c + 256, :].T.astype(BF16)

    @pl.when(i % tiles_per_seq == 0)
    def _():
        carry_ref[...] = jnp.zeros_like(carry_ref)

    sub = tm // SUBTILES
    lane = lax.broadcasted_iota(jnp.int32, (sub, LANES), 1)
    lo_half = lane < GMLP_HEAD
    bmat = bmat_ref[...]
    bd = bd_ref[...]
    for st in range(SUBTILES):
        rows = slice(st * sub, (st + 1) * sub)
        xn = _rms_rows(x_ref[rows, :], g_ref[...]).astype(BF16)
        h = _dot(xn, wbf_ref[...])
        u = h[:, 0:256]
        v = h[:, 256:512]
        q = h[:, 512:1280]
        k = h[:, 1280:2048]
        val = h[:, 2048:2816]
        f = _dot(xn, wf_ref[...])

        gu = jax.nn.gelu(u)
        vn = _head_rms(jax.nn.gelu(v), bd, gv_ref[...])
        pair_out = []
        for pr in range(2):
            vp = vn[:, pr * 128:(pr + 1) * 128]
            v_lo = jnp.where(lo_half, vp, 0.0).astype(BF16)
            v_hi = jnp.where(lo_half, 0.0, vp).astype(BF16)
            chunks = []
            for c in range(sub // CHUNK):
                rs = slice(c * CHUNK, (c + 1) * CHUNK)
                s = (_dot(wtril_ref[2 * pr], v_lo[rs]) + _dot(wtril_ref[2 * pr + 1], v_hi[rs])
                     + bmat[:, pr * 128:(pr + 1) * 128])
                chunks.append(s)
            pair_out.append(jnp.concatenate(chunks, axis=0))
        s_all = jnp.concatenate(pair_out, axis=1)
        a_ref[rows, :] = (gu * s_all).astype(BF16)

        qn = _head_rms(q, bd, gq_ref[...]) * (HEAD_DIM ** -0.5 * LOG2E)
        for pr in range(FOX_HEADS // 2):
            ls = slice(pr * LANES, (pr + 1) * LANES)
            q_ref[pr, :, rows] = qn[:, ls].T.astype(BF16)
            v_ref[pr, :, rows] = val[:, ls].T.astype(BF16)
        k_ref[rows, :] = _head_rms(k, bd, gk_ref[...]).astype(BF16)

        logf = _log_sigmoid(f + bf_ref[...])
        cum = (_dot_split(logf, ltri_ref[0:sub, 0:sub], terms=3, w_left=True)
               + carry_ref[0:1, :])
        carry_ref[0:1, :] = cum[sub - 1:sub, :]
        cum2 = cum * LOG2E
        fq_ref[rows, :] = cum2
        hi, mid, lo = (p.astype(F32) for p in _split_bf16(cum2, 3))
        f_ref[rows, :] = jnp.where(
            lane < FOX_HEADS, hi,
            jnp.where(lane < 2 * FOX_HEADS, mid,
                      jnp.where(lane < 3 * FOX_HEADS, lo,
                                jnp.where(lane < 3 * FOX_HEADS + 3, 1.0, 0.0)))).astype(BF16)


def _even_in(x2, g, w_all, wf, bf, wtril, bmat, gv, gq, gk, bd, ltri, seq):
    t = x2.shape[0]
    n = t // TM
    full = lambda a: pl.BlockSpec(a.shape, lambda i: (0,) * a.ndim)
    row = lambda wdt: pl.BlockSpec((TM, wdt), lambda i: (i, 0))
    pairs = FOX_HEADS // 2
    colT = pl.BlockSpec((pairs, LANES, TM), lambda i: (0, 0, i))
    w_main = pl.BlockSpec((EVEN_MAIN, D_MODEL), lambda i: (0, 0), pipeline_mode=pl.Buffered(1))
    return pl.pallas_call(
        functools.partial(_even_in_kernel, seq // TM),
        grid=(n,),
        in_specs=[row(D_MODEL), full(g), w_main, full(wf), full(bf), full(wtril), full(bmat),
                  full(gv), full(gq), full(gk), full(bd), full(ltri)],
        out_specs=[row(GMLP_WIDTH), colT, row(FOX_WIDTH), colT, row(LANES), row(LANES)],
        out_shape=[jax.ShapeDtypeStruct((t, GMLP_WIDTH), BF16),
                   jax.ShapeDtypeStruct((pairs, LANES, t), BF16),
                   jax.ShapeDtypeStruct((t, FOX_WIDTH), BF16),
                   jax.ShapeDtypeStruct((pairs, LANES, t), BF16),
                   jax.ShapeDtypeStruct((t, LANES), BF16),
                   jax.ShapeDtypeStruct((t, LANES), F32)],
        scratch_shapes=[pltpu.VMEM((D_MODEL, EVEN_MAIN), BF16), pltpu.VMEM((8, LANES), F32)],
        compiler_params=_cparams(("arbitrary",)),
        name="even_in",
    )(x2, g, w_all, wf, bf, wtril, bmat, gv, gq, gk, bd, ltri)


def _fox_query_operand(qt, pr, hh, shift_terms=None):
    tq = qt.shape[1]
    row = lax.broadcasted_iota(jnp.int32, (LANES, tq), 0)
    head = 2 * pr + hh
    in_head = (row >= hh * HEAD_DIM) & (row < (hh + 1) * HEAD_DIM)
    f_rows = (row == head) | (row == head + FOX_HEADS) | (row == head + 2 * FOX_HEADS)
    extra = jnp.where(f_rows, -1.0, 0.0)
    if shift_terms is not None:
        for n, term in enumerate(shift_terms):
            extra = jnp.where(row == 3 * FOX_HEADS + n, term, extra)
    return jnp.concatenate([jnp.where(in_head, qt, 0.0).astype(BF16), extra.astype(BF16)],
                           axis=0)


def _fox_online_kernel(q_ref, k_ref, f_ref, v_ref, o_ref, m_sc, l_sc, acc_sc):
    pr = pl.program_id(1)
    i = pl.program_id(2)
    tq = q_ref.shape[2]
    qt = q_ref[0].astype(F32)
    rhs = [_fox_query_operand(qt, pr, hh) for hh in range(2)]

    m_sc[...] = jnp.full_like(m_sc, NEG)
    l_sc[...] = jnp.zeros_like(l_sc)
    acc_sc[...] = jnp.zeros_like(acc_sc)

    def step(j, masked):
        start = pl.multiple_of(j * tq, tq)
        kaug = jnp.concatenate([k_ref[pl.ds(start, tq), :], f_ref[pl.ds(start, tq), :]],
                               axis=1)
        for hh in range(2):
            s = _dot(kaug, rhs[hh])
            if masked:
                r_id = lax.broadcasted_iota(jnp.int32, (tq, tq), 0)
                c_id = lax.broadcasted_iota(jnp.int32, (tq, tq), 1)
                s = jnp.where(r_id <= c_id, s, NEG)
            m_prev = m_sc[hh]
            m_new = jnp.maximum(m_prev, jnp.max(s, axis=0, keepdims=True))
            alpha = jnp.exp2(m_prev - m_new)
            p = jnp.exp2(s - m_new)
            l_sc[hh] = alpha * l_sc[hh] + jnp.sum(p, axis=0, keepdims=True)
            vt = v_ref[0, hh * HEAD_DIM:(hh + 1) * HEAD_DIM, pl.ds(start, tq)]
            acc_sc[hh] = alpha * acc_sc[hh] + _dot(vt, p.astype(BF16))
            m_sc[hh] = m_new

    def body(j, c):
        step(j, False)
        return c

    lax.fori_loop(0, i, body, 0)
    step(i, True)
    ot = jnp.concatenate([acc_sc[0] / l_sc[0], acc_sc[1] / l_sc[1]], axis=0)
    o_ref[...] = ot.T.astype(o_ref.dtype)


def _fox_fixed_kernel(b_ref, q_ref, k_ref, f_ref, v_ref, fq_ref, o_ref,
                      rhs_sc, z_sc, acc_sc, mask_sc):
    pr = pl.program_id(1)
    tq = z_sc.shape[3]
    tk = z_sc.shape[2]
    nq = rhs_sc.shape[0]
    row = lax.broadcasted_iota(jnp.int32, (LANES, tq), 0)
    for i in range(nq):
        qt = q_ref[0, :, i * tq:(i + 1) * tq].astype(F32)
        fqt = fq_ref[i * tq:(i + 1) * tq, :].T
        for hh in range(2):
            ft = jnp.sum(jnp.where(row == 2 * pr + hh, fqt, 0.0), axis=0, keepdims=True)
            shift = [p.astype(F32) for p in _split_bf16(ft - b_ref[...], 3)]
            rhs_sc[i, hh] = _fox_query_operand(qt, pr, hh, shift)

    mask_sc[...] = jnp.where(lax.broadcasted_iota(jnp.int32, (tk, tq), 0)
                             > lax.broadcasted_iota(jnp.int32, (tk, tq), 1), NEG, 0.0)
    acc_sc[...] = jnp.zeros_like(acc_sc)
    ones = jnp.ones((acc_sc.shape[2] - HEAD_DIM, tk), BF16)

    half = tk // 2

    def scores(i, j, slot, diagonal=False):
        start = pl.multiple_of(j * tk, tk)
        kaug = jnp.concatenate([k_ref[pl.ds(start, tk), :], f_ref[pl.ds(start, tk), :]],
                               axis=1)
        for hh in range(2):
            if diagonal:
                z_sc[slot, hh, 0:half, :] = _dot(kaug[0:half], rhs_sc[i, hh])
                z_sc[slot, hh, half:tk, half:tq] = _dot(kaug[half:tk], rhs_sc[i, hh, :, half:tq])
            else:
                z_sc[slot, hh] = _dot(kaug, rhs_sc[i, hh])

    def values(hh, start, size):
        return jnp.concatenate(
            [v_ref[0, hh * HEAD_DIM:(hh + 1) * HEAD_DIM, pl.ds(start, size)], ones[:, 0:size]],
            axis=0)

    def consume(i, j, slot):
        start = pl.multiple_of(j * tk, tk)
        for hh in range(2):
            p = jnp.exp2(z_sc[slot, hh]).astype(BF16)
            acc_sc[i, hh] += _dot(values(hh, start, tk), p)

    def consume_diagonal(i, slot):
        start = pl.multiple_of(i * tk, tk)
        outs = []
        for hh in range(2):
            p_top = jnp.exp2(z_sc[slot, hh, 0:half, :] + mask_sc[0:half, :]).astype(BF16)
            p_bot = jnp.exp2(z_sc[slot, hh, half:tk, half:tq]
                             + mask_sc[half:tk, half:tq]).astype(BF16)
            acc = acc_sc[i, hh] + _dot(values(hh, start, half), p_top)
            right = acc[:, half:tq] + _dot(values(hh, start + half, half), p_bot)
            acc = jnp.concatenate([acc[:, 0:half], right], axis=1)
            outs.append(acc[0:HEAD_DIM] / acc[HEAD_DIM:HEAD_DIM + 1])
        o_ref[pl.ds(pl.multiple_of(i * tq, tq), tq), :] = (
            jnp.concatenate(outs, axis=0).T.astype(o_ref.dtype))

    def after_offdiag(i, j):
        wrap = j + 1 == i
        i2 = jnp.where(wrap, i + 1, i)
        j2 = jnp.where(wrap, 0, j + 1)
        done = i2 >= nq
        return jnp.where(done, 0, i2), jnp.where(done, 0, j2)

    scores(1, 0, 0)

    def offdiag_body(_, item):
        i0, j0 = item
        i1, j1 = after_offdiag(i0, j0)
        scores(i1, j1, 1)
        consume(i0, j0, 0)
        i2, j2 = after_offdiag(i1, j1)
        scores(i2, j2, 0)
        consume(i1, j1, 1)
        return i2, j2

    lax.fori_loop(0, nq * (nq - 1) // 4, offdiag_body, (jnp.int32(1), jnp.int32(0)))

    def diag_body(m, c):
        i0 = 2 * m
        scores(i0 + 1, i0 + 1, 1, diagonal=True)
        consume_diagonal(i0, 0)
        i2 = jnp.minimum(i0 + 2, nq - 1)
        scores(i2, i2, 0, diagonal=True)
        consume_diagonal(i0 + 1, 1)
        return c

    lax.fori_loop(0, nq // 2, diag_body, 0)


def _fox(qt, k, f3, vt, fq, bound, bsz, seq):
    t = k.shape[0]
    pairs = FOX_HEADS // 2
    out_shape = jax.ShapeDtypeStruct((t, FOX_WIDTH), BF16)
    sem = _cparams(("parallel", "parallel", "arbitrary"))

    def common(tq):
        nq = seq // tq
        in_specs = [pl.BlockSpec((1, LANES, tq), lambda b, p, i: (p, 0, b * nq + i)),
                    pl.BlockSpec((seq, LANES), lambda b, p, i: (b, p)),
                    pl.BlockSpec((seq, LANES), lambda b, p, i: (b, 0)),
                    pl.BlockSpec((1, LANES, seq), lambda b, p, i: (p, 0, b))]
        return nq, in_specs, pl.BlockSpec((tq, LANES), lambda b, p, i: (b * nq + i, p))

    def fixed(brow):
        nq = seq // TQ
        assert nq % 2 == 0 and (nq * (nq - 1)) % 4 == 0
        return pl.pallas_call(
            _fox_fixed_kernel,
            grid=(bsz, pairs),
            in_specs=[pl.BlockSpec((1, TQ), lambda b, p: (0, 0)),
                      pl.BlockSpec((1, LANES, seq), lambda b, p: (p, 0, b)),
                      pl.BlockSpec((seq, LANES), lambda b, p: (b, p)),
                      pl.BlockSpec((seq, LANES), lambda b, p: (b, 0)),
                      pl.BlockSpec((1, LANES, seq), lambda b, p: (p, 0, b)),
                      pl.BlockSpec((seq, LANES), lambda b, p: (b, 0))],
            out_specs=pl.BlockSpec((seq, LANES), lambda b, p: (b, p)),
            out_shape=out_shape,
            scratch_shapes=[pltpu.VMEM((nq, 2, 2 * LANES, TQ), BF16),
                            pltpu.VMEM((2, 2, TQ, TQ), F32),
                            pltpu.VMEM((nq, 2, HEAD_DIM + 16, TQ), F32),
                            pltpu.VMEM((TQ, TQ), F32)],
            compiler_params=_cparams(("parallel", "parallel")), name="fox_attention_fixed",
        )(brow, qt, k, f3, vt, fq)

    def online(brow):
        del brow
        nq, in_specs, out_spec = common(TQ_ONLINE)
        stats = pltpu.VMEM((2, 1, TQ_ONLINE), F32)
        return pl.pallas_call(
            _fox_online_kernel,
            grid=(bsz, pairs, nq),
            in_specs=in_specs, out_specs=out_spec, out_shape=out_shape,
            scratch_shapes=[stats, stats, pltpu.VMEM((2, HEAD_DIM, TQ_ONLINE), F32)],
            compiler_params=sem, name="fox_attention_online",
        )(qt, k, f3, vt)

    brow = jnp.full((1, TQ), bound, F32)
    return lax.cond(bound <= FOX_FAST_BOUND_LOG2, fixed, online, brow)


def _even_out_kernel(a_ref, o_ref, x_ref, w_ref, gm_ref, wr_ref, br_ref,
                     x1_ref, xn_ref, comb_ref):
    sub = x_ref.shape[0] // SUBTILES
    for s in range(SUBTILES):
        rs = slice(s * sub, (s + 1) * sub)
        x1 = x_ref[rs, :] + _dot(jnp.concatenate([a_ref[rs, :], o_ref[rs, :]], axis=1),
                                 w_ref[...])
        x1_ref[rs, :] = x1
        xn = _rms_rows(x1, gm_ref[...]).astype(BF16)
        xn_ref[rs, :] = xn
        comb_ref[rs, :] = _router(xn, wr_ref, br_ref)


def _even_out(a, o, x2, w, gm, wr, br):
    t = x2.shape[0]
    n = t // TM
    full = lambda arr: pl.BlockSpec(arr.shape, lambda i: (0,) * arr.ndim)
    row = lambda wdt: pl.BlockSpec((TM, wdt), lambda i: (i, 0))
    return pl.pallas_call(
        _even_out_kernel,
        grid=(n,),
        in_specs=[row(GMLP_WIDTH), row(FOX_WIDTH), row(D_MODEL), full(w), full(gm),
                  full(wr), full(br)],
        out_specs=[row(D_MODEL), row(D_MODEL), row(LANES)],
        out_shape=[jax.ShapeDtypeStruct((t, D_MODEL), F32),
                   jax.ShapeDtypeStruct((t, D_MODEL), BF16),
                   jax.ShapeDtypeStruct((t, LANES), F32)],
        compiler_params=_cparams(("parallel",)),
        name="even_out",
    )(a, o, x2, w, gm, wr, br)


def _odd_kernel(tiles_per_seq, x_ref, g_ref, w_ref, cw_ref, wp_ref, ps_ref, wo_ref, gm_ref,
                wr_ref, br_ref, x1_ref, xn_ref, comb_ref, zbuf, pbuf):
    i = pl.program_id(0)
    tm = x_ref.shape[0]
    zpad = zbuf.shape[0] - tm
    ppad = pbuf.shape[0] - tm
    @pl.when(i % tiles_per_seq == 0)
    def _():
        zbuf[0:zpad, :] = jnp.zeros((zpad, CONV_WIDTH), F32)
        pbuf[0:ppad, :] = jnp.zeros((ppad, POOL_WIDTH), F32)

    cw = cw_ref[...]
    ps = ps_ref[...]
    sub = tm // SUBTILES
    for s in range(SUBTILES):
        r0 = s * sub
        rs = slice(r0, r0 + sub)
        x = x_ref[rs, :]
        xn = _rms_rows(x, g_ref[...]).astype(BF16)
        h = _dot(xn, w_ref[...])
        bg = h[:, 0:512]
        cg = h[:, 512:1024]
        hc = h[:, 1024:1536]
        p = h[:, 1536:2048]

        z = cg * hc
        zbuf[zpad + r0:zpad + r0 + sub, :] = z
        y = (cw[0:1, :] * zbuf[zpad + r0 - 2:zpad + r0 - 2 + sub, :]
             + cw[1:2, :] * zbuf[zpad + r0 - 1:zpad + r0 - 1 + sub, :]
             + cw[2:3, :] * z)
        c = bg * y

        pbuf[ppad + r0:ppad + r0 + sub, :] = p
        pos = ((i % tiles_per_seq) * tm + r0
               + lax.broadcasted_iota(jnp.int32, (sub, POOL_GROUP), 0)).astype(F32) + 1.0
        pooled_out = []
        for gi, win in enumerate(POOL_WINDOWS):
            ls = slice(gi * POOL_GROUP, (gi + 1) * POOL_GROUP)
            pg = p[:, ls]
            sm = pg
            for sh in range(1, win):
                sm = sm + pbuf[ppad + r0 - sh:ppad + r0 - sh + sub, ls]
            cnt = jnp.minimum(pos, float(win))
            pooled = sm / cnt - pg
            pooled_out.append((_dot(pooled.astype(BF16), wp_ref[gi]) * ps[:, ls]).astype(BF16))
        mix = jnp.concatenate([c.astype(BF16)] + pooled_out, axis=1)
        acc = x + _dot(mix, wo_ref[...])

        x1_ref[rs, :] = acc
        xn2 = _rms_rows(acc, gm_ref[...]).astype(BF16)
        xn_ref[rs, :] = xn2
        comb_ref[rs, :] = _router(xn2, wr_ref, br_ref)

    zbuf[0:zpad, :] = zbuf[tm:tm + zpad, :]
    pbuf[0:ppad, :] = pbuf[tm:tm + ppad, :]


def _odd(x2, g, w, cw, wp, ps, wo, gm, wr, br, seq):
    t = x2.shape[0]
    n = t // TM
    full = lambda arr: pl.BlockSpec(arr.shape, lambda i: (0,) * arr.ndim)
    row = lambda wdt: pl.BlockSpec((TM, wdt), lambda i: (i, 0))
    return pl.pallas_call(
        functools.partial(_odd_kernel, seq // TM),
        grid=(n,),
        in_specs=[row(D_MODEL), full(g), full(w), full(cw), full(wp), full(ps), full(wo),
                  full(gm), full(wr), full(br)],
        out_specs=[row(D_MODEL), row(D_MODEL), row(LANES)],
        out_shape=[jax.ShapeDtypeStruct((t, D_MODEL), F32),
                   jax.ShapeDtypeStruct((t, D_MODEL), BF16),
                   jax.ShapeDtypeStruct((t, LANES), F32)],
        scratch_shapes=[pltpu.VMEM((TM + 8, CONV_WIDTH), F32),
                        pltpu.VMEM((TM + 16, POOL_WIDTH), F32)],
        compiler_params=_cparams(("arbitrary",)),
        name="odd_mixer",
    )(x2, g, w, cw, wp, ps, wo, gm, wr, br)


def _moe_kernel(xn_ref, wg_ref, wu_ref, wd_ref, comb_ref, x1_ref, o_ref):
    g = pl.program_id(1)

    @pl.when(g == 0)
    def _():
        o_ref[...] = x1_ref[...]

    tm = xn_ref.shape[0]
    xn = xn_ref[...]
    lane = lax.broadcasted_iota(jnp.int32, (tm, LANES), 1)
    comb = comb_ref[...]
    hid = []
    for e in range(EXPERTS_PER_GROUP):
        gate = _dot(xn, wg_ref[e].astype(BF16))
        up = _dot(xn, wu_ref[e].astype(BF16))
        cw = jnp.sum(jnp.where(lane == g * EXPERTS_PER_GROUP + e, comb, 0.0), axis=1,
                     keepdims=True)
        hid.append(((gate * jax.nn.sigmoid(gate)) * up * cw).astype(BF16))
    o_ref[...] += _dot(jnp.concatenate(hid, axis=1), wd_ref[0].astype(BF16))


def _moe(xn, wg, wu, wd, comb, x1, layer):
    t = xn.shape[0]
    n = t // TM_MOE
    gh = EXPERTS_PER_GROUP * EXPERT_HIDDEN
    wblk = pl.BlockSpec((None, EXPERTS_PER_GROUP, D_MODEL, EXPERT_HIDDEN),
                        lambda i, g: (layer, g, 0, 0))
    return pl.pallas_call(
        _moe_kernel,
        grid=(n, N_GROUPS),
        in_specs=[pl.BlockSpec((TM_MOE, D_MODEL), lambda i, g: (i, 0)),
                  wblk, wblk,
                  pl.BlockSpec((None, 1, gh, D_MODEL), lambda i, g: (layer, g, 0, 0)),
                  pl.BlockSpec((TM_MOE, LANES), lambda i, g: (i, 0)),
                  pl.BlockSpec((TM_MOE, D_MODEL), lambda i, g: (i, 0))],
        out_specs=pl.BlockSpec((TM_MOE, D_MODEL), lambda i, g: (i, 0)),
        out_shape=jax.ShapeDtypeStruct((t, D_MODEL), F32),
        compiler_params=_cparams(("parallel", "arbitrary")),
        name="moe",
    )(xn, wg, wu, wd, comb, x1)


def _router_params(w_group, b_group, w_router, b_router):
    wr = jnp.concatenate([w_router.reshape(D_MODEL, N_EXPERTS), w_group], axis=1)
    wr = jnp.pad(wr, ((0, 0), (0, LANES - wr.shape[1]))).astype(BF16)
    br = jnp.concatenate([b_router.reshape(N_EXPERTS), b_group])
    br = jnp.pad(br, (0, LANES - br.shape[0])).reshape(1, LANES).astype(F32)
    return wr, br


def _moe_params(w_gate, w_up, w_down):
    wd = w_down.reshape(w_down.shape[0], N_GROUPS, EXPERTS_PER_GROUP * EXPERT_HIDDEN, D_MODEL)
    return w_gate, w_up, wd


def kernel(x, ev_norm, ev_w_in, ev_b_forget, ev_w_s, ev_b_s, ev_g_v, ev_g_q, ev_g_k, ev_w_out,
           od_norm, od_w_in, od_conv_w, od_w_pool, od_pool_scale, od_w_out,
           moe_norm, moe_w_group, moe_b_group, moe_w_router, moe_b_router,
           moe_w_gate, moe_w_up, moe_w_down):
    bsz, seq, d = x.shape
    t = bsz * seq
    x2 = x.reshape(t, d)

    w_f = ev_w_in[0][:, EVEN_MAIN:]
    w_f3 = jnp.pad(jnp.tile(w_f, (1, 3)), ((0, 0), (0, LANES - 3 * FOX_HEADS))).astype(BF16)
    bf = jnp.pad(jnp.tile(ev_b_forget[0], 3), (0, LANES - 3 * FOX_HEADS)).reshape(1, LANES)
    tril = jnp.tril(jnp.ones((CHUNK, CHUNK), F32))
    wtril = (ev_w_s[0] * tril).astype(BF16)
    bmat = jnp.repeat(ev_b_s[0].T, GMLP_HEAD, axis=1)
    gv = ev_g_v[0].reshape(1, GMLP_WIDTH)
    gq = jnp.tile(ev_g_q[0], FOX_HEADS).reshape(1, FOX_WIDTH)
    gk = jnp.tile(ev_g_k[0], FOX_HEADS).reshape(1, FOX_WIDTH)
    blk = jnp.arange(256) // HEAD_DIM
    bd = jnp.where(blk[:, None] == blk[None, :], 1.0 / HEAD_DIM, 0.0).astype(BF16)
    ltri = jnp.tril(jnp.ones((TM, TM), F32)).astype(BF16)

    a, qt, kn, vt, f3, fq = _even_in(x2, ev_norm[0].reshape(1, d), ev_w_in[0].T, w_f3, bf, wtril,
                                     bmat, gv, gq, gk, bd, ltri, seq)
    bound = (HEAD_DIM ** 0.5 * LOG2E * 1.01) * jnp.max(jnp.abs(ev_g_q[0])) * jnp.max(
        jnp.abs(ev_g_k[0]))
    o = _fox(qt, kn, f3, vt, fq, bound, bsz, seq)

    wr0, br0 = _router_params(moe_w_group[0], moe_b_group[0], moe_w_router[0], moe_b_router[0])
    w_out0 = ev_w_out[0].astype(BF16)
    x1, xn1, comb1 = _even_out(a, o, x2, w_out0,
                               moe_norm[0].reshape(1, d), wr0, br0)
    moe_w = _moe_params(moe_w_gate, moe_w_up, moe_w_down)
    xa = _moe(xn1, *moe_w, comb1, x1, 0)

    wr1, br1 = _router_params(moe_w_group[1], moe_b_group[1], moe_w_router[1], moe_b_router[1])
    x3, xn3, comb3 = _odd(xa, od_norm[0].reshape(1, d), od_w_in[0].astype(BF16), od_conv_w[0],
                          od_w_pool[0].astype(BF16), od_pool_scale[0].reshape(1, POOL_WIDTH),
                          od_w_out[0].astype(BF16), moe_norm[1].reshape(1, d), wr1, br1, seq)
    xb = _moe(xn3, *moe_w, comb3, x3, 1)
    return xb.reshape(bsz, seq, d)
```

```python
import functools

import jax
import jax.numpy as jnp
from jax import lax
from jax.experimental import pallas as pl
from jax.experimental.pallas import tpu as pltpu

F32 = jnp.float32
BF16 = jnp.bfloat16

D_MODEL = 1024
EPS = 1e-6
CHUNK = 128
GMLP_GROUPS = 4
GMLP_HEAD = 64
GMLP_WIDTH = 256
FOX_HEADS = 12
HEAD_DIM = 64
FOX_WIDTH = 768
CONV_WIDTH = 512
CONV_K = 3
POOL_WINDOWS = (2, 4, 8, 16)
POOL_GROUP = 128
POOL_WIDTH = 512
N_GROUPS = 4
EXPERTS_PER_GROUP = 4
N_EXPERTS = 16
EXPERT_HIDDEN = 256

LANES = 128
V7X_VMEM_LIMIT_BYTES = 56 * 1024 * 1024

TM = 512
TQ = 512
TQ_ONLINE = 256
ITEMS_PER_ITER = 4
TM_MOE = 1024
SUBTILES = 2
EVEN_MAIN = 2 * GMLP_WIDTH + 3 * FOX_WIDTH
NEG = -1e30
LOG2E = 1.4426950408889634
FOX_FAST_BOUND_LOG2 = 50.0


def _cparams(sem):
    return pltpu.CompilerParams(dimension_semantics=sem,
                                vmem_limit_bytes=V7X_VMEM_LIMIT_BYTES)


def _dot(a, b):
    return jnp.dot(a, b, preferred_element_type=F32)


def _split_bf16(x, terms):
    parts = []
    r = x
    for _ in range(terms):
        p = r.astype(BF16)
        parts.append(p)
        r = r - p.astype(F32)
    return parts


def _dot_split(x, w_bf16, terms=2, w_left=False):
    acc = None
    for p in _split_bf16(x, terms):
        d = _dot(w_bf16, p) if w_left else _dot(p, w_bf16)
        acc = d if acc is None else acc + d
    return acc


def _rms_rows(x, g):
    ms = jnp.mean(x * x, axis=-1, keepdims=True)
    return x * lax.rsqrt(ms + EPS) * g


def _head_rms(x, bd, g):
    outs = []
    for c in range(x.shape[1] // 256):
        xc = x[:, c * 256:(c + 1) * 256]
        ms = _dot_split(xc * xc, bd)
        outs.append(xc * lax.rsqrt(ms + EPS))
    y = outs[0] if len(outs) == 1 else jnp.concatenate(outs, axis=1)
    return y * g


def _log_sigmoid(x):
    return -(jnp.maximum(-x, 0.0) + jnp.log(1.0 + jnp.exp(-jnp.abs(x))))


def _router(xn_bf16, wr_ref, br_ref):
    r = _dot(xn_bf16, wr_ref[...]) + br_ref[...]
    tm = r.shape[0]
    rt = r.T
    er = rt[0:N_EXPERTS]
    gr = rt[N_EXPERTS:N_EXPERTS + 8]
    grow = lax.broadcasted_iota(jnp.int32, (8, tm), 0).astype(F32)
    erow = lax.broadcasted_iota(jnp.int32, (N_EXPERTS, tm), 0).astype(F32)
    is_g = grow < float(N_GROUPS)
    gl = jnp.where(is_g, gr, NEG)
    gmax = jnp.max(gl, axis=0, keepdims=True)
    gidx = jnp.min(jnp.where(gl == gmax, grow, 999.0), axis=0, keepdims=True)
    gsum = jnp.sum(jnp.where(is_g, jnp.exp(gl - gmax), 0.0), axis=0, keepdims=True)
    gp = 1.0 / gsum
    lo = gidx * float(EXPERTS_PER_GROUP)
    sel = (erow >= lo) & (erow < lo + float(EXPERTS_PER_GROUP))
    el = jnp.where(sel, er, NEG)
    emax = jnp.max(el, axis=0, keepdims=True)
    ee = jnp.where(sel, jnp.exp(el - emax), 0.0)
    ep = ee / jnp.sum(ee, axis=0, keepdims=True)
    ep = jnp.where(sel, ep, -1.0)
    p1 = jnp.max(ep, axis=0, keepdims=True)
    i1 = jnp.min(jnp.where(ep == p1, erow, 999.0), axis=0, keepdims=True)
    ep2 = jnp.where(erow == i1, -1.0, ep)
    p2 = jnp.max(ep2, axis=0, keepdims=True)
    i2 = jnp.min(jnp.where(ep2 == p2, erow, 999.0), axis=0, keepdims=True)
    den = p1 + p2
    comb_t = (jnp.where(erow == i1, gp * (p1 / den), 0.0)
              + jnp.where(erow == i2, gp * (p2 / den), 0.0))
    comb_t = jnp.concatenate([comb_t, jnp.zeros((LANES - N_EXPERTS, tm), F32)], axis=0)
    return comb_t.T


def _even_in_kernel(tiles_per_seq, x_ref, g_ref, w_ref, wf_ref, bf_ref, wtril_ref, bmat_ref,
                    gv_ref, gq_ref, gk_ref, bd_ref, ltri_ref,
                    a_ref, q_ref, k_ref, v_ref, f_ref, fq_ref, wbf_ref, carry_ref):
    i = pl.program_id(0)
    tm = x_ref.shape[0]

    @pl.when(i == 0)
    def _():
        for c in range(0, w_ref.shape[0], 256):
            wbf_ref[:, c:c + 256] = w_ref[c:c + 256, :].T.astype(BF16)

    @pl.when(i % tiles_per_seq == 0)
    def _():
        carry_ref[...] = jnp.zeros_like(carry_ref)

    sub = tm // SUBTILES
    lane = lax.broadcasted_iota(jnp.int32, (sub, LANES), 1)
    lo_half = lane < GMLP_HEAD
    bmat = bmat_ref[...]
    bd = bd_ref[...]
    for st in range(SUBTILES):
        rows = slice(st * sub, (st + 1) * sub)
        xn = _rms_rows(x_ref[rows, :], g_ref[...]).astype(BF16)
        h = _dot(xn, wbf_ref[...])
        u = h[:, 0:256]
        v = h[:, 256:512]
        q = h[:, 512:1280]
        k = h[:, 1280:2048]
        val = h[:, 2048:2816]
        f = _dot(xn, wf_ref[...])

        gu = jax.nn.gelu(u)
        vn = _head_rms(jax.nn.gelu(v), bd, gv_ref[...])
        pair_out = []
        for pr in range(2):
            vp = vn[:, pr * 128:(pr + 1) * 128]
            v_lo = jnp.where(lo_half, vp, 0.0).astype(BF16)
            v_hi = jnp.where(lo_half, 0.0, vp).astype(BF16)
            chunks = []
            for c in range(sub // CHUNK):
                rs = slice(c * CHUNK, (c + 1) * CHUNK)
                s = (_dot(wtril_ref[2 * pr], v_lo[rs]) + _dot(wtril_ref[2 * pr + 1], v_hi[rs])
                     + bmat[:, pr * 128:(pr + 1) * 128])
                chunks.append(s)
            pair_out.append(jnp.concatenate(chunks, axis=0))
        s_all = jnp.concatenate(pair_out, axis=1)
        a_ref[rows, :] = (gu * s_all).astype(BF16)

        qn = _head_rms(q, bd, gq_ref[...]) * (HEAD_DIM ** -0.5 * LOG2E)
        for pr in range(FOX_HEADS // 2):
            ls = slice(pr * LANES, (pr + 1) * LANES)
            q_ref[pr, :, rows] = qn[:, ls].T.astype(BF16)
            v_ref[pr, :, rows] = val[:, ls].T.astype(BF16)
        k_ref[rows, :] = _head_rms(k, bd, gk_ref[...]).astype(BF16)

        logf = _log_sigmoid(f + bf_ref[...])
        cum = (_dot_split(logf, ltri_ref[0:sub, 0:sub], terms=3, w_left=True)
               + carry_ref[0:1, :])
        carry_ref[0:1, :] = cum[sub - 1:sub, :]
        cum2 = cum * LOG2E
        fq_ref[rows, :] = cum2
        hi, mid, lo = (p.astype(F32) for p in _split_bf16(cum2, 3))
        f_ref[rows, :] = jnp.where(
            lane < FOX_HEADS, hi,
            jnp.where(lane < 2 * FOX_HEADS, mid,
                      jnp.where(lane < 3 * FOX_HEADS, lo,
                                jnp.where(lane < 3 * FOX_HEADS + 3, 1.0, 0.0)))).astype(BF16)


def _even_in(x2, g, w_all, wf, bf, wtril, bmat, gv, gq, gk, bd, ltri, seq):
    t = x2.shape[0]
    n = t // TM
    full = lambda a: pl.BlockSpec(a.shape, lambda i: (0,) * a.ndim)
    row = lambda wdt: pl.BlockSpec((TM, wdt), lambda i: (i, 0))
    pairs = FOX_HEADS // 2
    colT = pl.BlockSpec((pairs, LANES, TM), lambda i: (0, 0, i))
    w_main = pl.BlockSpec((EVEN_MAIN, D_MODEL), lambda i: (0, 0), pipeline_mode=pl.Buffered(1))
    return pl.pallas_call(
        functools.partial(_even_in_kernel, seq // TM),
        grid=(n,),
        in_specs=[row(D_MODEL), full(g), w_main, full(wf), full(bf), full(wtril), full(bmat),
                  full(gv), full(gq), full(gk), full(bd), full(ltri)],
        out_specs=[row(GMLP_WIDTH), colT, row(FOX_WIDTH), colT, row(LANES), row(LANES)],
        out_shape=[jax.ShapeDtypeStruct((t, GMLP_WIDTH), BF16),
                   jax.ShapeDtypeStruct((pairs, LANES, t), BF16),
                   jax.ShapeDtypeStruct((t, FOX_WIDTH), BF16),
                   jax.ShapeDtypeStruct((pairs, LANES, t), BF16),
                   jax.ShapeDtypeStruct((t, LANES), BF16),
                   jax.ShapeDtypeStruct((t, LANES), F32)],
        scratch_shapes=[pltpu.VMEM((D_MODEL, EVEN_MAIN), BF16), pltpu.VMEM((8, LANES), F32)],
        compiler_params=_cparams(("arbitrary",)),
        name="even_in",
    )(x2, g, w_all, wf, bf, wtril, bmat, gv, gq, gk, bd, ltri)


def _fox_query_operand(qt, pr, hh, shift_terms=None):
    tq = qt.shape[1]
    row = lax.broadcasted_iota(jnp.int32, (LANES, tq), 0)
    head = 2 * pr + hh
    in_head = (row >= hh * HEAD_DIM) & (row < (hh + 1) * HEAD_DIM)
    f_rows = (row == head) | (row == head + FOX_HEADS) | (row == head + 2 * FOX_HEADS)
    extra = jnp.where(f_rows, -1.0, 0.0)
    if shift_terms is not None:
        for n, term in enumerate(shift_terms):
            extra = jnp.where(row == 3 * FOX_HEADS + n, term, extra)
    return jnp.concatenate([jnp.where(in_head, qt, 0.0).astype(BF16), extra.astype(BF16)],
                           axis=0)


def _fox_online_kernel(q_ref, k_ref, f_ref, v_ref, o_ref, m_sc, l_sc, acc_sc):
    pr = pl.program_id(1)
    i = pl.program_id(2)
    tq = q_ref.shape[2]
    qt = q_ref[0].astype(F32)
    rhs = [_fox_query_operand(qt, pr, hh) for hh in range(2)]

    m_sc[...] = jnp.full_like(m_sc, NEG)
    l_sc[...] = jnp.zeros_like(l_sc)
    acc_sc[...] = jnp.zeros_like(acc_sc)

    def step(j, masked):
        start = pl.multiple_of(j * tq, tq)
        kaug = jnp.concatenate([k_ref[pl.ds(start, tq), :], f_ref[pl.ds(start, tq), :]],
                               axis=1)
        for hh in range(2):
            s = _dot(kaug, rhs[hh])
            if masked:
                r_id = lax.broadcasted_iota(jnp.int32, (tq, tq), 0)
                c_id = lax.broadcasted_iota(jnp.int32, (tq, tq), 1)
                s = jnp.where(r_id <= c_id, s, NEG)
            m_prev = m_sc[hh]
            m_new = jnp.maximum(m_prev, jnp.max(s, axis=0, keepdims=True))
            alpha = jnp.exp2(m_prev - m_new)
            p = jnp.exp2(s - m_new)
            l_sc[hh] = alpha * l_sc[hh] + jnp.sum(p, axis=0, keepdims=True)
            vt = v_ref[0, hh * HEAD_DIM:(hh + 1) * HEAD_DIM, pl.ds(start, tq)]
            acc_sc[hh] = alpha * acc_sc[hh] + _dot(vt, p.astype(BF16))
            m_sc[hh] = m_new

    def body(j, c):
        step(j, False)
        return c

    lax.fori_loop(0, i, body, 0)
    step(i, True)
    ot = jnp.concatenate([acc_sc[0] / l_sc[0], acc_sc[1] / l_sc[1]], axis=0)
    o_ref[...] = ot.T.astype(o_ref.dtype)


def _fox_fixed_kernel(b_ref, q_ref, k_ref, f_ref, v_ref, fq_ref, o_ref,
                      rhs_sc, z_sc, acc_sc, mask_sc):
    pr = pl.program_id(1)
    tq = z_sc.shape[3]
    tk = z_sc.shape[2]
    nq = rhs_sc.shape[0]
    row = lax.broadcasted_iota(jnp.int32, (LANES, tq), 0)
    for i in range(nq):
        qt = q_ref[0, :, i * tq:(i + 1) * tq].astype(F32)
        fqt = fq_ref[i * tq:(i + 1) * tq, :].T
        for hh in range(2):
            ft = jnp.sum(jnp.where(row == 2 * pr + hh, fqt, 0.0), axis=0, keepdims=True)
            shift = [p.astype(F32) for p in _split_bf16(ft - b_ref[...], 3)]
            rhs_sc[i, hh] = _fox_query_operand(qt, pr, hh, shift)

    mask_sc[...] = jnp.where(lax.broadcasted_iota(jnp.int32, (tk, tq), 0)
                             > lax.broadcasted_iota(jnp.int32, (tk, tq), 1), NEG, 0.0)
    acc_sc[...] = jnp.zeros_like(acc_sc)
    ones = jnp.ones((acc_sc.shape[2] - HEAD_DIM, tk), BF16)

    half = tk // 2

    def scores(i, j, slot, diagonal=False):
        start = pl.multiple_of(j * tk, tk)
        kaug = jnp.concatenate([k_ref[pl.ds(start, tk), :], f_ref[pl.ds(start, tk), :]],
                               axis=1)
        for hh in range(2):
            if diagonal:
                z_sc[slot, hh, 0:half, :] = _dot(kaug[0:half], rhs_sc[i, hh])
                z_sc[slot, hh, half:tk, half:tq] = _dot(kaug[half:tk], rhs_sc[i, hh, :, half:tq])
            else:
                z_sc[slot, hh] = _dot(kaug, rhs_sc[i, hh])

    def values(hh, start, size):
        return jnp.concatenate(
            [v_ref[0, hh * HEAD_DIM:(hh + 1) * HEAD_DIM, pl.ds(start, size)], ones[:, 0:size]],
            axis=0)

    def consume(i, j, slot):
        start = pl.multiple_of(j * tk, tk)
        for hh in range(2):
            p = jnp.exp2(z_sc[slot, hh]).astype(BF16)
            acc_sc[i, hh] += _dot(values(hh, start, tk), p)

    def consume_diagonal(i, slot):
        start = pl.multiple_of(i * tk, tk)
        outs = []
        for hh in range(2):
            p_top = jnp.exp2(z_sc[slot, hh, 0:half, :] + mask_sc[0:half, :]).astype(BF16)
            p_bot = jnp.exp2(z_sc[slot, hh, half:tk, half:tq]
                             + mask_sc[half:tk, half:tq]).astype(BF16)
            acc = acc_sc[i, hh] + _dot(values(hh, start, half), p_top)
            right = acc[:, half:tq] + _dot(values(hh, start + half, half), p_bot)
            acc = jnp.concatenate([acc[:, 0:half], right], axis=1)
            outs.append(acc[0:HEAD_DIM] / acc[HEAD_DIM:HEAD_DIM + 1])
        o_ref[pl.ds(pl.multiple_of(i * tq, tq), tq), :] = (
            jnp.concatenate(outs, axis=0).T.astype(o_ref.dtype))

    def after_offdiag(i, j):
        wrap = j + 1 == i
        i2 = jnp.where(wrap, i + 1, i)
        j2 = jnp.where(wrap, 0, j + 1)
        done = i2 >= nq
        return jnp.where(done, 0, i2), jnp.where(done, 0, j2)

    scores(1, 0, 0)

    def offdiag_body(_, item):
        for _ in range(ITEMS_PER_ITER // 2):
            i0, j0 = item
            i1, j1 = after_offdiag(i0, j0)
            scores(i1, j1, 1)
            consume(i0, j0, 0)
            item = after_offdiag(i1, j1)
            scores(item[0], item[1], 0)
            consume(i1, j1, 1)
        return item

    lax.fori_loop(0, nq * (nq - 1) // (2 * ITEMS_PER_ITER), offdiag_body,
                  (jnp.int32(1), jnp.int32(0)))

    def diag_body(m, c):
        for u in range(ITEMS_PER_ITER // 2):
            i0 = ITEMS_PER_ITER * m + 2 * u
            scores(i0 + 1, i0 + 1, 1, diagonal=True)
            consume_diagonal(i0, 0)
            i2 = jnp.minimum(i0 + 2, nq - 1)
            scores(i2, i2, 0, diagonal=True)
            consume_diagonal(i0 + 1, 1)
        return c

    lax.fori_loop(0, nq // ITEMS_PER_ITER, diag_body, 0)


def _fox(qt, k, f3, vt, fq, bound, bsz, seq):
    t = k.shape[0]
    pairs = FOX_HEADS // 2
    out_shape = jax.ShapeDtypeStruct((t, FOX_WIDTH), BF16)
    sem = _cparams(("parallel", "parallel", "arbitrary"))

    def common(tq):
        nq = seq // tq
        in_specs = [pl.BlockSpec((1, LANES, tq), lambda b, p, i: (p, 0, b * nq + i)),
                    pl.BlockSpec((seq, LANES), lambda b, p, i: (b, p)),
                    pl.BlockSpec((seq, LANES), lambda b, p, i: (b, 0)),
                    pl.BlockSpec((1, LANES, seq), lambda b, p, i: (p, 0, b))]
        return nq, in_specs, pl.BlockSpec((tq, LANES), lambda b, p, i: (b * nq + i, p))

    def fixed(brow):
        nq = seq // TQ
        assert nq % ITEMS_PER_ITER == 0 and (nq * (nq - 1) // 2) % ITEMS_PER_ITER == 0
        return pl.pallas_call(
            _fox_fixed_kernel,
            grid=(bsz, pairs),
            in_specs=[pl.BlockSpec((1, TQ), lambda b, p: (0, 0)),
                      pl.BlockSpec((1, LANES, seq), lambda b, p: (p, 0, b)),
                      pl.BlockSpec((seq, LANES), lambda b, p: (b, p)),
                      pl.BlockSpec((seq, LANES), lambda b, p: (b, 0)),
                      pl.BlockSpec((1, LANES, seq), lambda b, p: (p, 0, b)),
                      pl.BlockSpec((seq, LANES), lambda b, p: (b, 0))],
            out_specs=pl.BlockSpec((seq, LANES), lambda b, p: (b, p)),
            out_shape=out_shape,
            scratch_shapes=[pltpu.VMEM((nq, 2, 2 * LANES, TQ), BF16),
                            pltpu.VMEM((2, 2, TQ, TQ), F32),
                            pltpu.VMEM((nq, 2, HEAD_DIM + 16, TQ), F32),
                            pltpu.VMEM((TQ, TQ), F32)],
            compiler_params=_cparams(("parallel", "parallel")), name="fox_attention_fixed",
        )(brow, qt, k, f3, vt, fq)

    def online(brow):
        del brow
        nq, in_specs, out_spec = common(TQ_ONLINE)
        stats = pltpu.VMEM((2, 1, TQ_ONLINE), F32)
        return pl.pallas_call(
            _fox_online_kernel,
            grid=(bsz, pairs, nq),
            in_specs=in_specs, out_specs=out_spec, out_shape=out_shape,
            scratch_shapes=[stats, stats, pltpu.VMEM((2, HEAD_DIM, TQ_ONLINE), F32)],
            compiler_params=sem, name="fox_attention_online",
        )(qt, k, f3, vt)

    brow = jnp.full((1, TQ), bound, F32)
    return lax.cond(bound <= FOX_FAST_BOUND_LOG2, fixed, online, brow)


def _even_out_kernel(a_ref, o_ref, x_ref, w_ref, gm_ref, wr_ref, br_ref,
                     x1_ref, xn_ref, comb_ref):
    sub = x_ref.shape[0] // SUBTILES
    for s in range(SUBTILES):
        rs = slice(s * sub, (s + 1) * sub)
        x1 = x_ref[rs, :] + _dot(jnp.concatenate([a_ref[rs, :], o_ref[rs, :]], axis=1),
                                 w_ref[...])
        x1_ref[rs, :] = x1
        xn = _rms_rows(x1, gm_ref[...]).astype(BF16)
        xn_ref[rs, :] = xn
        comb_ref[rs, :] = _router(xn, wr_ref, br_ref)


def _even_out(a, o, x2, w, gm, wr, br):
    t = x2.shape[0]
    n = t // TM
    full = lambda arr: pl.BlockSpec(arr.shape, lambda i: (0,) * arr.ndim)
    row = lambda wdt: pl.BlockSpec((TM, wdt), lambda i: (i, 0))
    return pl.pallas_call(
        _even_out_kernel,
        grid=(n,),
        in_specs=[row(GMLP_WIDTH), row(FOX_WIDTH), row(D_MODEL), full(w), full(gm),
                  full(wr), full(br)],
        out_specs=[row(D_MODEL), row(D_MODEL), row(LANES)],
        out_shape=[jax.ShapeDtypeStruct((t, D_MODEL), F32),
                   jax.ShapeDtypeStruct((t, D_MODEL), BF16),
                   jax.ShapeDtypeStruct((t, LANES), F32)],
        compiler_params=_cparams(("parallel",)),
        name="even_out",
    )(a, o, x2, w, gm, wr, br)


def _odd_kernel(tiles_per_seq, x_ref, g_ref, w_ref, cw_ref, wp_ref, ps_ref, wo_ref, gm_ref,
                wr_ref, br_ref, x1_ref, xn_ref, comb_ref, zbuf, pbuf):
    i = pl.program_id(0)
    tm = x_ref.shape[0]
    zpad = zbuf.shape[0] - tm
    ppad = pbuf.shape[0] - tm
    @pl.when(i % tiles_per_seq == 0)
    def _():
        zbuf[0:zpad, :] = jnp.zeros((zpad, CONV_WIDTH), F32)
        pbuf[0:ppad, :] = jnp.zeros((ppad, POOL_WIDTH), F32)

    cw = cw_ref[...]
    ps = ps_ref[...]
    sub = tm // SUBTILES
    for s in range(SUBTILES):
        r0 = s * sub
        rs = slice(r0, r0 + sub)
        x = x_ref[rs, :]
        xn = _rms_rows(x, g_ref[...]).astype(BF16)
        h = _dot(xn, w_ref[...])
        bg = h[:, 0:512]
        cg = h[:, 512:1024]
        hc = h[:, 1024:1536]
        p = h[:, 1536:2048]

        z = cg * hc
        zbuf[zpad + r0:zpad + r0 + sub, :] = z
        y = (cw[0:1, :] * zbuf[zpad + r0 - 2:zpad + r0 - 2 + sub, :]
             + cw[1:2, :] * zbuf[zpad + r0 - 1:zpad + r0 - 1 + sub, :]
             + cw[2:3, :] * z)
        c = bg * y

        pbuf[ppad + r0:ppad + r0 + sub, :] = p
        pos = ((i % tiles_per_seq) * tm + r0
               + lax.broadcasted_iota(jnp.int32, (sub, POOL_GROUP), 0)).astype(F32) + 1.0
        pooled_out = []
        for gi, win in enumerate(POOL_WINDOWS):
            ls = slice(gi * POOL_GROUP, (gi + 1) * POOL_GROUP)
            pg = p[:, ls]
            sm = pg
            for sh in range(1, win):
                sm = sm + pbuf[ppad + r0 - sh:ppad + r0 - sh + sub, ls]
            cnt = jnp.minimum(pos, float(win))
            pooled = sm / cnt - pg
            pooled_out.append((_dot(pooled.astype(BF16), wp_ref[gi]) * ps[:, ls]).astype(BF16))
        mix = jnp.concatenate([c.astype(BF16)] + pooled_out, axis=1)
        acc = x + _dot(mix, wo_ref[...])

        x1_ref[rs, :] = acc
        xn2 = _rms_rows(acc, gm_ref[...]).astype(BF16)
        xn_ref[rs, :] = xn2
        comb_ref[rs, :] = _router(xn2, wr_ref, br_ref)

    zbuf[0:zpad, :] = zbuf[tm:tm + zpad, :]
    pbuf[0:ppad, :] = pbuf[tm:tm + ppad, :]


def _odd(x2, g, w, cw, wp, ps, wo, gm, wr, br, seq):
    t = x2.shape[0]
    n = t // TM
    full = lambda arr: pl.BlockSpec(arr.shape, lambda i: (0,) * arr.ndim)
    row = lambda wdt: pl.BlockSpec((TM, wdt), lambda i: (i, 0))
    return pl.pallas_call(
        functools.partial(_odd_kernel, seq // TM),
        grid=(n,),
        in_specs=[row(D_MODEL), full(g), full(w), full(cw), full(wp), full(ps), full(wo),
                  full(gm), full(wr), full(br)],
        out_specs=[row(D_MODEL), row(D_MODEL), row(LANES)],
        out_shape=[jax.ShapeDtypeStruct((t, D_MODEL), F32),
                   jax.ShapeDtypeStruct((t, D_MODEL), BF16),
                   jax.ShapeDtypeStruct((t, LANES), F32)],
        scratch_shapes=[pltpu.VMEM((TM + 8, CONV_WIDTH), F32),
                        pltpu.VMEM((TM + 16, POOL_WIDTH), F32)],
        compiler_params=_cparams(("arbitrary",)),
        name="odd_mixer",
    )(x2, g, w, cw, wp, ps, wo, gm, wr, br)


def _moe_kernel(xn_ref, wg_ref, wu_ref, wd_ref, comb_ref, x1_ref, o_ref):
    g = pl.program_id(1)

    @pl.when(g == 0)
    def _():
        o_ref[...] = x1_ref[...]

    tm = xn_ref.shape[0]
    xn = xn_ref[...]
    lane = lax.broadcasted_iota(jnp.int32, (tm, LANES), 1)
    comb = comb_ref[...]
    hid = []
    for e in range(EXPERTS_PER_GROUP):
        gate = _dot(xn, wg_ref[e].astype(BF16))
        up = _dot(xn, wu_ref[e].astype(BF16))
        cw = jnp.sum(jnp.where(lane == g * EXPERTS_PER_GROUP + e, comb, 0.0), axis=1,
                     keepdims=True)
        hid.append(((gate * jax.nn.sigmoid(gate)) * up * cw).astype(BF16))
    o_ref[...] += _dot(jnp.concatenate(hid, axis=1), wd_ref[0].astype(BF16))


def _moe(xn, wg, wu, wd, comb, x1, layer):
    t = xn.shape[0]
    n = t // TM_MOE
    gh = EXPERTS_PER_GROUP * EXPERT_HIDDEN
    wblk = pl.BlockSpec((None, EXPERTS_PER_GROUP, D_MODEL, EXPERT_HIDDEN),
                        lambda i, g: (layer, g, 0, 0))
    return pl.pallas_call(
        _moe_kernel,
        grid=(n, N_GROUPS),
        in_specs=[pl.BlockSpec((TM_MOE, D_MODEL), lambda i, g: (i, 0)),
                  wblk, wblk,
                  pl.BlockSpec((None, 1, gh, D_MODEL), lambda i, g: (layer, g, 0, 0)),
                  pl.BlockSpec((TM_MOE, LANES), lambda i, g: (i, 0)),
                  pl.BlockSpec((TM_MOE, D_MODEL), lambda i, g: (i, 0))],
        out_specs=pl.BlockSpec((TM_MOE, D_MODEL), lambda i, g: (i, 0)),
        out_shape=jax.ShapeDtypeStruct((t, D_MODEL), F32),
        compiler_params=_cparams(("parallel", "arbitrary")),
        name="moe",
    )(xn, wg, wu, wd, comb, x1)


def _router_params(w_group, b_group, w_router, b_router):
    wr = jnp.concatenate([w_router.reshape(D_MODEL, N_EXPERTS), w_group], axis=1)
    wr = jnp.pad(wr, ((0, 0), (0, LANES - wr.shape[1]))).astype(BF16)
    br = jnp.concatenate([b_router.reshape(N_EXPERTS), b_group])
    br = jnp.pad(br, (0, LANES - br.shape[0])).reshape(1, LANES).astype(F32)
    return wr, br


def _moe_params(w_gate, w_up, w_down):
    wd = w_down.reshape(w_down.shape[0], N_GROUPS, EXPERTS_PER_GROUP * EXPERT_HIDDEN, D_MODEL)
    return w_gate, w_up, wd


def kernel(x, ev_norm, ev_w_in, ev_b_forget, ev_w_s, ev_b_s, ev_g_v, ev_g_q, ev_g_k, ev_w_out,
           od_norm, od_w_in, od_conv_w, od_w_pool, od_pool_scale, od_w_out,
           moe_norm, moe_w_group, moe_b_group, moe_w_router, moe_b_router,
           moe_w_gate, moe_w_up, moe_w_down):
    bsz, seq, d = x.shape
    t = bsz * seq
    x2 = x.reshape(t, d)

    w_f = ev_w_in[0][:, EVEN_MAIN:]
    w_f3 = jnp.pad(jnp.tile(w_f, (1, 3)), ((0, 0), (0, LANES - 3 * FOX_HEADS))).astype(BF16)
    bf = jnp.pad(jnp.tile(ev_b_forget[0], 3), (0, LANES - 3 * FOX_HEADS)).reshape(1, LANES)
    tril = jnp.tril(jnp.ones((CHUNK, CHUNK), F32))
    wtril = (ev_w_s[0] * tril).astype(BF16)
    bmat = jnp.repeat(ev_b_s[0].T, GMLP_HEAD, axis=1)
    gv = ev_g_v[0].reshape(1, GMLP_WIDTH)
    gq = jnp.tile(ev_g_q[0], FOX_HEADS).reshape(1, FOX_WIDTH)
    gk = jnp.tile(ev_g_k[0], FOX_HEADS).reshape(1, FOX_WIDTH)
    blk = jnp.arange(256) // HEAD_DIM
    bd = jnp.where(blk[:, None] == blk[None, :], 1.0 / HEAD_DIM, 0.0).astype(BF16)
    ltri = jnp.tril(jnp.ones((TM, TM), F32)).astype(BF16)

    a, qt, kn, vt, f3, fq = _even_in(x2, ev_norm[0].reshape(1, d), ev_w_in[0].T, w_f3, bf, wtril,
                                     bmat, gv, gq, gk, bd, ltri, seq)
    bound = (HEAD_DIM ** 0.5 * LOG2E * 1.01) * jnp.max(jnp.abs(ev_g_q[0])) * jnp.max(
        jnp.abs(ev_g_k[0]))
    o = _fox(qt, kn, f3, vt, fq, bound, bsz, seq)

    wr0, br0 = _router_params(moe_w_group[0], moe_b_group[0], moe_w_router[0], moe_b_router[0])
    w_out0 = ev_w_out[0].astype(BF16)
    x1, xn1, comb1 = _even_out(a, o, x2, w_out0,
                               moe_norm[0].reshape(1, d), wr0, br0)
    moe_w = _moe_params(moe_w_gate, moe_w_up, moe_w_down)
    xa = _moe(xn1, *moe_w, comb1, x1, 0)

    wr1, br1 = _router_params(moe_w_group[1], moe_b_group[1], moe_w_router[1], moe_b_router[1])
    x3, xn3, comb3 = _odd(xa, od_norm[0].reshape(1, d), od_w_in[0].astype(BF16), od_conv_w[0],
                          od_w_pool[0].astype(BF16), od_pool_scale[0].reshape(1, POOL_WIDTH),
                          od_w_out[0].astype(BF16), moe_norm[1].reshape(1, d), wr1, br1, seq)
    xb = _moe(xn3, *moe_w, comb3, x3, 1)
    return xb.reshape(bsz, seq, d)
```

```python
import functools

import jax
import jax.numpy as jnp
from jax import lax
from jax.experimental import pallas as pl
from jax.experimental.pallas import tpu as pltpu

F32 = jnp.float32
BF16 = jnp.bfloat16

D_MODEL = 1024
EPS = 1e-6
CHUNK = 128
GMLP_GROUPS = 4
GMLP_HEAD = 64
GMLP_WIDTH = 256
FOX_HEADS = 12
HEAD_DIM = 64
FOX_WIDTH = 768
CONV_WIDTH = 512
CONV_K = 3
POOL_WINDOWS = (2, 4, 8, 16)
POOL_GROUP = 128
POOL_WIDTH = 512
N_GROUPS = 4
EXPERTS_PER_GROUP = 4
N_EXPERTS = 16
EXPERT_HIDDEN = 256

LANES = 128
V7X_VMEM_LIMIT_BYTES = 56 * 1024 * 1024

TM = 512
TQ = 512
TQ_ONLINE = 256
OFFDIAG_ITEMS_PER_ITER = 14
DIAG_ITEMS_PER_ITER = 8
TM_MOE = 1024
SUBTILES = 2
EVEN_MAIN = 2 * GMLP_WIDTH + 3 * FOX_WIDTH
NEG = -1e30
LOG2E = 1.4426950408889634
FOX_FAST_BOUND_LOG2 = 50.0


def _cparams(sem):
    return pltpu.CompilerParams(dimension_semantics=sem,
                                vmem_limit_bytes=V7X_VMEM_LIMIT_BYTES)


def _dot(a, b):
    return jnp.dot(a, b, preferred_element_type=F32)


def _split_bf16(x, terms):
    parts = []
    r = x
    for _ in range(terms):
        p = r.astype(BF16)
        parts.append(p)
        r = r - p.astype(F32)
    return parts


def _dot_split(x, w_bf16, terms=2, w_left=False):
    acc = None
    for p in _split_bf16(x, terms):
        d = _dot(w_bf16, p) if w_left else _dot(p, w_bf16)
        acc = d if acc is None else acc + d
    return acc


def _rms_rows(x, g):
    ms = jnp.mean(x * x, axis=-1, keepdims=True)
    return x * lax.rsqrt(ms + EPS) * g


def _head_rms(x, bd, g):
    outs = []
    for c in range(x.shape[1] // 256):
        xc = x[:, c * 256:(c + 1) * 256]
        ms = _dot_split(xc * xc, bd)
        outs.append(xc * lax.rsqrt(ms + EPS))
    y = outs[0] if len(outs) == 1 else jnp.concatenate(outs, axis=1)
    return y * g


def _log_sigmoid(x):
    return -(jnp.maximum(-x, 0.0) + jnp.log(1.0 + jnp.exp(-jnp.abs(x))))


def _router(xn_bf16, wr_ref, br_ref):
    r = _dot(xn_bf16, wr_ref[...]) + br_ref[...]
    tm = r.shape[0]
    rt = r.T
    er = rt[0:N_EXPERTS]
    gr = rt[N_EXPERTS:N_EXPERTS + 8]
    grow = lax.broadcasted_iota(jnp.int32, (8, tm), 0).astype(F32)
    erow = lax.broadcasted_iota(jnp.int32, (N_EXPERTS, tm), 0).astype(F32)
    is_g = grow < float(N_GROUPS)
    gl = jnp.where(is_g, gr, NEG)
    gmax = jnp.max(gl, axis=0, keepdims=True)
    gidx = jnp.min(jnp.where(gl == gmax, grow, 999.0), axis=0, keepdims=True)
    gsum = jnp.sum(jnp.where(is_g, jnp.exp(gl - gmax), 0.0), axis=0, keepdims=True)
    gp = 1.0 / gsum
    lo = gidx * float(EXPERTS_PER_GROUP)
    sel = (erow >= lo) & (erow < lo + float(EXPERTS_PER_GROUP))
    el = jnp.where(sel, er, NEG)
    emax = jnp.max(el, axis=0, keepdims=True)
    ee = jnp.where(sel, jnp.exp(el - emax), 0.0)
    ep = ee / jnp.sum(ee, axis=0, keepdims=True)
    ep = jnp.where(sel, ep, -1.0)
    p1 = jnp.max(ep, axis=0, keepdims=True)
    i1 = jnp.min(jnp.where(ep == p1, erow, 999.0), axis=0, keepdims=True)
    ep2 = jnp.where(erow == i1, -1.0, ep)
    p2 = jnp.max(ep2, axis=0, keepdims=True)
    i2 = jnp.min(jnp.where(ep2 == p2, erow, 999.0), axis=0, keepdims=True)
    den = p1 + p2
    comb_t = (jnp.where(erow == i1, gp * (p1 / den), 0.0)
              + jnp.where(erow == i2, gp * (p2 / den), 0.0))
    comb_t = jnp.concatenate([comb_t, jnp.zeros((LANES - N_EXPERTS, tm), F32)], axis=0)
    return comb_t.T


def _even_in_kernel(tiles_per_seq, x_ref, g_ref, w_ref, wf_ref, bf_ref, wtril_ref, bmat_ref,
                    gv_ref, gq_ref, gk_ref, bd_ref, ltri_ref,
                    a_ref, q_ref, k_ref, v_ref, f_ref, fq_ref, wbf_ref, carry_ref):
    i = pl.program_id(0)
    tm = x_ref.shape[0]

    @pl.when(i == 0)
    def _():
        for c in range(0, w_ref.shape[0], 256):
            wbf_ref[:, c:c + 256] = w_ref[c:c + 256, :].T.astype(BF16)

    @pl.when(i % tiles_per_seq == 0)
    def _():
        carry_ref[...] = jnp.zeros_like(carry_ref)

    sub = tm // SUBTILES
    lane = lax.broadcasted_iota(jnp.int32, (sub, LANES), 1)
    lo_half = lane < GMLP_HEAD
    bmat = bmat_ref[...]
    bd = bd_ref[...]
    for st in range(SUBTILES):
        rows = slice(st * sub, (st + 1) * sub)
        xn = _rms_rows(x_ref[rows, :], g_ref[...]).astype(BF16)
        h = _dot(xn, wbf_ref[...])
        u = h[:, 0:256]
        v = h[:, 256:512]
        q = h[:, 512:1280]
        k = h[:, 1280:2048]
        val = h[:, 2048:2816]
        f = _dot(xn, wf_ref[...])

        gu = jax.nn.gelu(u)
        vn = _head_rms(jax.nn.gelu(v), bd, gv_ref[...])
        pair_out = []
        for pr in range(2):
            vp = vn[:, pr * 128:(pr + 1) * 128]
            v_lo = jnp.where(lo_half, vp, 0.0).astype(BF16)
            v_hi = jnp.where(lo_half, 0.0, vp).astype(BF16)
            chunks = []
            for c in range(sub // CHUNK):
                rs = slice(c * CHUNK, (c + 1) * CHUNK)
                s = (_dot(wtril_ref[2 * pr], v_lo[rs]) + _dot(wtril_ref[2 * pr + 1], v_hi[rs])
                     + bmat[:, pr * 128:(pr + 1) * 128])
                chunks.append(s)
            pair_out.append(jnp.concatenate(chunks, axis=0))
        s_all = jnp.concatenate(pair_out, axis=1)
        a_ref[rows, :] = (gu * s_all).astype(BF16)

        qn = _head_rms(q, bd, gq_ref[...]) * (HEAD_DIM ** -0.5 * LOG2E)
        for pr in range(FOX_HEADS // 2):
            ls = slice(pr * LANES, (pr + 1) * LANES)
            q_ref[pr, :, rows] = qn[:, ls].T.astype(BF16)
            v_ref[pr, :, rows] = val[:, ls].T.astype(BF16)
        k_ref[rows, :] = _head_rms(k, bd, gk_ref[...]).astype(BF16)

        logf = _log_sigmoid(f + bf_ref[...])
        cum = (_dot_split(logf, ltri_ref[0:sub, 0:sub], terms=3, w_left=True)
               + carry_ref[0:1, :])
        carry_ref[0:1, :] = cum[sub - 1:sub, :]
        cum2 = cum * LOG2E
        fq_ref[rows, :] = cum2
        hi, mid, lo = (p.astype(F32) for p in _split_bf16(cum2, 3))
        f_ref[rows, :] = jnp.where(
            lane < FOX_HEADS, hi,
            jnp.where(lane < 2 * FOX_HEADS, mid,
                      jnp.where(lane < 3 * FOX_HEADS, lo,
                                jnp.where(lane < 3 * FOX_HEADS + 3, 1.0, 0.0)))).astype(BF16)


def _even_in(x2, g, w_all, wf, bf, wtril, bmat, gv, gq, gk, bd, ltri, seq):
    t = x2.shape[0]
    n = t // TM
    full = lambda a: pl.BlockSpec(a.shape, lambda i: (0,) * a.ndim)
    row = lambda wdt: pl.BlockSpec((TM, wdt), lambda i: (i, 0))
    pairs = FOX_HEADS // 2
    colT = pl.BlockSpec((pairs, LANES, TM), lambda i: (0, 0, i))
    w_main = pl.BlockSpec((EVEN_MAIN, D_MODEL), lambda i: (0, 0), pipeline_mode=pl.Buffered(1))
    return pl.pallas_call(
        functools.partial(_even_in_kernel, seq // TM),
        grid=(n,),
        in_specs=[row(D_MODEL), full(g), w_main, full(wf), full(bf), full(wtril), full(bmat),
                  full(gv), full(gq), full(gk), full(bd), full(ltri)],
        out_specs=[row(GMLP_WIDTH), colT, row(FOX_WIDTH), colT, row(LANES), row(LANES)],
        out_shape=[jax.ShapeDtypeStruct((t, GMLP_WIDTH), BF16),
                   jax.ShapeDtypeStruct((pairs, LANES, t), BF16),
                   jax.ShapeDtypeStruct((t, FOX_WIDTH), BF16),
                   jax.ShapeDtypeStruct((pairs, LANES, t), BF16),
                   jax.ShapeDtypeStruct((t, LANES), BF16),
                   jax.ShapeDtypeStruct((t, LANES), F32)],
        scratch_shapes=[pltpu.VMEM((D_MODEL, EVEN_MAIN), BF16), pltpu.VMEM((8, LANES), F32)],
        compiler_params=_cparams(("arbitrary",)),
        name="even_in",
    )(x2, g, w_all, wf, bf, wtril, bmat, gv, gq, gk, bd, ltri)


def _fox_query_operand(qt, pr, hh, shift_terms=None):
    tq = qt.shape[1]
    row = lax.broadcasted_iota(jnp.int32, (LANES, tq), 0)
    head = 2 * pr + hh
    in_head = (row >= hh * HEAD_DIM) & (row < (hh + 1) * HEAD_DIM)
    f_rows = (row == head) | (row == head + FOX_HEADS) | (row == head + 2 * FOX_HEADS)
    extra = jnp.where(f_rows, -1.0, 0.0)
    if shift_terms is not None:
        for n, term in enumerate(shift_terms):
            extra = jnp.where(row == 3 * FOX_HEADS + n, term, extra)
    return jnp.concatenate([jnp.where(in_head, qt, 0.0).astype(BF16), extra.astype(BF16)],
                           axis=0)


def _fox_online_kernel(q_ref, k_ref, f_ref, v_ref, o_ref, m_sc, l_sc, acc_sc):
    pr = pl.program_id(1)
    i = pl.program_id(2)
    tq = q_ref.shape[2]
    qt = q_ref[0].astype(F32)
    rhs = [_fox_query_operand(qt, pr, hh) for hh in range(2)]

    m_sc[...] = jnp.full_like(m_sc, NEG)
    l_sc[...] = jnp.zeros_like(l_sc)
    acc_sc[...] = jnp.zeros_like(acc_sc)

    def step(j, masked):
        start = pl.multiple_of(j * tq, tq)
        kaug = jnp.concatenate([k_ref[pl.ds(start, tq), :], f_ref[pl.ds(start, tq), :]],
                               axis=1)
        for hh in range(2):
            s = _dot(kaug, rhs[hh])
            if masked:
                r_id = lax.broadcasted_iota(jnp.int32, (tq, tq), 0)
                c_id = lax.broadcasted_iota(jnp.int32, (tq, tq), 1)
                s = jnp.where(r_id <= c_id, s, NEG)
            m_prev = m_sc[hh]
            m_new = jnp.maximum(m_prev, jnp.max(s, axis=0, keepdims=True))
            alpha = jnp.exp2(m_prev - m_new)
            p = jnp.exp2(s - m_new)
            l_sc[hh] = alpha * l_sc[hh] + jnp.sum(p, axis=0, keepdims=True)
            vt = v_ref[0, hh * HEAD_DIM:(hh + 1) * HEAD_DIM, pl.ds(start, tq)]
            acc_sc[hh] = alpha * acc_sc[hh] + _dot(vt, p.astype(BF16))
            m_sc[hh] = m_new

    def body(j, c):
        step(j, False)
        return c

    lax.fori_loop(0, i, body, 0)
    step(i, True)
    ot = jnp.concatenate([acc_sc[0] / l_sc[0], acc_sc[1] / l_sc[1]], axis=0)
    o_ref[...] = ot.T.astype(o_ref.dtype)


def _fox_fixed_kernel(b_ref, q_ref, k_ref, f_ref, v_ref, fq_ref, o_ref,
                      rhs_sc, z_sc, acc_sc, mask_sc):
    pr = pl.program_id(1)
    tq = z_sc.shape[3]
    tk = z_sc.shape[2]
    nq = rhs_sc.shape[0]
    row = lax.broadcasted_iota(jnp.int32, (LANES, tq), 0)
    for i in range(nq):
        qt = q_ref[0, :, i * tq:(i + 1) * tq].astype(F32)
        fqt = fq_ref[i * tq:(i + 1) * tq, :].T
        for hh in range(2):
            ft = jnp.sum(jnp.where(row == 2 * pr + hh, fqt, 0.0), axis=0, keepdims=True)
            shift = [p.astype(F32) for p in _split_bf16(ft - b_ref[...], 3)]
            rhs_sc[i, hh] = _fox_query_operand(qt, pr, hh, shift)

    mask_sc[...] = jnp.where(lax.broadcasted_iota(jnp.int32, (tk, tq), 0)
                             > lax.broadcasted_iota(jnp.int32, (tk, tq), 1), NEG, 0.0)
    acc_sc[...] = jnp.zeros_like(acc_sc)
    ones = jnp.ones((acc_sc.shape[2] - HEAD_DIM, tk), BF16)

    half = tk // 2

    def scores(i, j, slot, diagonal=False):
        start = pl.multiple_of(j * tk, tk)
        kaug = jnp.concatenate([k_ref[pl.ds(start, tk), :], f_ref[pl.ds(start, tk), :]],
                               axis=1)
        for hh in range(2):
            if diagonal:
                z_sc[slot, hh, 0:half, :] = _dot(kaug[0:half], rhs_sc[i, hh])
                z_sc[slot, hh, half:tk, half:tq] = _dot(kaug[half:tk], rhs_sc[i, hh, :, half:tq])
            else:
                z_sc[slot, hh] = _dot(kaug, rhs_sc[i, hh])

    def values(hh, start, size):
        return jnp.concatenate(
            [v_ref[0, hh * HEAD_DIM:(hh + 1) * HEAD_DIM, pl.ds(start, size)], ones[:, 0:size]],
            axis=0)

    def consume(i, j, slot):
        start = pl.multiple_of(j * tk, tk)
        for hh in range(2):
            p = jnp.exp2(z_sc[slot, hh]).astype(BF16)
            acc_sc[i, hh] += _dot(values(hh, start, tk), p)

    def consume_diagonal(i, slot):
        start = pl.multiple_of(i * tk, tk)
        outs = []
        for hh in range(2):
            p_top = jnp.exp2(z_sc[slot, hh, 0:half, :] + mask_sc[0:half, :]).astype(BF16)
            p_bot = jnp.exp2(z_sc[slot, hh, half:tk, half:tq]
                             + mask_sc[half:tk, half:tq]).astype(BF16)
            acc = acc_sc[i, hh] + _dot(values(hh, start, half), p_top)
            right = acc[:, half:tq] + _dot(values(hh, start + half, half), p_bot)
            acc = jnp.concatenate([acc[:, 0:half], right], axis=1)
            outs.append(acc[0:HEAD_DIM] / acc[HEAD_DIM:HEAD_DIM + 1])
        o_ref[pl.ds(pl.multiple_of(i * tq, tq), tq), :] = (
            jnp.concatenate(outs, axis=0).T.astype(o_ref.dtype))

    def after_offdiag(i, j):
        wrap = j + 1 == i
        i2 = jnp.where(wrap, i + 1, i)
        j2 = jnp.where(wrap, 0, j + 1)
        done = i2 >= nq
        return jnp.where(done, 0, i2), jnp.where(done, 0, j2)

    scores(1, 0, 0)

    def offdiag_body(_, item):
        for _ in range(OFFDIAG_ITEMS_PER_ITER // 2):
            i0, j0 = item
            i1, j1 = after_offdiag(i0, j0)
            scores(i1, j1, 1)
            consume(i0, j0, 0)
            item = after_offdiag(i1, j1)
            scores(item[0], item[1], 0)
            consume(i1, j1, 1)
        return item

    lax.fori_loop(0, nq * (nq - 1) // (2 * OFFDIAG_ITEMS_PER_ITER), offdiag_body,
                  (jnp.int32(1), jnp.int32(0)))

    def diag_body(m, c):
        for u in range(DIAG_ITEMS_PER_ITER // 2):
            i0 = DIAG_ITEMS_PER_ITER * m + 2 * u
            scores(i0 + 1, i0 + 1, 1, diagonal=True)
            consume_diagonal(i0, 0)
            i2 = jnp.minimum(i0 + 2, nq - 1)
            scores(i2, i2, 0, diagonal=True)
            consume_diagonal(i0 + 1, 1)
        return c

    lax.fori_loop(0, nq // DIAG_ITEMS_PER_ITER, diag_body, 0)


def _fox(qt, k, f3, vt, fq, bound, bsz, seq):
    t = k.shape[0]
    pairs = FOX_HEADS // 2
    out_shape = jax.ShapeDtypeStruct((t, FOX_WIDTH), BF16)
    sem = _cparams(("parallel", "parallel", "arbitrary"))

    def common(tq):
        nq = seq // tq
        in_specs = [pl.BlockSpec((1, LANES, tq), lambda b, p, i: (p, 0, b * nq + i)),
                    pl.BlockSpec((seq, LANES), lambda b, p, i: (b, p)),
                    pl.BlockSpec((seq, LANES), lambda b, p, i: (b, 0)),
                    pl.BlockSpec((1, LANES, seq), lambda b, p, i: (p, 0, b))]
        return nq, in_specs, pl.BlockSpec((tq, LANES), lambda b, p, i: (b * nq + i, p))

    def fixed(brow):
        nq = seq // TQ
        assert nq % DIAG_ITEMS_PER_ITER == 0
        assert (nq * (nq - 1) // 2) % OFFDIAG_ITEMS_PER_ITER == 0
        return pl.pallas_call(
            _fox_fixed_kernel,
            grid=(bsz, pairs),
            in_specs=[pl.BlockSpec((1, TQ), lambda b, p: (0, 0)),
                      pl.BlockSpec((1, LANES, seq), lambda b, p: (p, 0, b)),
                      pl.BlockSpec((seq, LANES), lambda b, p: (b, p)),
                      pl.BlockSpec((seq, LANES), lambda b, p: (b, 0)),
                      pl.BlockSpec((1, LANES, seq), lambda b, p: (p, 0, b)),
                      pl.BlockSpec((seq, LANES), lambda b, p: (b, 0))],
            out_specs=pl.BlockSpec((seq, LANES), lambda b, p: (b, p)),
            out_shape=out_shape,
            scratch_shapes=[pltpu.VMEM((nq, 2, 2 * LANES, TQ), BF16),
                            pltpu.VMEM((2, 2, TQ, TQ), F32),
                            pltpu.VMEM((nq, 2, HEAD_DIM + 16, TQ), F32),
                            pltpu.VMEM((TQ, TQ), F32)],
            compiler_params=_cparams(("parallel", "parallel")), name="fox_attention_fixed",
        )(brow, qt, k, f3, vt, fq)

    def online(brow):
        del brow
        nq, in_specs, out_spec = common(TQ_ONLINE)
        stats = pltpu.VMEM((2, 1, TQ_ONLINE), F32)
        return pl.pallas_call(
            _fox_online_kernel,
            grid=(bsz, pairs, nq),
            in_specs=in_specs, out_specs=out_spec, out_shape=out_shape,
            scratch_shapes=[stats, stats, pltpu.VMEM((2, HEAD_DIM, TQ_ONLINE), F32)],
            compiler_params=sem, name="fox_attention_online",
        )(qt, k, f3, vt)

    brow = jnp.full((1, TQ), bound, F32)
    return lax.cond(bound <= FOX_FAST_BOUND_LOG2, fixed, online, brow)


def _even_out_kernel(a_ref, o_ref, x_ref, w_ref, gm_ref, wr_ref, br_ref,
                     x1_ref, xn_ref, comb_ref):
    sub = x_ref.shape[0] // SUBTILES
    for s in range(SUBTILES):
        rs = slice(s * sub, (s + 1) * sub)
        x1 = x_ref[rs, :] + _dot(jnp.concatenate([a_ref[rs, :], o_ref[rs, :]], axis=1),
                                 w_ref[...])
        x1_ref[rs, :] = x1
        xn = _rms_rows(x1, gm_ref[...]).astype(BF16)
        xn_ref[rs, :] = xn
        comb_ref[rs, :] = _router(xn, wr_ref, br_ref)


def _even_out(a, o, x2, w, gm, wr, br):
    t = x2.shape[0]
    n = t // TM
    full = lambda arr: pl.BlockSpec(arr.shape, lambda i: (0,) * arr.ndim)
    row = lambda wdt: pl.BlockSpec((TM, wdt), lambda i: (i, 0))
    return pl.pallas_call(
        _even_out_kernel,
        grid=(n,),
        in_specs=[row(GMLP_WIDTH), row(FOX_WIDTH), row(D_MODEL), full(w), full(gm),
                  full(wr), full(br)],
        out_specs=[row(D_MODEL), row(D_MODEL), row(LANES)],
        out_shape=[jax.ShapeDtypeStruct((t, D_MODEL), F32),
                   jax.ShapeDtypeStruct((t, D_MODEL), BF16),
                   jax.ShapeDtypeStruct((t, LANES), F32)],
        compiler_params=_cparams(("parallel",)),
        name="even_out",
    )(a, o, x2, w, gm, wr, br)


def _odd_kernel(tiles_per_seq, x_ref, g_ref, w_ref, cw_ref, wp_ref, ps_ref, wo_ref, gm_ref,
                wr_ref, br_ref, x1_ref, xn_ref, comb_ref, zbuf, pbuf):
    i = pl.program_id(0)
    tm = x_ref.shape[0]
    zpad = zbuf.shape[0] - tm
    ppad = pbuf.shape[0] - tm
    @pl.when(i % tiles_per_seq == 0)
    def _():
        zbuf[0:zpad, :] = jnp.zeros((zpad, CONV_WIDTH), F32)
        pbuf[0:ppad, :] = jnp.zeros((ppad, POOL_WIDTH), F32)

    cw = cw_ref[...]
    ps = ps_ref[...]
    sub = tm // SUBTILES
    for s in range(SUBTILES):
        r0 = s * sub
        rs = slice(r0, r0 + sub)
        x = x_ref[rs, :]
        xn = _rms_rows(x, g_ref[...]).astype(BF16)
        h = _dot(xn, w_ref[...])
        bg = h[:, 0:512]
        cg = h[:, 512:1024]
        hc = h[:, 1024:1536]
        p = h[:, 1536:2048]

        z = cg * hc
        zbuf[zpad + r0:zpad + r0 + sub, :] = z
        y = (cw[0:1, :] * zbuf[zpad + r0 - 2:zpad + r0 - 2 + sub, :]
             + cw[1:2, :] * zbuf[zpad + r0 - 1:zpad + r0 - 1 + sub, :]
             + cw[2:3, :] * z)
        c = bg * y

        pbuf[ppad + r0:ppad + r0 + sub, :] = p
        pos = ((i % tiles_per_seq) * tm + r0
               + lax.broadcasted_iota(jnp.int32, (sub, POOL_GROUP), 0)).astype(F32) + 1.0
        pooled_out = []
        for gi, win in enumerate(POOL_WINDOWS):
            ls = slice(gi * POOL_GROUP, (gi + 1) * POOL_GROUP)
            pg = p[:, ls]
            sm = pg
            for sh in range(1, win):
                sm = sm + pbuf[ppad + r0 - sh:ppad + r0 - sh + sub, ls]
            cnt = jnp.minimum(pos, float(win))
            pooled = sm / cnt - pg
            pooled_out.append((_dot(pooled.astype(BF16), wp_ref[gi]) * ps[:, ls]).astype(BF16))
        mix = jnp.concatenate([c.astype(BF16)] + pooled_out, axis=1)
        acc = x + _dot(mix, wo_ref[...])

        x1_ref[rs, :] = acc
        xn2 = _rms_rows(acc, gm_ref[...]).astype(BF16)
        xn_ref[rs, :] = xn2
        comb_ref[rs, :] = _router(xn2, wr_ref, br_ref)

    zbuf[0:zpad, :] = zbuf[tm:tm + zpad, :]
    pbuf[0:ppad, :] = pbuf[tm:tm + ppad, :]


def _odd(x2, g, w, cw, wp, ps, wo, gm, wr, br, seq):
    t = x2.shape[0]
    n = t // TM
    full = lambda arr: pl.BlockSpec(arr.shape, lambda i: (0,) * arr.ndim)
    row = lambda wdt: pl.BlockSpec((TM, wdt), lambda i: (i, 0))
    return pl.pallas_call(
        functools.partial(_odd_kernel, seq // TM),
        grid=(n,),
        in_specs=[row(D_MODEL), full(g), full(w), full(cw), full(wp), full(ps), full(wo),
                  full(gm), full(wr), full(br)],
        out_specs=[row(D_MODEL), row(D_MODEL), row(LANES)],
        out_shape=[jax.ShapeDtypeStruct((t, D_MODEL), F32),
                   jax.ShapeDtypeStruct((t, D_MODEL), BF16),
                   jax.ShapeDtypeStruct((t, LANES), F32)],
        scratch_shapes=[pltpu.VMEM((TM + 8, CONV_WIDTH), F32),
                        pltpu.VMEM((TM + 16, POOL_WIDTH), F32)],
        compiler_params=_cparams(("arbitrary",)),
        name="odd_mixer",
    )(x2, g, w, cw, wp, ps, wo, gm, wr, br)


def _moe_kernel(xn_ref, wg_ref, wu_ref, wd_ref, comb_ref, x1_ref, o_ref):
    g = pl.program_id(1)

    @pl.when(g == 0)
    def _():
        o_ref[...] = x1_ref[...]

    tm = xn_ref.shape[0]
    xn = xn_ref[...]
    lane = lax.broadcasted_iota(jnp.int32, (tm, LANES), 1)
    comb = comb_ref[...]
    hid = []
    for e in range(EXPERTS_PER_GROUP):
        gate = _dot(xn, wg_ref[e].astype(BF16))
        up = _dot(xn, wu_ref[e].astype(BF16))
        cw = jnp.sum(jnp.where(lane == g * EXPERTS_PER_GROUP + e, comb, 0.0), axis=1,
                     keepdims=True)
        hid.append(((gate * jax.nn.sigmoid(gate)) * up * cw).astype(BF16))
    o_ref[...] += _dot(jnp.concatenate(hid, axis=1), wd_ref[0].astype(BF16))


def _moe(xn, wg, wu, wd, comb, x1, layer):
    t = xn.shape[0]
    n = t // TM_MOE
    gh = EXPERTS_PER_GROUP * EXPERT_HIDDEN
    wblk = pl.BlockSpec((None, EXPERTS_PER_GROUP, D_MODEL, EXPERT_HIDDEN),
                        lambda i, g: (layer, g, 0, 0))
    return pl.pallas_call(
        _moe_kernel,
        grid=(n, N_GROUPS),
        in_specs=[pl.BlockSpec((TM_MOE, D_MODEL), lambda i, g: (i, 0)),
                  wblk, wblk,
                  pl.BlockSpec((None, 1, gh, D_MODEL), lambda i, g: (layer, g, 0, 0)),
                  pl.BlockSpec((TM_MOE, LANES), lambda i, g: (i, 0)),
                  pl.BlockSpec((TM_MOE, D_MODEL), lambda i, g: (i, 0))],
        out_specs=pl.BlockSpec((TM_MOE, D_MODEL), lambda i, g: (i, 0)),
        out_shape=jax.ShapeDtypeStruct((t, D_MODEL), F32),
        compiler_params=_cparams(("parallel", "arbitrary")),
        name="moe",
    )(xn, wg, wu, wd, comb, x1)


def _router_params(w_group, b_group, w_router, b_router):
    wr = jnp.concatenate([w_router.reshape(D_MODEL, N_EXPERTS), w_group], axis=1)
    wr = jnp.pad(wr, ((0, 0), (0, LANES - wr.shape[1]))).astype(BF16)
    br = jnp.concatenate([b_router.reshape(N_EXPERTS), b_group])
    br = jnp.pad(br, (0, LANES - br.shape[0])).reshape(1, LANES).astype(F32)
    return wr, br


def _moe_params(w_gate, w_up, w_down):
    wd = w_down.reshape(w_down.shape[0], N_GROUPS, EXPERTS_PER_GROUP * EXPERT_HIDDEN, D_MODEL)
    return w_gate, w_up, wd


def kernel(x, ev_norm, ev_w_in, ev_b_forget, ev_w_s, ev_b_s, ev_g_v, ev_g_q, ev_g_k, ev_w_out,
           od_norm, od_w_in, od_conv_w, od_w_pool, od_pool_scale, od_w_out,
           moe_norm, moe_w_group, moe_b_group, moe_w_router, moe_b_router,
           moe_w_gate, moe_w_up, moe_w_down):
    bsz, seq, d = x.shape
    t = bsz * seq
    x2 = x.reshape(t, d)

    w_f = ev_w_in[0][:, EVEN_MAIN:]
    w_f3 = jnp.pad(jnp.tile(w_f, (1, 3)), ((0, 0), (0, LANES - 3 * FOX_HEADS))).astype(BF16)
    bf = jnp.pad(jnp.tile(ev_b_forget[0], 3), (0, LANES - 3 * FOX_HEADS)).reshape(1, LANES)
    tril = jnp.tril(jnp.ones((CHUNK, CHUNK), F32))
    wtril = (ev_w_s[0] * tril).astype(BF16)
    bmat = jnp.repeat(ev_b_s[0].T, GMLP_HEAD, axis=1)
    gv = ev_g_v[0].reshape(1, GMLP_WIDTH)
    gq = jnp.tile(ev_g_q[0], FOX_HEADS).reshape(1, FOX_WIDTH)
    gk = jnp.tile(ev_g_k[0], FOX_HEADS).reshape(1, FOX_WIDTH)
    blk = jnp.arange(256) // HEAD_DIM
    bd = jnp.where(blk[:, None] == blk[None, :], 1.0 / HEAD_DIM, 0.0).astype(BF16)
    ltri = jnp.tril(jnp.ones((TM, TM), F32)).astype(BF16)

    a, qt, kn, vt, f3, fq = _even_in(x2, ev_norm[0].reshape(1, d), ev_w_in[0].T, w_f3, bf, wtril,
                                     bmat, gv, gq, gk, bd, ltri, seq)
    bound = (HEAD_DIM ** 0.5 * LOG2E * 1.01) * jnp.max(jnp.abs(ev_g_q[0])) * jnp.max(
        jnp.abs(ev_g_k[0]))
    o = _fox(qt, kn, f3, vt, fq, bound, bsz, seq)

    wr0, br0 = _router_params(moe_w_group[0], moe_b_group[0], moe_w_router[0], moe_b_router[0])
    w_out0 = ev_w_out[0].astype(BF16)
    x1, xn1, comb1 = _even_out(a, o, x2, w_out0,
                               moe_norm[0].reshape(1, d), wr0, br0)
    moe_w = _moe_params(moe_w_gate, moe_w_up, moe_w_down)
    xa = _moe(xn1, *moe_w, comb1, x1, 0)

    wr1, br1 = _router_params(moe_w_group[1], moe_b_group[1], moe_w_router[1], moe_b_router[1])
    x3, xn3, comb3 = _odd(xa, od_norm[0].reshape(1, d), od_w_in[0].astype(BF16), od_conv_w[0],
                          od_w_pool[0].astype(BF16), od_pool_scale[0].reshape(1, POOL_WIDTH),
                          od_w_out[0].astype(BF16), moe_norm[1].reshape(1, d), wr1, br1, seq)
    xb = _moe(xn3, *moe_w, comb3, x3, 1)
    return xb.reshape(bsz, seq, d)
```

```python
import functools

import jax
import jax.numpy as jnp
from jax import lax
from jax.experimental import pallas as pl
from jax.experimental.pallas import tpu as pltpu

F32 = jnp.float32
BF16 = jnp.bfloat16

D_MODEL = 1024
EPS = 1e-6
CHUNK = 128
GMLP_GROUPS = 4
GMLP_HEAD = 64
GMLP_WIDTH = 256
FOX_HEADS = 12
HEAD_DIM = 64
FOX_WIDTH = 768
CONV_WIDTH = 512
CONV_K = 3
POOL_WINDOWS = (2, 4, 8, 16)
POOL_GROUP = 128
POOL_WIDTH = 512
N_GROUPS = 4
EXPERTS_PER_GROUP = 4
N_EXPERTS = 16
EXPERT_HIDDEN = 256

LANES = 128
V7X_VMEM_LIMIT_BYTES = 56 * 1024 * 1024

TM = 512
TQ = 512
TQ_ONLINE = 256
OFFDIAG_ITEMS_PER_ITER = 14
DIAG_ITEMS_PER_ITER = 8
TM_MOE = 1024
SUBTILES = 2
EVEN_MAIN = 2 * GMLP_WIDTH + 3 * FOX_WIDTH
NEG = -1e30
LOG2E = 1.4426950408889634
FOX_FAST_BOUND_LOG2 = 50.0


def _cparams(sem):
    return pltpu.CompilerParams(dimension_semantics=sem,
                                vmem_limit_bytes=V7X_VMEM_LIMIT_BYTES)


def _dot(a, b):
    return jnp.dot(a, b, preferred_element_type=F32)


def _split_bf16(x, terms):
    parts = []
    r = x
    for _ in range(terms):
        p = r.astype(BF16)
        parts.append(p)
        r = r - p.astype(F32)
    return parts


def _dot_split(x, w_bf16, terms=2, w_left=False):
    acc = None
    for p in _split_bf16(x, terms):
        d = _dot(w_bf16, p) if w_left else _dot(p, w_bf16)
        acc = d if acc is None else acc + d
    return acc


def _rms_rows(x, g):
    ms = jnp.mean(x * x, axis=-1, keepdims=True)
    return x * lax.rsqrt(ms + EPS) * g


def _head_rms(x, bd, g):
    outs = []
    for c in range(x.shape[1] // 256):
        xc = x[:, c * 256:(c + 1) * 256]
        ms = _dot_split(xc * xc, bd)
        outs.append(xc * lax.rsqrt(ms + EPS))
    y = outs[0] if len(outs) == 1 else jnp.concatenate(outs, axis=1)
    return y * g


def _log_sigmoid(x):
    return -(jnp.maximum(-x, 0.0) + jnp.log(1.0 + jnp.exp(-jnp.abs(x))))


def _router(xn_bf16, wr_ref, br_ref):
    r = _dot(xn_bf16, wr_ref[...]) + br_ref[...]
    tm = r.shape[0]
    rt = r.T
    er = rt[0:N_EXPERTS]
    gr = rt[N_EXPERTS:N_EXPERTS + 8]
    grow = lax.broadcasted_iota(jnp.int32, (8, tm), 0).astype(F32)
    erow = lax.broadcasted_iota(jnp.int32, (N_EXPERTS, tm), 0).astype(F32)
    is_g = grow < float(N_GROUPS)
    gl = jnp.where(is_g, gr, NEG)
    gmax = jnp.max(gl, axis=0, keepdims=True)
    gidx = jnp.min(jnp.where(gl == gmax, grow, 999.0), axis=0, keepdims=True)
    gsum = jnp.sum(jnp.where(is_g, jnp.exp(gl - gmax), 0.0), axis=0, keepdims=True)
    gp = 1.0 / gsum
    lo = gidx * float(EXPERTS_PER_GROUP)
    sel = (erow >= lo) & (erow < lo + float(EXPERTS_PER_GROUP))
    el = jnp.where(sel, er, NEG)
    emax = jnp.max(el, axis=0, keepdims=True)
    ee = jnp.where(sel, jnp.exp(el - emax), 0.0)
    ep = ee / jnp.sum(ee, axis=0, keepdims=True)
    ep = jnp.where(sel, ep, -1.0)
    p1 = jnp.max(ep, axis=0, keepdims=True)
    i1 = jnp.min(jnp.where(ep == p1, erow, 999.0), axis=0, keepdims=True)
    ep2 = jnp.where(erow == i1, -1.0, ep)
    p2 = jnp.max(ep2, axis=0, keepdims=True)
    i2 = jnp.min(jnp.where(ep2 == p2, erow, 999.0), axis=0, keepdims=True)
    den = p1 + p2
    comb_t = (jnp.where(erow == i1, gp * (p1 / den), 0.0)
              + jnp.where(erow == i2, gp * (p2 / den), 0.0))
    comb_t = jnp.concatenate([comb_t, jnp.zeros((LANES - N_EXPERTS, tm), F32)], axis=0)
    return comb_t.T


def _even_in_kernel(tiles_per_seq, x_ref, g_ref, w_ref, wf_ref, bf_ref, wtril_ref, bmat_ref,
                    gv_ref, gq_ref, gk_ref, bd_ref, ltri_ref,
                    a_ref, q_ref, k_ref, v_ref, f_ref, fq_ref, wbf_ref, carry_ref):
    i = pl.program_id(0)
    tm = x_ref.shape[0]

    @pl.when(i == 0)
    def _():
        for c in range(0, w_ref.shape[0], 256):
            wbf_ref[:, c:c + 256] = w_ref[c:c + 256, :].T.astype(BF16)

    @pl.when(i % tiles_per_seq == 0)
    def _():
        carry_ref[...] = jnp.zeros_like(carry_ref)

    sub = tm // SUBTILES
    lane = lax.broadcasted_iota(jnp.int32, (sub, LANES), 1)
    lo_half = lane < GMLP_HEAD
    bmat = bmat_ref[...]
    bd = bd_ref[...]
    for st in range(SUBTILES):
        rows = slice(st * sub, (st + 1) * sub)
        xn = _rms_rows(x_ref[rows, :], g_ref[...]).astype(BF16)
        u = _dot(xn, wbf_ref[:, 0:256])
        v = _dot(xn, wbf_ref[:, 256:512])
        q = _dot(xn, wbf_ref[:, 512:1280])
        k = _dot(xn, wbf_ref[:, 1280:2048])
        val = _dot(xn, wbf_ref[:, 2048:2816])
        f = _dot(xn, wf_ref[...])

        gu = jax.nn.gelu(u)
        vn = _head_rms(jax.nn.gelu(v), bd, gv_ref[...])
        pair_out = []
        for pr in range(2):
            vp = vn[:, pr * 128:(pr + 1) * 128]
            v_lo = jnp.where(lo_half, vp, 0.0).astype(BF16)
            v_hi = jnp.where(lo_half, 0.0, vp).astype(BF16)
            chunks = []
            for c in range(sub // CHUNK):
                rs = slice(c * CHUNK, (c + 1) * CHUNK)
                s = (_dot(wtril_ref[2 * pr], v_lo[rs]) + _dot(wtril_ref[2 * pr + 1], v_hi[rs])
                     + bmat[:, pr * 128:(pr + 1) * 128])
                chunks.append(s)
            pair_out.append(jnp.concatenate(chunks, axis=0))
        s_all = jnp.concatenate(pair_out, axis=1)
        a_ref[rows, :] = (gu * s_all).astype(BF16)

        qn = _head_rms(q, bd, gq_ref[...]) * (HEAD_DIM ** -0.5 * LOG2E)
        for pr in range(FOX_HEADS // 2):
            ls = slice(pr * LANES, (pr + 1) * LANES)
            q_ref[pr, :, rows] = qn[:, ls].T.astype(BF16)
            v_ref[pr, :, rows] = val[:, ls].T.astype(BF16)
        k_ref[rows, :] = _head_rms(k, bd, gk_ref[...]).astype(BF16)

        logf = _log_sigmoid(f + bf_ref[...])
        cum = (_dot_split(logf, ltri_ref[...], terms=3, w_left=True)
               + carry_ref[0:1, :])
        carry_ref[0:1, :] = cum[sub - 1:sub, :]
        cum2 = cum * LOG2E
        fq_ref[rows, :] = cum2
        hi, mid, lo = (p.astype(F32) for p in _split_bf16(cum2, 3))
        f_ref[rows, :] = jnp.where(
            lane < FOX_HEADS, hi,
            jnp.where(lane < 2 * FOX_HEADS, mid,
                      jnp.where(lane < 3 * FOX_HEADS, lo,
                                jnp.where(lane < 3 * FOX_HEADS + 3, 1.0, 0.0)))).astype(BF16)


def _even_in(x2, g, w_all, wf, bf, wtril, bmat, gv, gq, gk, bd, ltri, seq):
    t = x2.shape[0]
    n = t // TM
    full = lambda a: pl.BlockSpec(a.shape, lambda i: (0,) * a.ndim)
    row = lambda wdt: pl.BlockSpec((TM, wdt), lambda i: (i, 0))
    pairs = FOX_HEADS // 2
    colT = pl.BlockSpec((pairs, LANES, TM), lambda i: (0, 0, i))
    w_main = pl.BlockSpec((EVEN_MAIN, D_MODEL), lambda i: (0, 0), pipeline_mode=pl.Buffered(1))
    return pl.pallas_call(
        functools.partial(_even_in_kernel, seq // TM),
        grid=(n,),
        in_specs=[row(D_MODEL), full(g), w_main, full(wf), full(bf), full(wtril), full(bmat),
                  full(gv), full(gq), full(gk), full(bd), full(ltri)],
        out_specs=[row(GMLP_WIDTH), colT, row(FOX_WIDTH), colT, row(LANES), row(LANES)],
        out_shape=[jax.ShapeDtypeStruct((t, GMLP_WIDTH), BF16),
                   jax.ShapeDtypeStruct((pairs, LANES, t), BF16),
                   jax.ShapeDtypeStruct((t, FOX_WIDTH), BF16),
                   jax.ShapeDtypeStruct((pairs, LANES, t), BF16),
                   jax.ShapeDtypeStruct((t, LANES), BF16),
                   jax.ShapeDtypeStruct((t, LANES), F32)],
        scratch_shapes=[pltpu.VMEM((D_MODEL, EVEN_MAIN), BF16), pltpu.VMEM((8, LANES), F32)],
        compiler_params=_cparams(("arbitrary",)),
        name="even_in",
    )(x2, g, w_all, wf, bf, wtril, bmat, gv, gq, gk, bd, ltri)


def _fox_query_operand(qt, pr, hh, shift_terms=None):
    tq = qt.shape[1]
    row = lax.broadcasted_iota(jnp.int32, (LANES, tq), 0)
    head = 2 * pr + hh
    in_head = (row >= hh * HEAD_DIM) & (row < (hh + 1) * HEAD_DIM)
    f_rows = (row == head) | (row == head + FOX_HEADS) | (row == head + 2 * FOX_HEADS)
    extra = jnp.where(f_rows, -1.0, 0.0)
    if shift_terms is not None:
        for n, term in enumerate(shift_terms):
            extra = jnp.where(row == 3 * FOX_HEADS + n, term, extra)
    return jnp.concatenate([jnp.where(in_head, qt, 0.0).astype(BF16), extra.astype(BF16)],
                           axis=0)


def _fox_online_kernel(q_ref, k_ref, f_ref, v_ref, o_ref, m_sc, l_sc, acc_sc):
    pr = pl.program_id(1)
    i = pl.program_id(2)
    tq = q_ref.shape[2]
    qt = q_ref[0].astype(F32)
    rhs = [_fox_query_operand(qt, pr, hh) for hh in range(2)]

    m_sc[...] = jnp.full_like(m_sc, NEG)
    l_sc[...] = jnp.zeros_like(l_sc)
    acc_sc[...] = jnp.zeros_like(acc_sc)

    def step(j, masked):
        start = pl.multiple_of(j * tq, tq)
        kaug = jnp.concatenate([k_ref[pl.ds(start, tq), :], f_ref[pl.ds(start, tq), :]],
                               axis=1)
        for hh in range(2):
            s = _dot(kaug, rhs[hh])
            if masked:
                r_id = lax.broadcasted_iota(jnp.int32, (tq, tq), 0)
                c_id = lax.broadcasted_iota(jnp.int32, (tq, tq), 1)
                s = jnp.where(r_id <= c_id, s, NEG)
            m_prev = m_sc[hh]
            m_new = jnp.maximum(m_prev, jnp.max(s, axis=0, keepdims=True))
            alpha = jnp.exp2(m_prev - m_new)
            p = jnp.exp2(s - m_new)
            l_sc[hh] = alpha * l_sc[hh] + jnp.sum(p, axis=0, keepdims=True)
            vt = v_ref[0, hh * HEAD_DIM:(hh + 1) * HEAD_DIM, pl.ds(start, tq)]
            acc_sc[hh] = alpha * acc_sc[hh] + _dot(vt, p.astype(BF16))
            m_sc[hh] = m_new

    def body(j, c):
        step(j, False)
        return c

    lax.fori_loop(0, i, body, 0)
    step(i, True)
    ot = jnp.concatenate([acc_sc[0] / l_sc[0], acc_sc[1] / l_sc[1]], axis=0)
    o_ref[...] = ot.T.astype(o_ref.dtype)


def _fox_fixed_kernel(b_ref, q_ref, k_ref, f_ref, v_ref, fq_ref, o_ref,
                      rhs_sc, z_sc, acc_sc, mask_sc):
    pr = pl.program_id(1)
    tq = z_sc.shape[3]
    tk = z_sc.shape[2]
    nq = rhs_sc.shape[0]
    row = lax.broadcasted_iota(jnp.int32, (LANES, tq), 0)
    for i in range(nq):
        qt = q_ref[0, :, i * tq:(i + 1) * tq].astype(F32)
        fqt = fq_ref[i * tq:(i + 1) * tq, :].T
        for hh in range(2):
            ft = jnp.sum(jnp.where(row == 2 * pr + hh, fqt, 0.0), axis=0, keepdims=True)
            shift = [p.astype(F32) for p in _split_bf16(ft - b_ref[...], 3)]
            rhs_sc[i, hh] = _fox_query_operand(qt, pr, hh, shift)

    mask_sc[...] = jnp.where(lax.broadcasted_iota(jnp.int32, (tk, tq), 0)
                             > lax.broadcasted_iota(jnp.int32, (tk, tq), 1), NEG, 0.0)
    acc_sc[...] = jnp.zeros_like(acc_sc)
    ones = jnp.ones((acc_sc.shape[2] - HEAD_DIM, tk), BF16)

    half = tk // 2

    def scores(i, j, slot, diagonal=False):
        start = pl.multiple_of(j * tk, tk)
        kaug = jnp.concatenate([k_ref[pl.ds(start, tk), :], f_ref[pl.ds(start, tk), :]],
                               axis=1)
        for hh in range(2):
            if diagonal:
                z_sc[slot, hh, 0:half, :] = _dot(kaug[0:half], rhs_sc[i, hh])
                z_sc[slot, hh, half:tk, half:tq] = _dot(kaug[half:tk], rhs_sc[i, hh, :, half:tq])
            else:
                z_sc[slot, hh] = _dot(kaug, rhs_sc[i, hh])

    def values(hh, start, size):
        return jnp.concatenate(
            [v_ref[0, hh * HEAD_DIM:(hh + 1) * HEAD_DIM, pl.ds(start, size)], ones[:, 0:size]],
            axis=0)

    def consume(i, j, slot):
        start = pl.multiple_of(j * tk, tk)
        for hh in range(2):
            p = jnp.exp2(z_sc[slot, hh]).astype(BF16)
            acc_sc[i, hh] += _dot(values(hh, start, tk), p)

    def consume_diagonal(i, slot):
        start = pl.multiple_of(i * tk, tk)
        outs = []
        for hh in range(2):
            p_top = jnp.exp2(z_sc[slot, hh, 0:half, :] + mask_sc[0:half, :]).astype(BF16)
            p_bot = jnp.exp2(z_sc[slot, hh, half:tk, half:tq]
                             + mask_sc[half:tk, half:tq]).astype(BF16)
            acc = acc_sc[i, hh] + _dot(values(hh, start, half), p_top)
            right = acc[:, half:tq] + _dot(values(hh, start + half, half), p_bot)
            acc = jnp.concatenate([acc[:, 0:half], right], axis=1)
            outs.append(acc[0:HEAD_DIM] / acc[HEAD_DIM:HEAD_DIM + 1])
        o_ref[pl.ds(pl.multiple_of(i * tq, tq), tq), :] = (
            jnp.concatenate(outs, axis=0).T.astype(o_ref.dtype))

    def after_offdiag(i, j):
        wrap = j + 1 == i
        i2 = jnp.where(wrap, i + 1, i)
        j2 = jnp.where(wrap, 0, j + 1)
        done = i2 >= nq
        return jnp.where(done, 0, i2), jnp.where(done, 0, j2)

    scores(1, 0, 0)

    def offdiag_body(_, item):
        for _ in range(OFFDIAG_ITEMS_PER_ITER // 2):
            i0, j0 = item
            i1, j1 = after_offdiag(i0, j0)
            scores(i1, j1, 1)
            consume(i0, j0, 0)
            item = after_offdiag(i1, j1)
            scores(item[0], item[1], 0)
            consume(i1, j1, 1)
        return item

    lax.fori_loop(0, nq * (nq - 1) // (2 * OFFDIAG_ITEMS_PER_ITER), offdiag_body,
                  (jnp.int32(1), jnp.int32(0)))

    def diag_body(m, c):
        for u in range(DIAG_ITEMS_PER_ITER // 2):
            i0 = DIAG_ITEMS_PER_ITER * m + 2 * u
            scores(i0 + 1, i0 + 1, 1, diagonal=True)
            consume_diagonal(i0, 0)
            i2 = jnp.minimum(i0 + 2, nq - 1)
            scores(i2, i2, 0, diagonal=True)
            consume_diagonal(i0 + 1, 1)
        return c

    lax.fori_loop(0, nq // DIAG_ITEMS_PER_ITER, diag_body, 0)


def _fox(qt, k, f3, vt, fq, bound, bsz, seq):
    t = k.shape[0]
    pairs = FOX_HEADS // 2
    out_shape = jax.ShapeDtypeStruct((t, FOX_WIDTH), BF16)
    sem = _cparams(("parallel", "parallel", "arbitrary"))

    def common(tq):
        nq = seq // tq
        in_specs = [pl.BlockSpec((1, LANES, tq), lambda b, p, i: (p, 0, b * nq + i)),
                    pl.BlockSpec((seq, LANES), lambda b, p, i: (b, p)),
                    pl.BlockSpec((seq, LANES), lambda b, p, i: (b, 0)),
                    pl.BlockSpec((1, LANES, seq), lambda b, p, i: (p, 0, b))]
        return nq, in_specs, pl.BlockSpec((tq, LANES), lambda b, p, i: (b * nq + i, p))

    def fixed(brow):
        nq = seq // TQ
        assert nq % DIAG_ITEMS_PER_ITER == 0
        assert (nq * (nq - 1) // 2) % OFFDIAG_ITEMS_PER_ITER == 0
        return pl.pallas_call(
            _fox_fixed_kernel,
            grid=(bsz, pairs),
            in_specs=[pl.BlockSpec((1, TQ), lambda b, p: (0, 0)),
                      pl.BlockSpec((1, LANES, seq), lambda b, p: (p, 0, b)),
                      pl.BlockSpec((seq, LANES), lambda b, p: (b, p)),
                      pl.BlockSpec((seq, LANES), lambda b, p: (b, 0)),
                      pl.BlockSpec((1, LANES, seq), lambda b, p: (p, 0, b)),
                      pl.BlockSpec((seq, LANES), lambda b, p: (b, 0))],
            out_specs=pl.BlockSpec((seq, LANES), lambda b, p: (b, p)),
            out_shape=out_shape,
            scratch_shapes=[pltpu.VMEM((nq, 2, 2 * LANES, TQ), BF16),
                            pltpu.VMEM((2, 2, TQ, TQ), F32),
                            pltpu.VMEM((nq, 2, HEAD_DIM + 16, TQ), F32),
                            pltpu.VMEM((TQ, TQ), F32)],
            compiler_params=_cparams(("parallel", "parallel")), name="fox_attention_fixed",
        )(brow, qt, k, f3, vt, fq)

    def online(brow):
        del brow
        nq, in_specs, out_spec = common(TQ_ONLINE)
        stats = pltpu.VMEM((2, 1, TQ_ONLINE), F32)
        return pl.pallas_call(
            _fox_online_kernel,
            grid=(bsz, pairs, nq),
            in_specs=in_specs, out_specs=out_spec, out_shape=out_shape,
            scratch_shapes=[stats, stats, pltpu.VMEM((2, HEAD_DIM, TQ_ONLINE), F32)],
            compiler_params=sem, name="fox_attention_online",
        )(qt, k, f3, vt)

    brow = jnp.full((1, TQ), bound, F32)
    return lax.cond(bound <= FOX_FAST_BOUND_LOG2, fixed, online, brow)


def _even_out_kernel(a_ref, o_ref, x_ref, w_ref, gm_ref, wr_ref, br_ref,
                     x1_ref, xn_ref, comb_ref):
    sub = x_ref.shape[0] // SUBTILES
    for s in range(SUBTILES):
        rs = slice(s * sub, (s + 1) * sub)
        x1 = x_ref[rs, :] + _dot(jnp.concatenate([a_ref[rs, :], o_ref[rs, :]], axis=1),
                                 w_ref[...])
        x1_ref[rs, :] = x1
        xn = _rms_rows(x1, gm_ref[...]).astype(BF16)
        xn_ref[rs, :] = xn
        comb_ref[rs, :] = _router(xn, wr_ref, br_ref)


def _even_out(a, o, x2, w, gm, wr, br):
    t = x2.shape[0]
    n = t // TM
    full = lambda arr: pl.BlockSpec(arr.shape, lambda i: (0,) * arr.ndim)
    row = lambda wdt: pl.BlockSpec((TM, wdt), lambda i: (i, 0))
    return pl.pallas_call(
        _even_out_kernel,
        grid=(n,),
        in_specs=[row(GMLP_WIDTH), row(FOX_WIDTH), row(D_MODEL), full(w), full(gm),
                  full(wr), full(br)],
        out_specs=[row(D_MODEL), row(D_MODEL), row(LANES)],
        out_shape=[jax.ShapeDtypeStruct((t, D_MODEL), F32),
                   jax.ShapeDtypeStruct((t, D_MODEL), BF16),
                   jax.ShapeDtypeStruct((t, LANES), F32)],
        compiler_params=_cparams(("parallel",)),
        name="even_out",
    )(a, o, x2, w, gm, wr, br)


def _odd_kernel(tiles_per_seq, x_ref, g_ref, w_ref, cw_ref, wp_ref, ps_ref, wo_ref, gm_ref,
                wr_ref, br_ref, x1_ref, xn_ref, comb_ref, zbuf, pbuf):
    i = pl.program_id(0)
    tm = x_ref.shape[0]
    zpad = zbuf.shape[0] - tm
    ppad = pbuf.shape[0] - tm
    @pl.when(i % tiles_per_seq == 0)
    def _():
        zbuf[0:zpad, :] = jnp.zeros((zpad, CONV_WIDTH), F32)
        pbuf[0:ppad, :] = jnp.zeros((ppad, POOL_WIDTH), F32)

    cw = cw_ref[...]
    ps = ps_ref[...]
    sub = tm // SUBTILES
    for s in range(SUBTILES):
        r0 = s * sub
        rs = slice(r0, r0 + sub)
        x = x_ref[rs, :]
        xn = _rms_rows(x, g_ref[...]).astype(BF16)
        cg = _dot(xn, w_ref[:, 512:1024])
        hc = _dot(xn, w_ref[:, 1024:1536])
        p = _dot(xn, w_ref[:, 1536:2048])
        bg = _dot(xn, w_ref[:, 0:512])

        z = cg * hc
        zbuf[zpad + r0:zpad + r0 + sub, :] = z
        y = (cw[0:1, :] * zbuf[zpad + r0 - 2:zpad + r0 - 2 + sub, :]
             + cw[1:2, :] * zbuf[zpad + r0 - 1:zpad + r0 - 1 + sub, :]
             + cw[2:3, :] * z)
        c = bg * y

        pbuf[ppad + r0:ppad + r0 + sub, :] = p
        pos = ((i % tiles_per_seq) * tm + r0
               + lax.broadcasted_iota(jnp.int32, (sub, POOL_GROUP), 0)).astype(F32) + 1.0
        pooled_out = []
        for gi, win in enumerate(POOL_WINDOWS):
            ls = slice(gi * POOL_GROUP, (gi + 1) * POOL_GROUP)
            pg = p[:, ls]
            sm = pg
            for sh in range(1, win):
                sm = sm + pbuf[ppad + r0 - sh:ppad + r0 - sh + sub, ls]
            cnt = jnp.minimum(pos, float(win))
            pooled = sm / cnt - pg
            pooled_out.append((_dot(pooled.astype(BF16), wp_ref[gi]) * ps[:, ls]).astype(BF16))
        mix = jnp.concatenate([c.astype(BF16)] + pooled_out, axis=1)
        acc = x + _dot(mix, wo_ref[...])

        x1_ref[rs, :] = acc
        xn2 = _rms_rows(acc, gm_ref[...]).astype(BF16)
        xn_ref[rs, :] = xn2
        comb_ref[rs, :] = _router(xn2, wr_ref, br_ref)

    zbuf[0:zpad, :] = zbuf[tm:tm + zpad, :]
    pbuf[0:ppad, :] = pbuf[tm:tm + ppad, :]


def _odd(x2, g, w, cw, wp, ps, wo, gm, wr, br, seq):
    t = x2.shape[0]
    n = t // TM
    full = lambda arr: pl.BlockSpec(arr.shape, lambda i: (0,) * arr.ndim)
    row = lambda wdt: pl.BlockSpec((TM, wdt), lambda i: (i, 0))
    return pl.pallas_call(
        functools.partial(_odd_kernel, seq // TM),
        grid=(n,),
        in_specs=[row(D_MODEL), full(g), full(w), full(cw), full(wp), full(ps), full(wo),
                  full(gm), full(wr), full(br)],
        out_specs=[row(D_MODEL), row(D_MODEL), row(LANES)],
        out_shape=[jax.ShapeDtypeStruct((t, D_MODEL), F32),
                   jax.ShapeDtypeStruct((t, D_MODEL), BF16),
                   jax.ShapeDtypeStruct((t, LANES), F32)],
        scratch_shapes=[pltpu.VMEM((TM + 8, CONV_WIDTH), F32),
                        pltpu.VMEM((TM + 16, POOL_WIDTH), F32)],
        compiler_params=_cparams(("arbitrary",)),
        name="odd_mixer",
    )(x2, g, w, cw, wp, ps, wo, gm, wr, br)


def _moe_kernel(xn_ref, wg_ref, wu_ref, wd_ref, comb_ref, x1_ref, o_ref):
    g = pl.program_id(1)

    @pl.when(g == 0)
    def _():
        o_ref[...] = x1_ref[...]

    tm = xn_ref.shape[0]
    xn = xn_ref[...]
    lane = lax.broadcasted_iota(jnp.int32, (tm, LANES), 1)
    comb = comb_ref[...]
    hid = []
    for e in range(EXPERTS_PER_GROUP):
        gate = _dot(xn, wg_ref[e].astype(BF16))
        up = _dot(xn, wu_ref[e].astype(BF16))
        cw = jnp.sum(jnp.where(lane == g * EXPERTS_PER_GROUP + e, comb, 0.0), axis=1,
                     keepdims=True)
        hid.append(((gate * jax.nn.sigmoid(gate)) * up * cw).astype(BF16))
    o_ref[...] += _dot(jnp.concatenate(hid, axis=1), wd_ref[0].astype(BF16))


def _moe(xn, wg, wu, wd, comb, x1, layer):
    t = xn.shape[0]
    n = t // TM_MOE
    gh = EXPERTS_PER_GROUP * EXPERT_HIDDEN
    wblk = pl.BlockSpec((None, EXPERTS_PER_GROUP, D_MODEL, EXPERT_HIDDEN),
                        lambda i, g: (layer, g, 0, 0))
    return pl.pallas_call(
        _moe_kernel,
        grid=(n, N_GROUPS),
        in_specs=[pl.BlockSpec((TM_MOE, D_MODEL), lambda i, g: (i, 0)),
                  wblk, wblk,
                  pl.BlockSpec((None, 1, gh, D_MODEL), lambda i, g: (layer, g, 0, 0)),
                  pl.BlockSpec((TM_MOE, LANES), lambda i, g: (i, 0)),
                  pl.BlockSpec((TM_MOE, D_MODEL), lambda i, g: (i, 0))],
        out_specs=pl.BlockSpec((TM_MOE, D_MODEL), lambda i, g: (i, 0)),
        out_shape=jax.ShapeDtypeStruct((t, D_MODEL), F32),
        compiler_params=_cparams(("parallel", "arbitrary")),
        name="moe",
    )(xn, wg, wu, wd, comb, x1)


def _router_params(w_group, b_group, w_router, b_router):
    wr = jnp.concatenate([w_router.reshape(D_MODEL, N_EXPERTS), w_group], axis=1)
    wr = jnp.pad(wr, ((0, 0), (0, LANES - wr.shape[1]))).astype(BF16)
    br = jnp.concatenate([b_router.reshape(N_EXPERTS), b_group])
    br = jnp.pad(br, (0, LANES - br.shape[0])).reshape(1, LANES).astype(F32)
    return wr, br


def _moe_params(w_gate, w_up, w_down):
    wd = w_down.reshape(w_down.shape[0], N_GROUPS, EXPERTS_PER_GROUP * EXPERT_HIDDEN, D_MODEL)
    return w_gate, w_up, wd


def kernel(x, ev_norm, ev_w_in, ev_b_forget, ev_w_s, ev_b_s, ev_g_v, ev_g_q, ev_g_k, ev_w_out,
           od_norm, od_w_in, od_conv_w, od_w_pool, od_pool_scale, od_w_out,
           moe_norm, moe_w_group, moe_b_group, moe_w_router, moe_b_router,
           moe_w_gate, moe_w_up, moe_w_down):
    bsz, seq, d = x.shape
    t = bsz * seq
    x2 = x.reshape(t, d)

    w_f = ev_w_in[0][:, EVEN_MAIN:]
    w_f3 = jnp.pad(jnp.tile(w_f, (1, 3)), ((0, 0), (0, LANES - 3 * FOX_HEADS))).astype(BF16)
    bf = jnp.pad(jnp.tile(ev_b_forget[0], 3), (0, LANES - 3 * FOX_HEADS)).reshape(1, LANES)
    tril = jnp.tril(jnp.ones((CHUNK, CHUNK), F32))
    wtril = (ev_w_s[0] * tril).astype(BF16)
    bmat = jnp.repeat(ev_b_s[0].T, GMLP_HEAD, axis=1)
    gv = ev_g_v[0].reshape(1, GMLP_WIDTH)
    gq = jnp.tile(ev_g_q[0], FOX_HEADS).reshape(1, FOX_WIDTH)
    gk = jnp.tile(ev_g_k[0], FOX_HEADS).reshape(1, FOX_WIDTH)
    blk = jnp.arange(256) // HEAD_DIM
    bd = jnp.where(blk[:, None] == blk[None, :], 1.0 / HEAD_DIM, 0.0).astype(BF16)
    ltri = jnp.tril(jnp.ones((TM // SUBTILES, TM // SUBTILES), F32)).astype(BF16)

    a, qt, kn, vt, f3, fq = _even_in(x2, ev_norm[0].reshape(1, d), ev_w_in[0].T, w_f3, bf, wtril,
                                     bmat, gv, gq, gk, bd, ltri, seq)
    bound = (HEAD_DIM ** 0.5 * LOG2E * 1.01) * jnp.max(jnp.abs(ev_g_q[0])) * jnp.max(
        jnp.abs(ev_g_k[0]))
    o = _fox(qt, kn, f3, vt, fq, bound, bsz, seq)

    wr0, br0 = _router_params(moe_w_group[0], moe_b_group[0], moe_w_router[0], moe_b_router[0])
    w_out0 = ev_w_out[0].astype(BF16)
    x1, xn1, comb1 = _even_out(a, o, x2, w_out0,
                               moe_norm[0].reshape(1, d), wr0, br0)
    moe_w = _moe_params(moe_w_gate, moe_w_up, moe_w_down)
    xa = _moe(xn1, *moe_w, comb1, x1, 0)

    wr1, br1 = _router_params(moe_w_group[1], moe_b_group[1], moe_w_router[1], moe_b_router[1])
    x3, xn3, comb3 = _odd(xa, od_norm[0].reshape(1, d), od_w_in[0].astype(BF16), od_conv_w[0],
                          od_w_pool[0].astype(BF16), od_pool_scale[0].reshape(1, POOL_WIDTH),
                          od_w_out[0].astype(BF16), moe_norm[1].reshape(1, d), wr1, br1, seq)
    xb = _moe(xn3, *moe_w, comb3, x3, 1)
    return xb.reshape(bsz, seq, d)
```

```python
import functools

import jax
import jax.numpy as jnp
from jax import lax
from jax.experimental import pallas as pl
from jax.experimental.pallas import tpu as pltpu

F32 = jnp.float32
BF16 = jnp.bfloat16

D_MODEL = 1024
EPS = 1e-6
CHUNK = 128
GMLP_GROUPS = 4
GMLP_HEAD = 64
GMLP_WIDTH = 256
FOX_HEADS = 12
HEAD_DIM = 64
FOX_WIDTH = 768
CONV_WIDTH = 512
CONV_K = 3
POOL_WINDOWS = (2, 4, 8, 16)
POOL_GROUP = 128
POOL_WIDTH = 512
N_GROUPS = 4
EXPERTS_PER_GROUP = 4
N_EXPERTS = 16
EXPERT_HIDDEN = 256

LANES = 128
V7X_VMEM_LIMIT_BYTES = 56 * 1024 * 1024

TM = 512
TQ = 512
TQ_ONLINE = 256
OFFDIAG_ITEMS_PER_ITER = 14
DIAG_ITEMS_PER_ITER = 8
TM_MOE = 1024
SUBTILES = 2
EVEN_MAIN = 2 * GMLP_WIDTH + 3 * FOX_WIDTH
NEG = -1e30
LOG2E = 1.4426950408889634
FOX_FAST_BOUND_LOG2 = 50.0


def _cparams(sem):
    return pltpu.CompilerParams(dimension_semantics=sem,
                                vmem_limit_bytes=V7X_VMEM_LIMIT_BYTES)


def _dot(a, b):
    return jnp.dot(a, b, preferred_element_type=F32)


def _split_bf16(x, terms):
    parts = []
    r = x
    for _ in range(terms):
        p = r.astype(BF16)
        parts.append(p)
        r = r - p.astype(F32)
    return parts


def _dot_split(x, w_bf16, terms=2, w_left=False):
    acc = None
    for p in _split_bf16(x, terms):
        d = _dot(w_bf16, p) if w_left else _dot(p, w_bf16)
        acc = d if acc is None else acc + d
    return acc


def _rms_rows(x, g):
    ms = jnp.mean(x * x, axis=-1, keepdims=True)
    return x * lax.rsqrt(ms + EPS) * g


def _head_rms(x, bd, g):
    outs = []
    for c in range(x.shape[1] // 256):
        xc = x[:, c * 256:(c + 1) * 256]
        ms = _dot_split(xc * xc, bd)
        outs.append(xc * lax.rsqrt(ms + EPS))
    y = outs[0] if len(outs) == 1 else jnp.concatenate(outs, axis=1)
    return y * g


def _log_sigmoid(x):
    return -(jnp.maximum(-x, 0.0) + jnp.log(1.0 + jnp.exp(-jnp.abs(x))))


def _router(xn_bf16, wr_ref, br_ref):
    r = _dot(xn_bf16, wr_ref[...]) + br_ref[...]
    tm = r.shape[0]
    rt = r.T
    er = rt[0:N_EXPERTS]
    gr = rt[N_EXPERTS:N_EXPERTS + 8]
    grow = lax.broadcasted_iota(jnp.int32, (8, tm), 0).astype(F32)
    erow = lax.broadcasted_iota(jnp.int32, (N_EXPERTS, tm), 0).astype(F32)
    is_g = grow < float(N_GROUPS)
    gl = jnp.where(is_g, gr, NEG)
    gmax = jnp.max(gl, axis=0, keepdims=True)
    gidx = jnp.min(jnp.where(gl == gmax, grow, 999.0), axis=0, keepdims=True)
    gsum = jnp.sum(jnp.where(is_g, jnp.exp(gl - gmax), 0.0), axis=0, keepdims=True)
    gp = 1.0 / gsum
    lo = gidx * float(EXPERTS_PER_GROUP)
    sel = (erow >= lo) & (erow < lo + float(EXPERTS_PER_GROUP))
    el = jnp.where(sel, er, NEG)
    emax = jnp.max(el, axis=0, keepdims=True)
    ee = jnp.where(sel, jnp.exp(el - emax), 0.0)
    ep = ee / jnp.sum(ee, axis=0, keepdims=True)
    ep = jnp.where(sel, ep, -1.0)
    p1 = jnp.max(ep, axis=0, keepdims=True)
    i1 = jnp.min(jnp.where(ep == p1, erow, 999.0), axis=0, keepdims=True)
    ep2 = jnp.where(erow == i1, -1.0, ep)
    p2 = jnp.max(ep2, axis=0, keepdims=True)
    i2 = jnp.min(jnp.where(ep2 == p2, erow, 999.0), axis=0, keepdims=True)
    den = p1 + p2
    comb_t = (jnp.where(erow == i1, gp * (p1 / den), 0.0)
              + jnp.where(erow == i2, gp * (p2 / den), 0.0))
    comb_t = jnp.concatenate([comb_t, jnp.zeros((LANES - N_EXPERTS, tm), F32)], axis=0)
    return comb_t.T


def _even_in_kernel(tiles_per_seq, x_ref, g_ref, w_ref, wf_ref, bf_ref, wtril_ref, bmat_ref,
                    gv_ref, gq_ref, gk_ref, bd_ref, ltri_ref,
                    a_ref, q_ref, k_ref, v_ref, f_ref, fq_ref, wbf_ref, carry_ref):
    i = pl.program_id(0)
    tm = x_ref.shape[0]

    @pl.when(i == 0)
    def _():
        for c in range(0, w_ref.shape[0], 256):
            wbf_ref[:, c:c + 256] = w_ref[c:c + 256, :].T.astype(BF16)

    @pl.when(i % tiles_per_seq == 0)
    def _():
        carry_ref[...] = jnp.zeros_like(carry_ref)

    sub = tm // SUBTILES
    lane = lax.broadcasted_iota(jnp.int32, (sub, LANES), 1)
    lo_half = lane < GMLP_HEAD
    bmat = bmat_ref[...]
    bd = bd_ref[...]
    xn_all = _rms_rows(x_ref[...], g_ref[...]).astype(BF16)
    u_all = _dot(xn_all, wbf_ref[:, 0:256])
    v_all = _dot(xn_all, wbf_ref[:, 256:512])
    q_all = _dot(xn_all, wbf_ref[:, 512:1280])
    k_all = _dot(xn_all, wbf_ref[:, 1280:2048])
    val_all = _dot(xn_all, wbf_ref[:, 2048:2816])
    f_all = _dot(xn_all, wf_ref[...])
    for st in range(SUBTILES):
        rows = slice(st * sub, (st + 1) * sub)
        u, v, q, k = u_all[rows], v_all[rows], q_all[rows], k_all[rows]
        val, f = val_all[rows], f_all[rows]

        gu = jax.nn.gelu(u)
        vn = _head_rms(jax.nn.gelu(v), bd, gv_ref[...])
        pair_out = []
        for pr in range(2):
            vp = vn[:, pr * 128:(pr + 1) * 128]
            v_lo = jnp.where(lo_half, vp, 0.0).astype(BF16)
            v_hi = jnp.where(lo_half, 0.0, vp).astype(BF16)
            chunks = []
            for c in range(sub // CHUNK):
                rs = slice(c * CHUNK, (c + 1) * CHUNK)
                s = (_dot(wtril_ref[2 * pr], v_lo[rs]) + _dot(wtril_ref[2 * pr + 1], v_hi[rs])
                     + bmat[:, pr * 128:(pr + 1) * 128])
                chunks.append(s)
            pair_out.append(jnp.concatenate(chunks, axis=0))
        s_all = jnp.concatenate(pair_out, axis=1)
        a_ref[rows, :] = (gu * s_all).astype(BF16)

        qn = _head_rms(q, bd, gq_ref[...]) * (HEAD_DIM ** -0.5 * LOG2E)
        for pr in range(FOX_HEADS // 2):
            ls = slice(pr * LANES, (pr + 1) * LANES)
            q_ref[pr, :, rows] = qn[:, ls].T.astype(BF16)
            v_ref[pr, :, rows] = val[:, ls].T.astype(BF16)
        k_ref[rows, :] = _head_rms(k, bd, gk_ref[...]).astype(BF16)

        logf = _log_sigmoid(f + bf_ref[...])
        cum = (_dot_split(logf, ltri_ref[...], terms=3, w_left=True)
               + carry_ref[0:1, :])
        carry_ref[0:1, :] = cum[sub - 1:sub, :]
        cum2 = cum * LOG2E
        fq_ref[rows, :] = cum2
        hi, mid, lo = (p.astype(F32) for p in _split_bf16(cum2, 3))
        f_ref[rows, :] = jnp.where(
            lane < FOX_HEADS, hi,
            jnp.where(lane < 2 * FOX_HEADS, mid,
                      jnp.where(lane < 3 * FOX_HEADS, lo,
                                jnp.where(lane < 3 * FOX_HEADS + 3, 1.0, 0.0)))).astype(BF16)


def _even_in(x2, g, w_all, wf, bf, wtril, bmat, gv, gq, gk, bd, ltri, seq):
    t = x2.shape[0]
    n = t // TM
    full = lambda a: pl.BlockSpec(a.shape, lambda i: (0,) * a.ndim)
    row = lambda wdt: pl.BlockSpec((TM, wdt), lambda i: (i, 0))
    pairs = FOX_HEADS // 2
    colT = pl.BlockSpec((pairs, LANES, TM), lambda i: (0, 0, i))
    w_main = pl.BlockSpec((EVEN_MAIN, D_MODEL), lambda i: (0, 0), pipeline_mode=pl.Buffered(1))
    return pl.pallas_call(
        functools.partial(_even_in_kernel, seq // TM),
        grid=(n,),
        in_specs=[row(D_MODEL), full(g), w_main, full(wf), full(bf), full(wtril), full(bmat),
                  full(gv), full(gq), full(gk), full(bd), full(ltri)],
        out_specs=[row(GMLP_WIDTH), colT, row(FOX_WIDTH), colT, row(LANES), row(LANES)],
        out_shape=[jax.ShapeDtypeStruct((t, GMLP_WIDTH), BF16),
                   jax.ShapeDtypeStruct((pairs, LANES, t), BF16),
                   jax.ShapeDtypeStruct((t, FOX_WIDTH), BF16),
                   jax.ShapeDtypeStruct((pairs, LANES, t), BF16),
                   jax.ShapeDtypeStruct((t, LANES), BF16),
                   jax.ShapeDtypeStruct((t, LANES), F32)],
        scratch_shapes=[pltpu.VMEM((D_MODEL, EVEN_MAIN), BF16), pltpu.VMEM((8, LANES), F32)],
        compiler_params=_cparams(("arbitrary",)),
        name="even_in",
    )(x2, g, w_all, wf, bf, wtril, bmat, gv, gq, gk, bd, ltri)


def _fox_query_operand(qt, pr, hh, shift_terms=None):
    tq = qt.shape[1]
    row = lax.broadcasted_iota(jnp.int32, (LANES, tq), 0)
    head = 2 * pr + hh
    in_head = (row >= hh * HEAD_DIM) & (row < (hh + 1) * HEAD_DIM)
    f_rows = (row == head) | (row == head + FOX_HEADS) | (row == head + 2 * FOX_HEADS)
    extra = jnp.where(f_rows, -1.0, 0.0)
    if shift_terms is not None:
        for n, term in enumerate(shift_terms):
            extra = jnp.where(row == 3 * FOX_HEADS + n, term, extra)
    return jnp.concatenate([jnp.where(in_head, qt, 0.0).astype(BF16), extra.astype(BF16)],
                           axis=0)


def _fox_online_kernel(q_ref, k_ref, f_ref, v_ref, o_ref, m_sc, l_sc, acc_sc):
    pr = pl.program_id(1)
    i = pl.program_id(2)
    tq = q_ref.shape[2]
    qt = q_ref[0].astype(F32)
    rhs = [_fox_query_operand(qt, pr, hh) for hh in range(2)]

    m_sc[...] = jnp.full_like(m_sc, NEG)
    l_sc[...] = jnp.zeros_like(l_sc)
    acc_sc[...] = jnp.zeros_like(acc_sc)

    def step(j, masked):
        start = pl.multiple_of(j * tq, tq)
        kaug = jnp.concatenate([k_ref[pl.ds(start, tq), :], f_ref[pl.ds(start, tq), :]],
                               axis=1)
        for hh in range(2):
            s = _dot(kaug, rhs[hh])
            if masked:
                r_id = lax.broadcasted_iota(jnp.int32, (tq, tq), 0)
                c_id = lax.broadcasted_iota(jnp.int32, (tq, tq), 1)
                s = jnp.where(r_id <= c_id, s, NEG)
            m_prev = m_sc[hh]
            m_new = jnp.maximum(m_prev, jnp.max(s, axis=0, keepdims=True))
            alpha = jnp.exp2(m_prev - m_new)
            p = jnp.exp2(s - m_new)
            l_sc[hh] = alpha * l_sc[hh] + jnp.sum(p, axis=0, keepdims=True)
            vt = v_ref[0, hh * HEAD_DIM:(hh + 1) * HEAD_DIM, pl.ds(start, tq)]
            acc_sc[hh] = alpha * acc_sc[hh] + _dot(vt, p.astype(BF16))
            m_sc[hh] = m_new

    def body(j, c):
        step(j, False)
        return c

    lax.fori_loop(0, i, body, 0)
    step(i, True)
    ot = jnp.concatenate([acc_sc[0] / l_sc[0], acc_sc[1] / l_sc[1]], axis=0)
    o_ref[...] = ot.T.astype(o_ref.dtype)


def _fox_fixed_kernel(b_ref, q_ref, k_ref, f_ref, v_ref, fq_ref, o_ref,
                      rhs_sc, z_sc, acc_sc, mask_sc):
    pr = pl.program_id(1)
    tq = z_sc.shape[3]
    tk = z_sc.shape[2]
    nq = rhs_sc.shape[0]
    row = lax.broadcasted_iota(jnp.int32, (LANES, tq), 0)
    for i in range(nq):
        qt = q_ref[0, :, i * tq:(i + 1) * tq].astype(F32)
        fqt = fq_ref[i * tq:(i + 1) * tq, :].T
        for hh in range(2):
            ft = jnp.sum(jnp.where(row == 2 * pr + hh, fqt, 0.0), axis=0, keepdims=True)
            shift = [p.astype(F32) for p in _split_bf16(ft - b_ref[...], 3)]
            rhs_sc[i, hh] = _fox_query_operand(qt, pr, hh, shift)

    mask_sc[...] = jnp.where(lax.broadcasted_iota(jnp.int32, (tk, tq), 0)
                             > lax.broadcasted_iota(jnp.int32, (tk, tq), 1), NEG, 0.0)
    acc_sc[...] = jnp.zeros_like(acc_sc)
    ones = jnp.ones((acc_sc.shape[2] - HEAD_DIM, tk), BF16)

    half = tk // 2

    def scores(i, j, slot, diagonal=False):
        start = pl.multiple_of(j * tk, tk)
        kaug = jnp.concatenate([k_ref[pl.ds(start, tk), :], f_ref[pl.ds(start, tk), :]],
                               axis=1)
        for hh in range(2):
            if diagonal:
                z_sc[slot, hh, 0:half, :] = _dot(kaug[0:half], rhs_sc[i, hh])
                z_sc[slot, hh, half:tk, half:tq] = _dot(kaug[half:tk], rhs_sc[i, hh, :, half:tq])
            else:
                z_sc[slot, hh] = _dot(kaug, rhs_sc[i, hh])

    def values(hh, start, size):
        return jnp.concatenate(
            [v_ref[0, hh * HEAD_DIM:(hh + 1) * HEAD_DIM, pl.ds(start, size)], ones[:, 0:size]],
            axis=0)

    def consume(i, j, slot):
        start = pl.multiple_of(j * tk, tk)
        for hh in range(2):
            p = jnp.exp2(z_sc[slot, hh]).astype(BF16)
            acc_sc[i, hh] += _dot(values(hh, start, tk), p)

    def consume_diagonal(i, slot):
        start = pl.multiple_of(i * tk, tk)
        outs = []
        for hh in range(2):
            p_top = jnp.exp2(z_sc[slot, hh, 0:half, :] + mask_sc[0:half, :]).astype(BF16)
            p_bot = jnp.exp2(z_sc[slot, hh, half:tk, half:tq]
                             + mask_sc[half:tk, half:tq]).astype(BF16)
            acc = acc_sc[i, hh] + _dot(values(hh, start, half), p_top)
            right = acc[:, half:tq] + _dot(values(hh, start + half, half), p_bot)
            acc = jnp.concatenate([acc[:, 0:half], right], axis=1)
            outs.append(acc[0:HEAD_DIM] / acc[HEAD_DIM:HEAD_DIM + 1])
        o_ref[pl.ds(pl.multiple_of(i * tq, tq), tq), :] = (
            jnp.concatenate(outs, axis=0).T.astype(o_ref.dtype))

    def after_offdiag(i, j):
        wrap = j + 1 == i
        i2 = jnp.where(wrap, i + 1, i)
        j2 = jnp.where(wrap, 0, j + 1)
        done = i2 >= nq
        return jnp.where(done, 0, i2), jnp.where(done, 0, j2)

    scores(1, 0, 0)

    def offdiag_body(_, item):
        for _ in range(OFFDIAG_ITEMS_PER_ITER // 2):
            i0, j0 = item
            i1, j1 = after_offdiag(i0, j0)
            scores(i1, j1, 1)
            consume(i0, j0, 0)
            item = after_offdiag(i1, j1)
            scores(item[0], item[1], 0)
            consume(i1, j1, 1)
        return item

    lax.fori_loop(0, nq * (nq - 1) // (2 * OFFDIAG_ITEMS_PER_ITER), offdiag_body,
                  (jnp.int32(1), jnp.int32(0)))

    def diag_body(m, c):
        for u in range(DIAG_ITEMS_PER_ITER // 2):
            i0 = DIAG_ITEMS_PER_ITER * m + 2 * u
            scores(i0 + 1, i0 + 1, 1, diagonal=True)
            consume_diagonal(i0, 0)
            i2 = jnp.minimum(i0 + 2, nq - 1)
            scores(i2, i2, 0, diagonal=True)
            consume_diagonal(i0 + 1, 1)
        return c

    lax.fori_loop(0, nq // DIAG_ITEMS_PER_ITER, diag_body, 0)


def _fox(qt, k, f3, vt, fq, bound, bsz, seq):
    t = k.shape[0]
    pairs = FOX_HEADS // 2
    out_shape = jax.ShapeDtypeStruct((t, FOX_WIDTH), BF16)
    sem = _cparams(("parallel", "parallel", "arbitrary"))

    def common(tq):
        nq = seq // tq
        in_specs = [pl.BlockSpec((1, LANES, tq), lambda b, p, i: (p, 0, b * nq + i)),
                    pl.BlockSpec((seq, LANES), lambda b, p, i: (b, p)),
                    pl.BlockSpec((seq, LANES), lambda b, p, i: (b, 0)),
                    pl.BlockSpec((1, LANES, seq), lambda b, p, i: (p, 0, b))]
        return nq, in_specs, pl.BlockSpec((tq, LANES), lambda b, p, i: (b * nq + i, p))

    def fixed(brow):
        nq = seq // TQ
        assert nq % DIAG_ITEMS_PER_ITER == 0
        assert (nq * (nq - 1) // 2) % OFFDIAG_ITEMS_PER_ITER == 0
        return pl.pallas_call(
            _fox_fixed_kernel,
            grid=(bsz, pairs),
            in_specs=[pl.BlockSpec((1, TQ), lambda b, p: (0, 0)),
                      pl.BlockSpec((1, LANES, seq), lambda b, p: (p, 0, b)),
                      pl.BlockSpec((seq, LANES), lambda b, p: (b, p)),
                      pl.BlockSpec((seq, LANES), lambda b, p: (b, 0)),
                      pl.BlockSpec((1, LANES, seq), lambda b, p: (p, 0, b)),
                      pl.BlockSpec((seq, LANES), lambda b, p: (b, 0))],
            out_specs=pl.BlockSpec((seq, LANES), lambda b, p: (b, p)),
            out_shape=out_shape,
            scratch_shapes=[pltpu.VMEM((nq, 2, 2 * LANES, TQ), BF16),
                            pltpu.VMEM((2, 2, TQ, TQ), F32),
                            pltpu.VMEM((nq, 2, HEAD_DIM + 16, TQ), F32),
                            pltpu.VMEM((TQ, TQ), F32)],
            compiler_params=_cparams(("parallel", "parallel")), name="fox_attention_fixed",
        )(brow, qt, k, f3, vt, fq)

    def online(brow):
        del brow
        nq, in_specs, out_spec = common(TQ_ONLINE)
        stats = pltpu.VMEM((2, 1, TQ_ONLINE), F32)
        return pl.pallas_call(
            _fox_online_kernel,
            grid=(bsz, pairs, nq),
            in_specs=in_specs, out_specs=out_spec, out_shape=out_shape,
            scratch_shapes=[stats, stats, pltpu.VMEM((2, HEAD_DIM, TQ_ONLINE), F32)],
            compiler_params=sem, name="fox_attention_online",
        )(qt, k, f3, vt)

    brow = jnp.full((1, TQ), bound, F32)
    return lax.cond(bound <= FOX_FAST_BOUND_LOG2, fixed, online, brow)


def _even_out_kernel(a_ref, o_ref, x_ref, w_ref, gm_ref, wr_ref, br_ref,
                     x1_ref, xn_ref, comb_ref):
    y_all = _dot(jnp.concatenate([a_ref[...], o_ref[...]], axis=1), w_ref[...])
    sub = x_ref.shape[0] // SUBTILES
    for s in range(SUBTILES):
        rs = slice(s * sub, (s + 1) * sub)
        x1 = x_ref[rs, :] + y_all[rs]
        x1_ref[rs, :] = x1
        xn = _rms_rows(x1, gm_ref[...]).astype(BF16)
        xn_ref[rs, :] = xn
        comb_ref[rs, :] = _router(xn, wr_ref, br_ref)


def _even_out(a, o, x2, w, gm, wr, br):
    t = x2.shape[0]
    n = t // TM
    full = lambda arr: pl.BlockSpec(arr.shape, lambda i: (0,) * arr.ndim)
    row = lambda wdt: pl.BlockSpec((TM, wdt), lambda i: (i, 0))
    return pl.pallas_call(
        _even_out_kernel,
        grid=(n,),
        in_specs=[row(GMLP_WIDTH), row(FOX_WIDTH), row(D_MODEL), full(w), full(gm),
                  full(wr), full(br)],
        out_specs=[row(D_MODEL), row(D_MODEL), row(LANES)],
        out_shape=[jax.ShapeDtypeStruct((t, D_MODEL), F32),
                   jax.ShapeDtypeStruct((t, D_MODEL), BF16),
                   jax.ShapeDtypeStruct((t, LANES), F32)],
        compiler_params=_cparams(("parallel",)),
        name="even_out",
    )(a, o, x2, w, gm, wr, br)


def _odd_kernel(tiles_per_seq, x_ref, g_ref, w_ref, cw_ref, wp_ref, ps_ref, wo_ref, gm_ref,
                wr_ref, br_ref, x1_ref, xn_ref, comb_ref, zbuf, pbuf):
    i = pl.program_id(0)
    tm = x_ref.shape[0]
    zpad = zbuf.shape[0] - tm
    ppad = pbuf.shape[0] - tm
    @pl.when(i % tiles_per_seq == 0)
    def _():
        zbuf[0:zpad, :] = jnp.zeros((zpad, CONV_WIDTH), F32)
        pbuf[0:ppad, :] = jnp.zeros((ppad, POOL_WIDTH), F32)

    cw = cw_ref[...]
    ps = ps_ref[...]
    sub = tm // SUBTILES
    xn_all = _rms_rows(x_ref[...], g_ref[...]).astype(BF16)
    cg_all = _dot(xn_all, w_ref[:, 512:1024])
    hc_all = _dot(xn_all, w_ref[:, 1024:1536])
    p_all = _dot(xn_all, w_ref[:, 1536:2048])
    bg_all = _dot(xn_all, w_ref[:, 0:512])
    for s in range(SUBTILES):
        r0 = s * sub
        rs = slice(r0, r0 + sub)
        x = x_ref[rs, :]
        cg, hc, p, bg = cg_all[rs], hc_all[rs], p_all[rs], bg_all[rs]

        z = cg * hc
        zbuf[zpad + r0:zpad + r0 + sub, :] = z
        y = (cw[0:1, :] * zbuf[zpad + r0 - 2:zpad + r0 - 2 + sub, :]
             + cw[1:2, :] * zbuf[zpad + r0 - 1:zpad + r0 - 1 + sub, :]
             + cw[2:3, :] * z)
        c = bg * y

        pbuf[ppad + r0:ppad + r0 + sub, :] = p
        pos = ((i % tiles_per_seq) * tm + r0
               + lax.broadcasted_iota(jnp.int32, (sub, POOL_GROUP), 0)).astype(F32) + 1.0
        pooled_out = []
        for gi, win in enumerate(POOL_WINDOWS):
            ls = slice(gi * POOL_GROUP, (gi + 1) * POOL_GROUP)
            pg = p[:, ls]
            sm = pg
            for sh in range(1, win):
                sm = sm + pbuf[ppad + r0 - sh:ppad + r0 - sh + sub, ls]
            cnt = jnp.minimum(pos, float(win))
            pooled = sm / cnt - pg
            pooled_out.append((_dot(pooled.astype(BF16), wp_ref[gi]) * ps[:, ls]).astype(BF16))
        mix = jnp.concatenate([c.astype(BF16)] + pooled_out, axis=1)
        acc = x + _dot(mix, wo_ref[...])

        x1_ref[rs, :] = acc
        xn2 = _rms_rows(acc, gm_ref[...]).astype(BF16)
        xn_ref[rs, :] = xn2
        comb_ref[rs, :] = _router(xn2, wr_ref, br_ref)

    zbuf[0:zpad, :] = zbuf[tm:tm + zpad, :]
    pbuf[0:ppad, :] = pbuf[tm:tm + ppad, :]


def _odd(x2, g, w, cw, wp, ps, wo, gm, wr, br, seq):
    t = x2.shape[0]
    n = t // TM
    full = lambda arr: pl.BlockSpec(arr.shape, lambda i: (0,) * arr.ndim)
    row = lambda wdt: pl.BlockSpec((TM, wdt), lambda i: (i, 0))
    return pl.pallas_call(
        functools.partial(_odd_kernel, seq // TM),
        grid=(n,),
        in_specs=[row(D_MODEL), full(g), full(w), full(cw), full(wp), full(ps), full(wo),
                  full(gm), full(wr), full(br)],
        out_specs=[row(D_MODEL), row(D_MODEL), row(LANES)],
        out_shape=[jax.ShapeDtypeStruct((t, D_MODEL), F32),
                   jax.ShapeDtypeStruct((t, D_MODEL), BF16),
                   jax.ShapeDtypeStruct((t, LANES), F32)],
        scratch_shapes=[pltpu.VMEM((TM + 8, CONV_WIDTH), F32),
                        pltpu.VMEM((TM + 16, POOL_WIDTH), F32)],
        compiler_params=_cparams(("arbitrary",)),
        name="odd_mixer",
    )(x2, g, w, cw, wp, ps, wo, gm, wr, br)


def _moe_kernel(xn_ref, wg_ref, wu_ref, wd_ref, comb_ref, x1_ref, o_ref):
    g = pl.program_id(1)

    @pl.when(g == 0)
    def _():
        o_ref[...] = x1_ref[...]

    tm = xn_ref.shape[0]
    xn = xn_ref[...]
    lane = lax.broadcasted_iota(jnp.int32, (tm, LANES), 1)
    comb = comb_ref[...]
    hid = []
    for e in range(EXPERTS_PER_GROUP):
        gate = _dot(xn, wg_ref[e].astype(BF16))
        up = _dot(xn, wu_ref[e].astype(BF16))
        cw = jnp.sum(jnp.where(lane == g * EXPERTS_PER_GROUP + e, comb, 0.0), axis=1,
                     keepdims=True)
        hid.append(((gate * jax.nn.sigmoid(gate)) * up * cw).astype(BF16))
    o_ref[...] += _dot(jnp.concatenate(hid, axis=1), wd_ref[0].astype(BF16))


def _moe(xn, wg, wu, wd, comb, x1, layer):
    t = xn.shape[0]
    n = t // TM_MOE
    gh = EXPERTS_PER_GROUP * EXPERT_HIDDEN
    wblk = pl.BlockSpec((None, EXPERTS_PER_GROUP, D_MODEL, EXPERT_HIDDEN),
                        lambda i, g: (layer, g, 0, 0))
    return pl.pallas_call(
        _moe_kernel,
        grid=(n, N_GROUPS),
        in_specs=[pl.BlockSpec((TM_MOE, D_MODEL), lambda i, g: (i, 0)),
                  wblk, wblk,
                  pl.BlockSpec((None, 1, gh, D_MODEL), lambda i, g: (layer, g, 0, 0)),
                  pl.BlockSpec((TM_MOE, LANES), lambda i, g: (i, 0)),
                  pl.BlockSpec((TM_MOE, D_MODEL), lambda i, g: (i, 0))],
        out_specs=pl.BlockSpec((TM_MOE, D_MODEL), lambda i, g: (i, 0)),
        out_shape=jax.ShapeDtypeStruct((t, D_MODEL), F32),
        compiler_params=_cparams(("parallel", "arbitrary")),
        name="moe",
    )(xn, wg, wu, wd, comb, x1)


def _router_params(w_group, b_group, w_router, b_router):
    wr = jnp.concatenate([w_router.reshape(D_MODEL, N_EXPERTS), w_group], axis=1)
    wr = jnp.pad(wr, ((0, 0), (0, LANES - wr.shape[1]))).astype(BF16)
    br = jnp.concatenate([b_router.reshape(N_EXPERTS), b_group])
    br = jnp.pad(br, (0, LANES - br.shape[0])).reshape(1, LANES).astype(F32)
    return wr, br


def _moe_params(w_gate, w_up, w_down):
    wd = w_down.reshape(w_down.shape[0], N_GROUPS, EXPERTS_PER_GROUP * EXPERT_HIDDEN, D_MODEL)
    return w_gate, w_up, wd


def kernel(x, ev_norm, ev_w_in, ev_b_forget, ev_w_s, ev_b_s, ev_g_v, ev_g_q, ev_g_k, ev_w_out,
           od_norm, od_w_in, od_conv_w, od_w_pool, od_pool_scale, od_w_out,
           moe_norm, moe_w_group, moe_b_group, moe_w_router, moe_b_router,
           moe_w_gate, moe_w_up, moe_w_down):
    bsz, seq, d = x.shape
    t = bsz * seq
    x2 = x.reshape(t, d)

    w_f = ev_w_in[0][:, EVEN_MAIN:]
    w_f3 = jnp.pad(jnp.tile(w_f, (1, 3)), ((0, 0), (0, LANES - 3 * FOX_HEADS))).astype(BF16)
    bf = jnp.pad(jnp.tile(ev_b_forget[0], 3), (0, LANES - 3 * FOX_HEADS)).reshape(1, LANES)
    tril = jnp.tril(jnp.ones((CHUNK, CHUNK), F32))
    wtril = (ev_w_s[0] * tril).astype(BF16)
    bmat = jnp.repeat(ev_b_s[0].T, GMLP_HEAD, axis=1)
    gv = ev_g_v[0].reshape(1, GMLP_WIDTH)
    gq = jnp.tile(ev_g_q[0], FOX_HEADS).reshape(1, FOX_WIDTH)
    gk = jnp.tile(ev_g_k[0], FOX_HEADS).reshape(1, FOX_WIDTH)
    blk = jnp.arange(256) // HEAD_DIM
    bd = jnp.where(blk[:, None] == blk[None, :], 1.0 / HEAD_DIM, 0.0).astype(BF16)
    ltri = jnp.tril(jnp.ones((TM // SUBTILES, TM // SUBTILES), F32)).astype(BF16)

    a, qt, kn, vt, f3, fq = _even_in(x2, ev_norm[0].reshape(1, d), ev_w_in[0].T, w_f3, bf, wtril,
                                     bmat, gv, gq, gk, bd, ltri, seq)
    bound = (HEAD_DIM ** 0.5 * LOG2E * 1.01) * jnp.max(jnp.abs(ev_g_q[0])) * jnp.max(
        jnp.abs(ev_g_k[0]))
    o = _fox(qt, kn, f3, vt, fq, bound, bsz, seq)

    wr0, br0 = _router_params(moe_w_group[0], moe_b_group[0], moe_w_router[0], moe_b_router[0])
    w_out0 = ev_w_out[0].astype(BF16)
    x1, xn1, comb1 = _even_out(a, o, x2, w_out0,
                               moe_norm[0].reshape(1, d), wr0, br0)
    moe_w = _moe_params(moe_w_gate, moe_w_up, moe_w_down)
    xa = _moe(xn1, *moe_w, comb1, x1, 0)

    wr1, br1 = _router_params(moe_w_group[1], moe_b_group[1], moe_w_router[1], moe_b_router[1])
    x3, xn3, comb3 = _odd(xa, od_norm[0].reshape(1, d), od_w_in[0].astype(BF16), od_conv_w[0],
                          od_w_pool[0].astype(BF16), od_pool_scale[0].reshape(1, POOL_WIDTH),
                          od_w_out[0].astype(BF16), moe_norm[1].reshape(1, d), wr1, br1, seq)
    xb = _moe(xn3, *moe_w, comb3, x3, 1)
    return xb.reshape(bsz, seq, d)
```

```python
import functools

import jax
import jax.numpy as jnp
from jax import lax
from jax.experimental import pallas as pl
from jax.experimental.pallas import tpu as pltpu

F32 = jnp.float32
BF16 = jnp.bfloat16

D_MODEL = 1024
EPS = 1e-6
CHUNK = 128
GMLP_GROUPS = 4
GMLP_HEAD = 64
GMLP_WIDTH = 256
FOX_HEADS = 12
HEAD_DIM = 64
FOX_WIDTH = 768
CONV_WIDTH = 512
CONV_K = 3
POOL_WINDOWS = (2, 4, 8, 16)
POOL_GROUP = 128
POOL_WIDTH = 512
N_GROUPS = 4
EXPERTS_PER_GROUP = 4
N_EXPERTS = 16
EXPERT_HIDDEN = 256

LANES = 128
V7X_VMEM_LIMIT_BYTES = 56 * 1024 * 1024

TM = 512
TQ = 512
TQ_ONLINE = 256
Z_SLOTS = 2
TM_MOE = 1024
SUBTILES = 2
EVEN_MAIN = 2 * GMLP_WIDTH + 3 * FOX_WIDTH
NEG = -1e30
LOG2E = 1.4426950408889634
FOX_FAST_BOUND_LOG2 = 50.0


def _cparams(sem):
    return pltpu.CompilerParams(dimension_semantics=sem,
                                vmem_limit_bytes=V7X_VMEM_LIMIT_BYTES)


def _dot(a, b):
    return jnp.dot(a, b, preferred_element_type=F32)


def _split_bf16(x, terms):
    parts = []
    r = x
    for _ in range(terms):
        p = r.astype(BF16)
        parts.append(p)
        r = r - p.astype(F32)
    return parts


def _dot_split(x, w_bf16, terms=2, w_left=False):
    acc = None
    for p in _split_bf16(x, terms):
        d = _dot(w_bf16, p) if w_left else _dot(p, w_bf16)
        acc = d if acc is None else acc + d
    return acc


def _rms_rows(x, g):
    ms = jnp.mean(x * x, axis=-1, keepdims=True)
    return x * lax.rsqrt(ms + EPS) * g


def _head_rms(x, bd, g):
    outs = []
    for c in range(x.shape[1] // 256):
        xc = x[:, c * 256:(c + 1) * 256]
        ms = _dot_split(xc * xc, bd)
        outs.append(xc * lax.rsqrt(ms + EPS))
    y = outs[0] if len(outs) == 1 else jnp.concatenate(outs, axis=1)
    return y * g


def _log_sigmoid(x):
    return -(jnp.maximum(-x, 0.0) + jnp.log(1.0 + jnp.exp(-jnp.abs(x))))


def _router(xn_bf16, wr_ref, br_ref):
    r = _dot(xn_bf16, wr_ref[...]) + br_ref[...]
    tm = r.shape[0]
    rt = r.T
    er = rt[0:N_EXPERTS]
    gr = rt[N_EXPERTS:N_EXPERTS + 8]
    grow = lax.broadcasted_iota(jnp.int32, (8, tm), 0).astype(F32)
    erow = lax.broadcasted_iota(jnp.int32, (N_EXPERTS, tm), 0).astype(F32)
    is_g = grow < float(N_GROUPS)
    gl = jnp.where(is_g, gr, NEG)
    gmax = jnp.max(gl, axis=0, keepdims=True)
    gidx = jnp.min(jnp.where(gl == gmax, grow, 999.0), axis=0, keepdims=True)
    gsum = jnp.sum(jnp.where(is_g, jnp.exp(gl - gmax), 0.0), axis=0, keepdims=True)
    gp = 1.0 / gsum
    lo = gidx * float(EXPERTS_PER_GROUP)
    sel = (erow >= lo) & (erow < lo + float(EXPERTS_PER_GROUP))
    el = jnp.where(sel, er, NEG)
    emax = jnp.max(el, axis=0, keepdims=True)
    ee = jnp.where(sel, jnp.exp(el - emax), 0.0)
    ep = ee / jnp.sum(ee, axis=0, keepdims=True)
    ep = jnp.where(sel, ep, -1.0)
    p1 = jnp.max(ep, axis=0, keepdims=True)
    i1 = jnp.min(jnp.where(ep == p1, erow, 999.0), axis=0, keepdims=True)
    ep2 = jnp.where(erow == i1, -1.0, ep)
    p2 = jnp.max(ep2, axis=0, keepdims=True)
    i2 = jnp.min(jnp.where(ep2 == p2, erow, 999.0), axis=0, keepdims=True)
    den = p1 + p2
    comb_t = (jnp.where(erow == i1, gp * (p1 / den), 0.0)
              + jnp.where(erow == i2, gp * (p2 / den), 0.0))
    comb_t = jnp.concatenate([comb_t, jnp.zeros((LANES - N_EXPERTS, tm), F32)], axis=0)
    return comb_t.T


def _even_in_kernel(tiles_per_seq, x_ref, g_ref, w_ref, wf_ref, bf_ref, wtril_ref, bmat_ref,
                    gv_ref, gq_ref, gk_ref, bd_ref, ltri_ref,
                    a_ref, q_ref, k_ref, v_ref, f_ref, fq_ref, wbf_ref, carry_ref):
    i = pl.program_id(0)
    tm = x_ref.shape[0]

    @pl.when(i == 0)
    def _():
        for c in range(0, w_ref.shape[0], 256):
            wbf_ref[:, c:c + 256] = w_ref[c:c + 256, :].T.astype(BF16)

    @pl.when(i % tiles_per_seq == 0)
    def _():
        carry_ref[...] = jnp.zeros_like(carry_ref)

    sub = tm // SUBTILES
    lane = lax.broadcasted_iota(jnp.int32, (sub, LANES), 1)
    lo_half = lane < GMLP_HEAD
    bmat = bmat_ref[...]
    bd = bd_ref[...]
    xn_all = _rms_rows(x_ref[...], g_ref[...]).astype(BF16)
    u_all = _dot(xn_all, wbf_ref[:, 0:256])
    v_all = _dot(xn_all, wbf_ref[:, 256:512])
    q_all = _dot(xn_all, wbf_ref[:, 512:1280])
    k_all = _dot(xn_all, wbf_ref[:, 1280:2048])
    val_all = _dot(xn_all, wbf_ref[:, 2048:2816])
    f_all = _dot(xn_all, wf_ref[...])
    for st in range(SUBTILES):
        rows = slice(st * sub, (st + 1) * sub)
        u, v, q, k = u_all[rows], v_all[rows], q_all[rows], k_all[rows]
        val, f = val_all[rows], f_all[rows]

        gu = jax.nn.gelu(u)
        vn = _head_rms(jax.nn.gelu(v), bd, gv_ref[...])
        pair_out = []
        for pr in range(2):
            vp = vn[:, pr * 128:(pr + 1) * 128]
            v_lo = jnp.where(lo_half, vp, 0.0).astype(BF16)
            v_hi = jnp.where(lo_half, 0.0, vp).astype(BF16)
            chunks = []
            for c in range(sub // CHUNK):
                rs = slice(c * CHUNK, (c + 1) * CHUNK)
                s = (_dot(wtril_ref[2 * pr], v_lo[rs]) + _dot(wtril_ref[2 * pr + 1], v_hi[rs])
                     + bmat[:, pr * 128:(pr + 1) * 128])
                chunks.append(s)
            pair_out.append(jnp.concatenate(chunks, axis=0))
        s_all = jnp.concatenate(pair_out, axis=1)
        a_ref[rows, :] = (gu * s_all).astype(BF16)

        qn = _head_rms(q, bd, gq_ref[...]) * (HEAD_DIM ** -0.5 * LOG2E)
        for pr in range(FOX_HEADS // 2):
            ls = slice(pr * LANES, (pr + 1) * LANES)
            q_ref[pr, :, rows] = qn[:, ls].T.astype(BF16)
            v_ref[pr, :, rows] = val[:, ls].T.astype(BF16)
        k_ref[rows, :] = _head_rms(k, bd, gk_ref[...]).astype(BF16)

        logf = _log_sigmoid(f + bf_ref[...])
        cum = (_dot_split(logf, ltri_ref[...], terms=3, w_left=True)
               + carry_ref[0:1, :])
        carry_ref[0:1, :] = cum[sub - 1:sub, :]
        cum2 = cum * LOG2E
        fq_ref[rows, :] = cum2
        hi, mid, lo = (p.astype(F32) for p in _split_bf16(cum2, 3))
        f_ref[rows, :] = jnp.where(
            lane < FOX_HEADS, hi,
            jnp.where(lane < 2 * FOX_HEADS, mid,
                      jnp.where(lane < 3 * FOX_HEADS, lo,
                                jnp.where(lane < 3 * FOX_HEADS + 3, 1.0, 0.0)))).astype(BF16)


def _even_in(x2, g, w_all, wf, bf, wtril, bmat, gv, gq, gk, bd, ltri, seq):
    t = x2.shape[0]
    n = t // TM
    full = lambda a: pl.BlockSpec(a.shape, lambda i: (0,) * a.ndim)
    row = lambda wdt: pl.BlockSpec((TM, wdt), lambda i: (i, 0))
    pairs = FOX_HEADS // 2
    colT = pl.BlockSpec((pairs, LANES, TM), lambda i: (0, 0, i))
    w_main = pl.BlockSpec((EVEN_MAIN, D_MODEL), lambda i: (0, 0), pipeline_mode=pl.Buffered(1))
    return pl.pallas_call(
        functools.partial(_even_in_kernel, seq // TM),
        grid=(n,),
        in_specs=[row(D_MODEL), full(g), w_main, full(wf), full(bf), full(wtril), full(bmat),
                  full(gv), full(gq), full(gk), full(bd), full(ltri)],
        out_specs=[row(GMLP_WIDTH), colT, row(FOX_WIDTH), colT, row(LANES), row(LANES)],
        out_shape=[jax.ShapeDtypeStruct((t, GMLP_WIDTH), BF16),
                   jax.ShapeDtypeStruct((pairs, LANES, t), BF16),
                   jax.ShapeDtypeStruct((t, FOX_WIDTH), BF16),
                   jax.ShapeDtypeStruct((pairs, LANES, t), BF16),
                   jax.ShapeDtypeStruct((t, LANES), BF16),
                   jax.ShapeDtypeStruct((t, LANES), F32)],
        scratch_shapes=[pltpu.VMEM((D_MODEL, EVEN_MAIN), BF16), pltpu.VMEM((8, LANES), F32)],
        compiler_params=_cparams(("arbitrary",)),
        name="even_in",
    )(x2, g, w_all, wf, bf, wtril, bmat, gv, gq, gk, bd, ltri)


def _fox_query_operand(qt, pr, hh, shift_terms=None):
    tq = qt.shape[1]
    row = lax.broadcasted_iota(jnp.int32, (LANES, tq), 0)
    head = 2 * pr + hh
    in_head = (row >= hh * HEAD_DIM) & (row < (hh + 1) * HEAD_DIM)
    f_rows = (row == head) | (row == head + FOX_HEADS) | (row == head + 2 * FOX_HEADS)
    extra = jnp.where(f_rows, -1.0, 0.0)
    if shift_terms is not None:
        for n, term in enumerate(shift_terms):
            extra = jnp.where(row == 3 * FOX_HEADS + n, term, extra)
    return jnp.concatenate([jnp.where(in_head, qt, 0.0).astype(BF16), extra.astype(BF16)],
                           axis=0)


def _fox_online_kernel(q_ref, k_ref, f_ref, v_ref, o_ref, m_sc, l_sc, acc_sc):
    pr = pl.program_id(1)
    i = pl.program_id(2)
    tq = q_ref.shape[2]
    qt = q_ref[0].astype(F32)
    rhs = [_fox_query_operand(qt, pr, hh) for hh in range(2)]

    m_sc[...] = jnp.full_like(m_sc, NEG)
    l_sc[...] = jnp.zeros_like(l_sc)
    acc_sc[...] = jnp.zeros_like(acc_sc)

    def step(j, masked):
        start = pl.multiple_of(j * tq, tq)
        kaug = jnp.concatenate([k_ref[pl.ds(start, tq), :], f_ref[pl.ds(start, tq), :]],
                               axis=1)
        for hh in range(2):
            s = _dot(kaug, rhs[hh])
            if masked:
                r_id = lax.broadcasted_iota(jnp.int32, (tq, tq), 0)
                c_id = lax.broadcasted_iota(jnp.int32, (tq, tq), 1)
                s = jnp.where(r_id <= c_id, s, NEG)
            m_prev = m_sc[hh]
            m_new = jnp.maximum(m_prev, jnp.max(s, axis=0, keepdims=True))
            alpha = jnp.exp2(m_prev - m_new)
            p = jnp.exp2(s - m_new)
            l_sc[hh] = alpha * l_sc[hh] + jnp.sum(p, axis=0, keepdims=True)
            vt = v_ref[0, hh * HEAD_DIM:(hh + 1) * HEAD_DIM, pl.ds(start, tq)]
            acc_sc[hh] = alpha * acc_sc[hh] + _dot(vt, p.astype(BF16))
            m_sc[hh] = m_new

    def body(j, c):
        step(j, False)
        return c

    lax.fori_loop(0, i, body, 0)
    step(i, True)
    ot = jnp.concatenate([acc_sc[0] / l_sc[0], acc_sc[1] / l_sc[1]], axis=0)
    o_ref[...] = ot.T.astype(o_ref.dtype)


def _fox_fixed_kernel(b_ref, q_ref, k_ref, f_ref, v_ref, fq_ref, o_ref,
                      rhs_sc, z_sc, acc_sc, mask_sc):
    pr = pl.program_id(1)
    tq = z_sc.shape[3]
    tk = z_sc.shape[2]
    nq = rhs_sc.shape[0]
    row = lax.broadcasted_iota(jnp.int32, (LANES, tq), 0)
    for i in range(nq):
        qt = q_ref[0, :, i * tq:(i + 1) * tq].astype(F32)
        fqt = fq_ref[i * tq:(i + 1) * tq, :].T
        for hh in range(2):
            ft = jnp.sum(jnp.where(row == 2 * pr + hh, fqt, 0.0), axis=0, keepdims=True)
            shift = [p.astype(F32) for p in _split_bf16(ft - b_ref[...], 3)]
            rhs_sc[i, hh] = _fox_query_operand(qt, pr, hh, shift)

    mask_sc[...] = jnp.where(lax.broadcasted_iota(jnp.int32, (tk, tq), 0)
                             > lax.broadcasted_iota(jnp.int32, (tk, tq), 1), NEG, 0.0)
    acc_sc[...] = jnp.zeros_like(acc_sc)
    ones = jnp.ones((acc_sc.shape[2] - HEAD_DIM, tk), BF16)

    half = tk // 2

    def scores(i, j, slot):
        rows = slice(j * tk, (j + 1) * tk)
        kaug = jnp.concatenate([k_ref[rows, :], f_ref[rows, :]], axis=1)
        for hh in range(2):
            if i == j:
                z_sc[slot, hh, 0:half, :] = _dot(kaug[0:half], rhs_sc[i, hh])
                z_sc[slot, hh, half:tk, half:tq] = _dot(kaug[half:tk], rhs_sc[i, hh, :, half:tq])
            else:
                z_sc[slot, hh] = _dot(kaug, rhs_sc[i, hh])

    def values(hh, start, size):
        return jnp.concatenate(
            [v_ref[0, hh * HEAD_DIM:(hh + 1) * HEAD_DIM, start:start + size], ones[:, 0:size]],
            axis=0)

    def consume(i, j, slot):
        for hh in range(2):
            p = jnp.exp2(z_sc[slot, hh]).astype(BF16)
            acc_sc[i, hh] += _dot(values(hh, j * tk, tk), p)

    def consume_diagonal(i, slot):
        outs = []
        for hh in range(2):
            p_top = jnp.exp2(z_sc[slot, hh, 0:half, :] + mask_sc[0:half, :]).astype(BF16)
            p_bot = jnp.exp2(z_sc[slot, hh, half:tk, half:tq]
                             + mask_sc[half:tk, half:tq]).astype(BF16)
            acc = acc_sc[i, hh] + _dot(values(hh, i * tk, half), p_top)
            right = acc[:, half:tq] + _dot(values(hh, i * tk + half, half), p_bot)
            acc = jnp.concatenate([acc[:, 0:half], right], axis=1)
            outs.append(acc[0:HEAD_DIM] / acc[HEAD_DIM:HEAD_DIM + 1])
        o_ref[i * tq:(i + 1) * tq, :] = jnp.concatenate(outs, axis=0).T.astype(o_ref.dtype)

    items = [(i, j) for i in range(1, nq) for j in range(i)] + [(i, i) for i in range(nq)]
    scores(*items[0], 0)
    for s, (i, j) in enumerate(items):
        if s + 1 < len(items):
            scores(*items[s + 1], (s + 1) % 2)
        if i == j:
            consume_diagonal(i, s % 2)
        else:
            consume(i, j, s % 2)


def _fox(qt, k, f3, vt, fq, bound, bsz, seq):
    t = k.shape[0]
    pairs = FOX_HEADS // 2
    out_shape = jax.ShapeDtypeStruct((t, FOX_WIDTH), BF16)
    sem = _cparams(("parallel", "parallel", "arbitrary"))

    def common(tq):
        nq = seq // tq
        in_specs = [pl.BlockSpec((1, LANES, tq), lambda b, p, i: (p, 0, b * nq + i)),
                    pl.BlockSpec((seq, LANES), lambda b, p, i: (b, p)),
                    pl.BlockSpec((seq, LANES), lambda b, p, i: (b, 0)),
                    pl.BlockSpec((1, LANES, seq), lambda b, p, i: (p, 0, b))]
        return nq, in_specs, pl.BlockSpec((tq, LANES), lambda b, p, i: (b * nq + i, p))

    def fixed(brow):
        nq = seq // TQ
        return pl.pallas_call(
            _fox_fixed_kernel,
            grid=(bsz, pairs),
            in_specs=[pl.BlockSpec((1, TQ), lambda b, p: (0, 0)),
                      pl.BlockSpec((1, LANES, seq), lambda b, p: (p, 0, b)),
                      pl.BlockSpec((seq, LANES), lambda b, p: (b, p)),
                      pl.BlockSpec((seq, LANES), lambda b, p: (b, 0)),
                      pl.BlockSpec((1, LANES, seq), lambda b, p: (p, 0, b)),
                      pl.BlockSpec((seq, LANES), lambda b, p: (b, 0))],
            out_specs=pl.BlockSpec((seq, LANES), lambda b, p: (b, p)),
            out_shape=out_shape,
            scratch_shapes=[pltpu.VMEM((nq, 2, 2 * LANES, TQ), BF16),
                            pltpu.VMEM((Z_SLOTS, 2, TQ, TQ), F32),
                            pltpu.VMEM((nq, 2, HEAD_DIM + 16, TQ), F32),
                            pltpu.VMEM((TQ, TQ), F32)],
            compiler_params=_cparams(("parallel", "parallel")), name="fox_attention_fixed",
        )(brow, qt, k, f3, vt, fq)

    def online(brow):
        del brow
        nq, in_specs, out_spec = common(TQ_ONLINE)
        stats = pltpu.VMEM((2, 1, TQ_ONLINE), F32)
        return pl.pallas_call(
            _fox_online_kernel,
            grid=(bsz, pairs, nq),
            in_specs=in_specs, out_specs=out_spec, out_shape=out_shape,
            scratch_shapes=[stats, stats, pltpu.VMEM((2, HEAD_DIM, TQ_ONLINE), F32)],
            compiler_params=sem, name="fox_attention_online",
        )(qt, k, f3, vt)

    brow = jnp.full((1, TQ), bound, F32)
    return lax.cond(bound <= FOX_FAST_BOUND_LOG2, fixed, online, brow)


def _even_out_kernel(a_ref, o_ref, x_ref, w_ref, gm_ref, wr_ref, br_ref,
                     x1_ref, xn_ref, comb_ref):
    y_all = _dot(jnp.concatenate([a_ref[...], o_ref[...]], axis=1), w_ref[...])
    sub = x_ref.shape[0] // SUBTILES
    for s in range(SUBTILES):
        rs = slice(s * sub, (s + 1) * sub)
        x1 = x_ref[rs, :] + y_all[rs]
        x1_ref[rs, :] = x1
        xn = _rms_rows(x1, gm_ref[...]).astype(BF16)
        xn_ref[rs, :] = xn
        comb_ref[rs, :] = _router(xn, wr_ref, br_ref)


def _even_out(a, o, x2, w, gm, wr, br):
    t = x2.shape[0]
    n = t // TM
    full = lambda arr: pl.BlockSpec(arr.shape, lambda i: (0,) * arr.ndim)
    row = lambda wdt: pl.BlockSpec((TM, wdt), lambda i: (i, 0))
    return pl.pallas_call(
        _even_out_kernel,
        grid=(n,),
        in_specs=[row(GMLP_WIDTH), row(FOX_WIDTH), row(D_MODEL), full(w), full(gm),
                  full(wr), full(br)],
        out_specs=[row(D_MODEL), row(D_MODEL), row(LANES)],
        out_shape=[jax.ShapeDtypeStruct((t, D_MODEL), F32),
                   jax.ShapeDtypeStruct((t, D_MODEL), BF16),
                   jax.ShapeDtypeStruct((t, LANES), F32)],
        compiler_params=_cparams(("parallel",)),
        name="even_out",
    )(a, o, x2, w, gm, wr, br)


def _odd_kernel(tiles_per_seq, x_ref, g_ref, w_ref, cw_ref, wp_ref, ps_ref, wo_ref, gm_ref,
                wr_ref, br_ref, x1_ref, xn_ref, comb_ref, zbuf, pbuf):
    i = pl.program_id(0)
    tm = x_ref.shape[0]
    zpad = zbuf.shape[0] - tm
    ppad = pbuf.shape[0] - tm
    @pl.when(i % tiles_per_seq == 0)
    def _():
        zbuf[0:zpad, :] = jnp.zeros((zpad, CONV_WIDTH), F32)
        pbuf[0:ppad, :] = jnp.zeros((ppad, POOL_WIDTH), F32)

    cw = cw_ref[...]
    ps = ps_ref[...]
    sub = tm // SUBTILES
    xn_all = _rms_rows(x_ref[...], g_ref[...]).astype(BF16)
    cg_all = _dot(xn_all, w_ref[:, 512:1024])
    hc_all = _dot(xn_all, w_ref[:, 1024:1536])
    p_all = _dot(xn_all, w_ref[:, 1536:2048])
    bg_all = _dot(xn_all, w_ref[:, 0:512])
    for s in range(SUBTILES):
        r0 = s * sub
        rs = slice(r0, r0 + sub)
        x = x_ref[rs, :]
        cg, hc, p, bg = cg_all[rs], hc_all[rs], p_all[rs], bg_all[rs]

        z = cg * hc
        zbuf[zpad + r0:zpad + r0 + sub, :] = z
        y = (cw[0:1, :] * zbuf[zpad + r0 - 2:zpad + r0 - 2 + sub, :]
             + cw[1:2, :] * zbuf[zpad + r0 - 1:zpad + r0 - 1 + sub, :]
             + cw[2:3, :] * z)
        c = bg * y

        pbuf[ppad + r0:ppad + r0 + sub, :] = p
        pos = ((i % tiles_per_seq) * tm + r0
               + lax.broadcasted_iota(jnp.int32, (sub, POOL_GROUP), 0)).astype(F32) + 1.0
        pooled_out = []
        for gi, win in enumerate(POOL_WINDOWS):
            ls = slice(gi * POOL_GROUP, (gi + 1) * POOL_GROUP)
            pg = p[:, ls]
            sm = pg
            for sh in range(1, win):
                sm = sm + pbuf[ppad + r0 - sh:ppad + r0 - sh + sub, ls]
            cnt = jnp.minimum(pos, float(win))
            pooled = sm / cnt - pg
            pooled_out.append((_dot(pooled.astype(BF16), wp_ref[gi]) * ps[:, ls]).astype(BF16))
        mix = jnp.concatenate([c.astype(BF16)] + pooled_out, axis=1)
        acc = x + _dot(mix, wo_ref[...])

        x1_ref[rs, :] = acc
        xn2 = _rms_rows(acc, gm_ref[...]).astype(BF16)
        xn_ref[rs, :] = xn2
        comb_ref[rs, :] = _router(xn2, wr_ref, br_ref)

    zbuf[0:zpad, :] = zbuf[tm:tm + zpad, :]
    pbuf[0:ppad, :] = pbuf[tm:tm + ppad, :]


def _odd(x2, g, w, cw, wp, ps, wo, gm, wr, br, seq):
    t = x2.shape[0]
    n = t // TM
    full = lambda arr: pl.BlockSpec(arr.shape, lambda i: (0,) * arr.ndim)
    row = lambda wdt: pl.BlockSpec((TM, wdt), lambda i: (i, 0))
    return pl.pallas_call(
        functools.partial(_odd_kernel, seq // TM),
        grid=(n,),
        in_specs=[row(D_MODEL), full(g), full(w), full(cw), full(wp), full(ps), full(wo),
                  full(gm), full(wr), full(br)],
        out_specs=[row(D_MODEL), row(D_MODEL), row(LANES)],
        out_shape=[jax.ShapeDtypeStruct((t, D_MODEL), F32),
                   jax.ShapeDtypeStruct((t, D_MODEL), BF16),
                   jax.ShapeDtypeStruct((t, LANES), F32)],
        scratch_shapes=[pltpu.VMEM((TM + 8, CONV_WIDTH), F32),
                        pltpu.VMEM((TM + 16, POOL_WIDTH), F32)],
        compiler_params=_cparams(("arbitrary",)),
        name="odd_mixer",
    )(x2, g, w, cw, wp, ps, wo, gm, wr, br)


def _moe_kernel(xn_ref, wg_ref, wu_ref, wd_ref, comb_ref, x1_ref, o_ref):
    g = pl.program_id(1)

    @pl.when(g == 0)
    def _():
        o_ref[...] = x1_ref[...]

    tm = xn_ref.shape[0]
    xn = xn_ref[...]
    lane = lax.broadcasted_iota(jnp.int32, (tm, LANES), 1)
    comb = comb_ref[...]
    hid = []
    for e in range(EXPERTS_PER_GROUP):
        gate = _dot(xn, wg_ref[e].astype(BF16))
        up = _dot(xn, wu_ref[e].astype(BF16))
        cw = jnp.sum(jnp.where(lane == g * EXPERTS_PER_GROUP + e, comb, 0.0), axis=1,
                     keepdims=True)
        hid.append(((gate * jax.nn.sigmoid(gate)) * up * cw).astype(BF16))
    o_ref[...] += _dot(jnp.concatenate(hid, axis=1), wd_ref[0].astype(BF16))


def _moe(xn, wg, wu, wd, comb, x1, layer):
    t = xn.shape[0]
    n = t // TM_MOE
    gh = EXPERTS_PER_GROUP * EXPERT_HIDDEN
    wblk = pl.BlockSpec((None, EXPERTS_PER_GROUP, D_MODEL, EXPERT_HIDDEN),
                        lambda i, g: (layer, g, 0, 0))
    return pl.pallas_call(
        _moe_kernel,
        grid=(n, N_GROUPS),
        in_specs=[pl.BlockSpec((TM_MOE, D_MODEL), lambda i, g: (i, 0)),
                  wblk, wblk,
                  pl.BlockSpec((None, 1, gh, D_MODEL), lambda i, g: (layer, g, 0, 0)),
                  pl.BlockSpec((TM_MOE, LANES), lambda i, g: (i, 0)),
                  pl.BlockSpec((TM_MOE, D_MODEL), lambda i, g: (i, 0))],
        out_specs=pl.BlockSpec((TM_MOE, D_MODEL), lambda i, g: (i, 0)),
        out_shape=jax.ShapeDtypeStruct((t, D_MODEL), F32),
        compiler_params=_cparams(("parallel", "arbitrary")),
        name="moe",
    )(xn, wg, wu, wd, comb, x1)


def _router_params(w_group, b_group, w_router, b_router):
    wr = jnp.concatenate([w_router.reshape(D_MODEL, N_EXPERTS), w_group], axis=1)
    wr = jnp.pad(wr, ((0, 0), (0, LANES - wr.shape[1]))).astype(BF16)
    br = jnp.concatenate([b_router.reshape(N_EXPERTS), b_group])
    br = jnp.pad(br, (0, LANES - br.shape[0])).reshape(1, LANES).astype(F32)
    return wr, br


def _moe_params(w_gate, w_up, w_down):
    wd = w_down.reshape(w_down.shape[0], N_GROUPS, EXPERTS_PER_GROUP * EXPERT_HIDDEN, D_MODEL)
    return w_gate, w_up, wd


def kernel(x, ev_norm, ev_w_in, ev_b_forget, ev_w_s, ev_b_s, ev_g_v, ev_g_q, ev_g_k, ev_w_out,
           od_norm, od_w_in, od_conv_w, od_w_pool, od_pool_scale, od_w_out,
           moe_norm, moe_w_group, moe_b_group, moe_w_router, moe_b_router,
           moe_w_gate, moe_w_up, moe_w_down):
    bsz, seq, d = x.shape
    t = bsz * seq
    x2 = x.reshape(t, d)

    w_f = ev_w_in[0][:, EVEN_MAIN:]
    w_f3 = jnp.pad(jnp.tile(w_f, (1, 3)), ((0, 0), (0, LANES - 3 * FOX_HEADS))).astype(BF16)
    bf = jnp.pad(jnp.tile(ev_b_forget[0], 3), (0, LANES - 3 * FOX_HEADS)).reshape(1, LANES)
    tril = jnp.tril(jnp.ones((CHUNK, CHUNK), F32))
    wtril = (ev_w_s[0] * tril).astype(BF16)
    bmat = jnp.repeat(ev_b_s[0].T, GMLP_HEAD, axis=1)
    gv = ev_g_v[0].reshape(1, GMLP_WIDTH)
    gq = jnp.tile(ev_g_q[0], FOX_HEADS).reshape(1, FOX_WIDTH)
    gk = jnp.tile(ev_g_k[0], FOX_HEADS).reshape(1, FOX_WIDTH)
    blk = jnp.arange(256) // HEAD_DIM
    bd = jnp.where(blk[:, None] == blk[None, :], 1.0 / HEAD_DIM, 0.0).astype(BF16)
    ltri = jnp.tril(jnp.ones((TM // SUBTILES, TM // SUBTILES), F32)).astype(BF16)

    a, qt, kn, vt, f3, fq = _even_in(x2, ev_norm[0].reshape(1, d), ev_w_in[0].T, w_f3, bf, wtril,
                                     bmat, gv, gq, gk, bd, ltri, seq)
    bound = (HEAD_DIM ** 0.5 * LOG2E * 1.01) * jnp.max(jnp.abs(ev_g_q[0])) * jnp.max(
        jnp.abs(ev_g_k[0]))
    o = _fox(qt, kn, f3, vt, fq, bound, bsz, seq)

    wr0, br0 = _router_params(moe_w_group[0], moe_b_group[0], moe_w_router[0], moe_b_router[0])
    w_out0 = ev_w_out[0].astype(BF16)
    x1, xn1, comb1 = _even_out(a, o, x2, w_out0,
                               moe_norm[0].reshape(1, d), wr0, br0)
    moe_w = _moe_params(moe_w_gate, moe_w_up, moe_w_down)
    xa = _moe(xn1, *moe_w, comb1, x1, 0)

    wr1, br1 = _router_params(moe_w_group[1], moe_b_group[1], moe_w_router[1], moe_b_router[1])
    x3, xn3, comb3 = _odd(xa, od_norm[0].reshape(1, d), od_w_in[0].astype(BF16), od_conv_w[0],
                          od_w_pool[0].astype(BF16), od_pool_scale[0].reshape(1, POOL_WIDTH),
                          od_w_out[0].astype(BF16), moe_norm[1].reshape(1, d), wr1, br1, seq)
    xb = _moe(xn3, *moe_w, comb3, x3, 1)
    return xb.reshape(bsz, seq, d)
```

```python
import functools

import jax
import jax.numpy as jnp
from jax import lax
from jax.experimental import pallas as pl
from jax.experimental.pallas import tpu as pltpu

F32 = jnp.float32
BF16 = jnp.bfloat16

D_MODEL = 1024
EPS = 1e-6
CHUNK = 128
GMLP_GROUPS = 4
GMLP_HEAD = 64
GMLP_WIDTH = 256
FOX_HEADS = 12
HEAD_DIM = 64
FOX_WIDTH = 768
CONV_WIDTH = 512
CONV_K = 3
POOL_WINDOWS = (2, 4, 8, 16)
POOL_GROUP = 128
POOL_WIDTH = 512
N_GROUPS = 4
EXPERTS_PER_GROUP = 4
N_EXPERTS = 16
EXPERT_HIDDEN = 256

LANES = 128
V7X_VMEM_LIMIT_BYTES = 56 * 1024 * 1024

TM = 512
TQ = 512
TQ_ONLINE = 256
Z_SLOTS = 2
TM_MOE = 1024
SUBTILES = 2
EVEN_MAIN = 2 * GMLP_WIDTH + 3 * FOX_WIDTH
NEG = -1e30
LOG2E = 1.4426950408889634
FOX_FAST_BOUND_LOG2 = 50.0


def _cparams(sem):
    return pltpu.CompilerParams(dimension_semantics=sem,
                                vmem_limit_bytes=V7X_VMEM_LIMIT_BYTES)


def _dot(a, b):
    return jnp.dot(a, b, preferred_element_type=F32)


def _split_bf16(x, terms):
    parts = []
    r = x
    for _ in range(terms):
        p = r.astype(BF16)
        parts.append(p)
        r = r - p.astype(F32)
    return parts


def _dot_split(x, w_bf16, terms=2, w_left=False):
    acc = None
    for p in _split_bf16(x, terms):
        d = _dot(w_bf16, p) if w_left else _dot(p, w_bf16)
        acc = d if acc is None else acc + d
    return acc


def _rms_rows(x, g):
    ms = jnp.mean(x * x, axis=-1, keepdims=True)
    return x * lax.rsqrt(ms + EPS) * g


def _head_rms(x, bd, g):
    outs = []
    for c in range(x.shape[1] // 256):
        xc = x[:, c * 256:(c + 1) * 256]
        ms = _dot_split(xc * xc, bd)
        outs.append(xc * lax.rsqrt(ms + EPS))
    y = outs[0] if len(outs) == 1 else jnp.concatenate(outs, axis=1)
    return y * g


def _log_sigmoid(x):
    return -(jnp.maximum(-x, 0.0) + jnp.log(1.0 + jnp.exp(-jnp.abs(x))))


def _router(xn_bf16, wr_ref, br_ref):
    r = _dot(xn_bf16, wr_ref[...]) + br_ref[...]
    tm = r.shape[0]
    rt = r.T
    er = rt[0:N_EXPERTS]
    gr = rt[N_EXPERTS:N_EXPERTS + 8]
    grow = lax.broadcasted_iota(jnp.int32, (8, tm), 0).astype(F32)
    erow = lax.broadcasted_iota(jnp.int32, (N_EXPERTS, tm), 0).astype(F32)
    is_g = grow < float(N_GROUPS)
    gl = jnp.where(is_g, gr, NEG)
    gmax = jnp.max(gl, axis=0, keepdims=True)
    gidx = jnp.min(jnp.where(gl == gmax, grow, 999.0), axis=0, keepdims=True)
    gsum = jnp.sum(jnp.where(is_g, jnp.exp(gl - gmax), 0.0), axis=0, keepdims=True)
    gp = 1.0 / gsum
    lo = gidx * float(EXPERTS_PER_GROUP)
    sel = (erow >= lo) & (erow < lo + float(EXPERTS_PER_GROUP))
    el = jnp.where(sel, er, NEG)
    emax = jnp.max(el, axis=0, keepdims=True)
    ee = jnp.where(sel, jnp.exp(el - emax), 0.0)
    ep = ee / jnp.sum(ee, axis=0, keepdims=True)
    ep = jnp.where(sel, ep, -1.0)
    p1 = jnp.max(ep, axis=0, keepdims=True)
    i1 = jnp.min(jnp.where(ep == p1, erow, 999.0), axis=0, keepdims=True)
    ep2 = jnp.where(erow == i1, -1.0, ep)
    p2 = jnp.max(ep2, axis=0, keepdims=True)
    i2 = jnp.min(jnp.where(ep2 == p2, erow, 999.0), axis=0, keepdims=True)
    den = p1 + p2
    comb_t = (jnp.where(erow == i1, gp * (p1 / den), 0.0)
              + jnp.where(erow == i2, gp * (p2 / den), 0.0))
    comb_t = jnp.concatenate([comb_t, jnp.zeros((LANES - N_EXPERTS, tm), F32)], axis=0)
    return comb_t.T


def _even_in_kernel(tiles_per_seq, x_ref, g_ref, w_ref, wf_ref, bf_ref, wtril_ref, bmat_ref,
                    gv_ref, gq_ref, gk_ref, bd_ref, ltri_ref,
                    a_ref, q_ref, k_ref, v_ref, f_ref, fq_ref, wbf_ref, carry_ref):
    i = pl.program_id(0)
    tm = x_ref.shape[0]

    @pl.when(i == 0)
    def _():
        for c in range(0, w_ref.shape[0], 256):
            wbf_ref[:, c:c + 256] = w_ref[c:c + 256, :].T.astype(BF16)

    @pl.when(i % tiles_per_seq == 0)
    def _():
        carry_ref[...] = jnp.zeros_like(carry_ref)

    sub = tm // SUBTILES
    lane = lax.broadcasted_iota(jnp.int32, (sub, LANES), 1)
    lo_half = lane < GMLP_HEAD
    bmat = bmat_ref[...]
    bd = bd_ref[...]
    xn_all = _rms_rows(x_ref[...], g_ref[...]).astype(BF16)
    u_all = _dot(xn_all, wbf_ref[:, 0:256])
    v_all = _dot(xn_all, wbf_ref[:, 256:512])
    q_all = _dot(xn_all, wbf_ref[:, 512:1280])
    k_all = _dot(xn_all, wbf_ref[:, 1280:2048])
    val_all = _dot(xn_all, wbf_ref[:, 2048:2816])
    f_all = _dot(xn_all, wf_ref[...])
    for st in range(SUBTILES):
        rows = slice(st * sub, (st + 1) * sub)
        u, v, q, k = u_all[rows], v_all[rows], q_all[rows], k_all[rows]
        val, f = val_all[rows], f_all[rows]

        gu = jax.nn.gelu(u)
        vn = _head_rms(jax.nn.gelu(v), bd, gv_ref[...])
        pair_out = []
        for pr in range(2):
            vp = vn[:, pr * 128:(pr + 1) * 128]
            v_lo = jnp.where(lo_half, vp, 0.0).astype(BF16)
            v_hi = jnp.where(lo_half, 0.0, vp).astype(BF16)
            chunks = []
            for c in range(sub // CHUNK):
                rs = slice(c * CHUNK, (c + 1) * CHUNK)
                s = (_dot(wtril_ref[2 * pr], v_lo[rs]) + _dot(wtril_ref[2 * pr + 1], v_hi[rs])
                     + bmat[:, pr * 128:(pr + 1) * 128])
                chunks.append(s)
            pair_out.append(jnp.concatenate(chunks, axis=0))
        s_all = jnp.concatenate(pair_out, axis=1)
        a_ref[rows, :] = (gu * s_all).astype(BF16)

        for pr in range(FOX_HEADS // 2):
            ls = slice(pr * LANES, (pr + 1) * LANES)
            qt = q[:, ls].T
            heads = []
            for hh in range(2):
                xh = qt[hh * HEAD_DIM:(hh + 1) * HEAD_DIM]
                ms = jnp.sum(xh * xh, axis=0, keepdims=True) * (1.0 / HEAD_DIM)
                heads.append(xh * lax.rsqrt(ms + EPS))
            q_ref[pr, :, rows] = (jnp.concatenate(heads, axis=0) * gq_ref[...]).astype(BF16)
            v_ref[pr, :, rows] = val[:, ls].T.astype(BF16)
        k_ref[rows, :] = _head_rms(k, bd, gk_ref[...]).astype(BF16)

        logf = _log_sigmoid(f + bf_ref[...])
        cum = (_dot_split(logf, ltri_ref[...], terms=3, w_left=True)
               + carry_ref[0:1, :])
        carry_ref[0:1, :] = cum[sub - 1:sub, :]
        cum2 = cum * LOG2E
        fq_ref[rows, :] = cum2
        hi, mid, lo = (p.astype(F32) for p in _split_bf16(cum2, 3))
        f_ref[rows, :] = jnp.where(
            lane < FOX_HEADS, hi,
            jnp.where(lane < 2 * FOX_HEADS, mid,
                      jnp.where(lane < 3 * FOX_HEADS, lo,
                                jnp.where(lane < 3 * FOX_HEADS + 3, 1.0, 0.0)))).astype(BF16)


def _even_in(x2, g, w_all, wf, bf, wtril, bmat, gv, gq, gk, bd, ltri, seq):
    t = x2.shape[0]
    n = t // TM
    full = lambda a: pl.BlockSpec(a.shape, lambda i: (0,) * a.ndim)
    row = lambda wdt: pl.BlockSpec((TM, wdt), lambda i: (i, 0))
    pairs = FOX_HEADS // 2
    colT = pl.BlockSpec((pairs, LANES, TM), lambda i: (0, 0, i))
    w_main = pl.BlockSpec((EVEN_MAIN, D_MODEL), lambda i: (0, 0), pipeline_mode=pl.Buffered(1))
    return pl.pallas_call(
        functools.partial(_even_in_kernel, seq // TM),
        grid=(n,),
        in_specs=[row(D_MODEL), full(g), w_main, full(wf), full(bf), full(wtril), full(bmat),
                  full(gv), full(gq), full(gk), full(bd), full(ltri)],
        out_specs=[row(GMLP_WIDTH), colT, row(FOX_WIDTH), colT, row(LANES), row(LANES)],
        out_shape=[jax.ShapeDtypeStruct((t, GMLP_WIDTH), BF16),
                   jax.ShapeDtypeStruct((pairs, LANES, t), BF16),
                   jax.ShapeDtypeStruct((t, FOX_WIDTH), BF16),
                   jax.ShapeDtypeStruct((pairs, LANES, t), BF16),
                   jax.ShapeDtypeStruct((t, LANES), BF16),
                   jax.ShapeDtypeStruct((t, LANES), F32)],
        scratch_shapes=[pltpu.VMEM((D_MODEL, EVEN_MAIN), BF16), pltpu.VMEM((8, LANES), F32)],
        compiler_params=_cparams(("arbitrary",)),
        name="even_in",
    )(x2, g, w_all, wf, bf, wtril, bmat, gv, gq, gk, bd, ltri)


def _fox_query_operand(qt, pr, hh, shift_terms=None):
    tq = qt.shape[1]
    row = lax.broadcasted_iota(jnp.int32, (LANES, tq), 0)
    head = 2 * pr + hh
    in_head = (row >= hh * HEAD_DIM) & (row < (hh + 1) * HEAD_DIM)
    f_rows = (row == head) | (row == head + FOX_HEADS) | (row == head + 2 * FOX_HEADS)
    extra = jnp.where(f_rows, -1.0, 0.0)
    if shift_terms is not None:
        for n, term in enumerate(shift_terms):
            extra = jnp.where(row == 3 * FOX_HEADS + n, term, extra)
    return jnp.concatenate([jnp.where(in_head, qt, 0.0).astype(BF16), extra.astype(BF16)],
                           axis=0)


def _fox_online_kernel(q_ref, k_ref, f_ref, v_ref, o_ref, m_sc, l_sc, acc_sc):
    pr = pl.program_id(1)
    i = pl.program_id(2)
    tq = q_ref.shape[2]
    qt = q_ref[0].astype(F32)
    rhs = [_fox_query_operand(qt, pr, hh) for hh in range(2)]

    m_sc[...] = jnp.full_like(m_sc, NEG)
    l_sc[...] = jnp.zeros_like(l_sc)
    acc_sc[...] = jnp.zeros_like(acc_sc)

    def step(j, masked):
        start = pl.multiple_of(j * tq, tq)
        kaug = jnp.concatenate([k_ref[pl.ds(start, tq), :], f_ref[pl.ds(start, tq), :]],
                               axis=1)
        for hh in range(2):
            s = _dot(kaug, rhs[hh])
            if masked:
                r_id = lax.broadcasted_iota(jnp.int32, (tq, tq), 0)
                c_id = lax.broadcasted_iota(jnp.int32, (tq, tq), 1)
                s = jnp.where(r_id <= c_id, s, NEG)
            m_prev = m_sc[hh]
            m_new = jnp.maximum(m_prev, jnp.max(s, axis=0, keepdims=True))
            alpha = jnp.exp2(m_prev - m_new)
            p = jnp.exp2(s - m_new)
            l_sc[hh] = alpha * l_sc[hh] + jnp.sum(p, axis=0, keepdims=True)
            vt = v_ref[0, hh * HEAD_DIM:(hh + 1) * HEAD_DIM, pl.ds(start, tq)]
            acc_sc[hh] = alpha * acc_sc[hh] + _dot(vt, p.astype(BF16))
            m_sc[hh] = m_new

    def body(j, c):
        step(j, False)
        return c

    lax.fori_loop(0, i, body, 0)
    step(i, True)
    ot = jnp.concatenate([acc_sc[0] / l_sc[0], acc_sc[1] / l_sc[1]], axis=0)
    o_ref[...] = ot.T.astype(o_ref.dtype)


def _fox_fixed_kernel(b_ref, q_ref, k_ref, f_ref, v_ref, fq_ref, o_ref,
                      rhs_sc, z_sc, acc_sc, mask_sc):
    pr = pl.program_id(1)
    tq = z_sc.shape[3]
    tk = z_sc.shape[2]
    nq = rhs_sc.shape[0]
    row = lax.broadcasted_iota(jnp.int32, (LANES, tq), 0)
    for i in range(nq):
        qt = q_ref[0, :, i * tq:(i + 1) * tq].astype(F32)
        fqt = fq_ref[i * tq:(i + 1) * tq, :].T
        for hh in range(2):
            ft = jnp.sum(jnp.where(row == 2 * pr + hh, fqt, 0.0), axis=0, keepdims=True)
            shift = [p.astype(F32) for p in _split_bf16(ft - b_ref[...], 3)]
            rhs_sc[i, hh] = _fox_query_operand(qt, pr, hh, shift)

    mask_sc[...] = jnp.where(lax.broadcasted_iota(jnp.int32, (tk, tq), 0)
                             > lax.broadcasted_iota(jnp.int32, (tk, tq), 1), NEG, 0.0)
    acc_sc[...] = jnp.zeros_like(acc_sc)
    ones = jnp.ones((acc_sc.shape[2] - HEAD_DIM, tk), BF16)

    half = tk // 2

    def scores(i, j, slot):
        rows = slice(j * tk, (j + 1) * tk)
        kaug = jnp.concatenate([k_ref[rows, :], f_ref[rows, :]], axis=1)
        for hh in range(2):
            if i == j:
                z_sc[slot, hh, 0:half, :] = _dot(kaug[0:half], rhs_sc[i, hh])
                z_sc[slot, hh, half:tk, half:tq] = _dot(kaug[half:tk], rhs_sc[i, hh, :, half:tq])
            else:
                z_sc[slot, hh] = _dot(kaug, rhs_sc[i, hh])

    def values(hh, start, size):
        return jnp.concatenate(
            [v_ref[0, hh * HEAD_DIM:(hh + 1) * HEAD_DIM, start:start + size], ones[:, 0:size]],
            axis=0)

    def consume(i, j, slot):
        for hh in range(2):
            p = jnp.exp2(z_sc[slot, hh]).astype(BF16)
            acc_sc[i, hh] += _dot(values(hh, j * tk, tk), p)

    def consume_diagonal(i, slot):
        outs = []
        for hh in range(2):
            p_top = jnp.exp2(z_sc[slot, hh, 0:half, :] + mask_sc[0:half, :]).astype(BF16)
            p_bot = jnp.exp2(z_sc[slot, hh, half:tk, half:tq]
                             + mask_sc[half:tk, half:tq]).astype(BF16)
            acc = acc_sc[i, hh] + _dot(values(hh, i * tk, half), p_top)
            right = acc[:, half:tq] + _dot(values(hh, i * tk + half, half), p_bot)
            acc = jnp.concatenate([acc[:, 0:half], right], axis=1)
            outs.append(acc[0:HEAD_DIM] / acc[HEAD_DIM:HEAD_DIM + 1])
        o_ref[i * tq:(i + 1) * tq, :] = jnp.concatenate(outs, axis=0).T.astype(o_ref.dtype)

    items = [(i, j) for i in range(1, nq) for j in range(i)] + [(i, i) for i in range(nq)]
    scores(*items[0], 0)
    for s, (i, j) in enumerate(items):
        if s + 1 < len(items):
            scores(*items[s + 1], (s + 1) % 2)
        if i == j:
            consume_diagonal(i, s % 2)
        else:
            consume(i, j, s % 2)


def _fox(qt, k, f3, vt, fq, bound, bsz, seq):
    t = k.shape[0]
    pairs = FOX_HEADS // 2
    out_shape = jax.ShapeDtypeStruct((t, FOX_WIDTH), BF16)
    sem = _cparams(("parallel", "parallel", "arbitrary"))

    def common(tq):
        nq = seq // tq
        in_specs = [pl.BlockSpec((1, LANES, tq), lambda b, p, i: (p, 0, b * nq + i)),
                    pl.BlockSpec((seq, LANES), lambda b, p, i: (b, p)),
                    pl.BlockSpec((seq, LANES), lambda b, p, i: (b, 0)),
                    pl.BlockSpec((1, LANES, seq), lambda b, p, i: (p, 0, b))]
        return nq, in_specs, pl.BlockSpec((tq, LANES), lambda b, p, i: (b * nq + i, p))

    def fixed(brow):
        nq = seq // TQ
        return pl.pallas_call(
            _fox_fixed_kernel,
            grid=(bsz, pairs),
            in_specs=[pl.BlockSpec((1, TQ), lambda b, p: (0, 0)),
                      pl.BlockSpec((1, LANES, seq), lambda b, p: (p, 0, b)),
                      pl.BlockSpec((seq, LANES), lambda b, p: (b, p)),
                      pl.BlockSpec((seq, LANES), lambda b, p: (b, 0)),
                      pl.BlockSpec((1, LANES, seq), lambda b, p: (p, 0, b)),
                      pl.BlockSpec((seq, LANES), lambda b, p: (b, 0))],
            out_specs=pl.BlockSpec((seq, LANES), lambda b, p: (b, p)),
            out_shape=out_shape,
            scratch_shapes=[pltpu.VMEM((nq, 2, 2 * LANES, TQ), BF16),
                            pltpu.VMEM((Z_SLOTS, 2, TQ, TQ), F32),
                            pltpu.VMEM((nq, 2, HEAD_DIM + 16, TQ), F32),
                            pltpu.VMEM((TQ, TQ), F32)],
            compiler_params=_cparams(("parallel", "parallel")), name="fox_attention_fixed",
        )(brow, qt, k, f3, vt, fq)

    def online(brow):
        del brow
        nq, in_specs, out_spec = common(TQ_ONLINE)
        stats = pltpu.VMEM((2, 1, TQ_ONLINE), F32)
        return pl.pallas_call(
            _fox_online_kernel,
            grid=(bsz, pairs, nq),
            in_specs=in_specs, out_specs=out_spec, out_shape=out_shape,
            scratch_shapes=[stats, stats, pltpu.VMEM((2, HEAD_DIM, TQ_ONLINE), F32)],
            compiler_params=sem, name="fox_attention_online",
        )(qt, k, f3, vt)

    brow = jnp.full((1, TQ), bound, F32)
    return lax.cond(bound <= FOX_FAST_BOUND_LOG2, fixed, online, brow)


def _even_out_kernel(a_ref, o_ref, x_ref, w_ref, gm_ref, wr_ref, br_ref,
                     x1_ref, xn_ref, comb_ref):
    y_all = _dot(jnp.concatenate([a_ref[...], o_ref[...]], axis=1), w_ref[...])
    sub = x_ref.shape[0] // SUBTILES
    for s in range(SUBTILES):
        rs = slice(s * sub, (s + 1) * sub)
        x1 = x_ref[rs, :] + y_all[rs]
        x1_ref[rs, :] = x1
        xn = _rms_rows(x1, gm_ref[...]).astype(BF16)
        xn_ref[rs, :] = xn
        comb_ref[rs, :] = _router(xn, wr_ref, br_ref)


def _even_out(a, o, x2, w, gm, wr, br):
    t = x2.shape[0]
    n = t // TM
    full = lambda arr: pl.BlockSpec(arr.shape, lambda i: (0,) * arr.ndim)
    row = lambda wdt: pl.BlockSpec((TM, wdt), lambda i: (i, 0))
    return pl.pallas_call(
        _even_out_kernel,
        grid=(n,),
        in_specs=[row(GMLP_WIDTH), row(FOX_WIDTH), row(D_MODEL), full(w), full(gm),
                  full(wr), full(br)],
        out_specs=[row(D_MODEL), row(D_MODEL), row(LANES)],
        out_shape=[jax.ShapeDtypeStruct((t, D_MODEL), F32),
                   jax.ShapeDtypeStruct((t, D_MODEL), BF16),
                   jax.ShapeDtypeStruct((t, LANES), F32)],
        compiler_params=_cparams(("parallel",)),
        name="even_out",
    )(a, o, x2, w, gm, wr, br)


def _odd_kernel(tiles_per_seq, x_ref, g_ref, w_ref, cw_ref, wp_ref, ps_ref, wo_ref, gm_ref,
                wr_ref, br_ref, x1_ref, xn_ref, comb_ref, zbuf, pbuf):
    i = pl.program_id(0)
    tm = x_ref.shape[0]
    zpad = zbuf.shape[0] - tm
    ppad = pbuf.shape[0] - tm
    @pl.when(i % tiles_per_seq == 0)
    def _():
        zbuf[0:zpad, :] = jnp.zeros((zpad, CONV_WIDTH), F32)
        pbuf[0:ppad, :] = jnp.zeros((ppad, POOL_WIDTH), F32)

    cw = cw_ref[...]
    ps = ps_ref[...]
    sub = tm // SUBTILES
    xn_all = _rms_rows(x_ref[...], g_ref[...]).astype(BF16)
    cg_all = _dot(xn_all, w_ref[:, 512:1024])
    hc_all = _dot(xn_all, w_ref[:, 1024:1536])
    p_all = _dot(xn_all, w_ref[:, 1536:2048])
    bg_all = _dot(xn_all, w_ref[:, 0:512])
    for s in range(SUBTILES):
        r0 = s * sub
        rs = slice(r0, r0 + sub)
        x = x_ref[rs, :]
        cg, hc, p, bg = cg_all[rs], hc_all[rs], p_all[rs], bg_all[rs]

        z = cg * hc
        zbuf[zpad + r0:zpad + r0 + sub, :] = z
        y = (cw[0:1, :] * zbuf[zpad + r0 - 2:zpad + r0 - 2 + sub, :]
             + cw[1:2, :] * zbuf[zpad + r0 - 1:zpad + r0 - 1 + sub, :]
             + cw[2:3, :] * z)
        c = bg * y

        pbuf[ppad + r0:ppad + r0 + sub, :] = p
        pos = ((i % tiles_per_seq) * tm + r0
               + lax.broadcasted_iota(jnp.int32, (sub, POOL_GROUP), 0)).astype(F32) + 1.0
        pooled_out = []
        for gi, win in enumerate(POOL_WINDOWS):
            ls = slice(gi * POOL_GROUP, (gi + 1) * POOL_GROUP)
            pg = p[:, ls]
            sm = pg
            for sh in range(1, win):
                sm = sm + pbuf[ppad + r0 - sh:ppad + r0 - sh + sub, ls]
            cnt = jnp.minimum(pos, float(win))
            pooled = sm / cnt - pg
            pooled_out.append((_dot(pooled.astype(BF16), wp_ref[gi]) * ps[:, ls]).astype(BF16))
        mix = jnp.concatenate([c.astype(BF16)] + pooled_out, axis=1)
        acc = x + _dot(mix, wo_ref[...])

        x1_ref[rs, :] = acc
        xn2 = _rms_rows(acc, gm_ref[...]).astype(BF16)
        xn_ref[rs, :] = xn2
        comb_ref[rs, :] = _router(xn2, wr_ref, br_ref)

    zbuf[0:zpad, :] = zbuf[tm:tm + zpad, :]
    pbuf[0:ppad, :] = pbuf[tm:tm + ppad, :]


def _odd(x2, g, w, cw, wp, ps, wo, gm, wr, br, seq):
    t = x2.shape[0]
    n = t // TM
    full = lambda arr: pl.BlockSpec(arr.shape, lambda i: (0,) * arr.ndim)
    row = lambda wdt: pl.BlockSpec((TM, wdt), lambda i: (i, 0))
    return pl.pallas_call(
        functools.partial(_odd_kernel, seq // TM),
        grid=(n,),
        in_specs=[row(D_MODEL), full(g), full(w), full(cw), full(wp), full(ps), full(wo),
                  full(gm), full(wr), full(br)],
        out_specs=[row(D_MODEL), row(D_MODEL), row(LANES)],
        out_shape=[jax.ShapeDtypeStruct((t, D_MODEL), F32),
                   jax.ShapeDtypeStruct((t, D_MODEL), BF16),
                   jax.ShapeDtypeStruct((t, LANES), F32)],
        scratch_shapes=[pltpu.VMEM((TM + 8, CONV_WIDTH), F32),
                        pltpu.VMEM((TM + 16, POOL_WIDTH), F32)],
        compiler_params=_cparams(("arbitrary",)),
        name="odd_mixer",
    )(x2, g, w, cw, wp, ps, wo, gm, wr, br)


def _moe_kernel(xn_ref, wg_ref, wu_ref, wd_ref, comb_ref, x1_ref, o_ref):
    g = pl.program_id(1)

    @pl.when(g == 0)
    def _():
        o_ref[...] = x1_ref[...]

    tm = xn_ref.shape[0]
    xn = xn_ref[...]
    lane = lax.broadcasted_iota(jnp.int32, (tm, LANES), 1)
    comb = comb_ref[...]
    hid = []
    for e in range(EXPERTS_PER_GROUP):
        gate = _dot(xn, wg_ref[e].astype(BF16))
        up = _dot(xn, wu_ref[e].astype(BF16))
        cw = jnp.sum(jnp.where(lane == g * EXPERTS_PER_GROUP + e, comb, 0.0), axis=1,
                     keepdims=True)
        hid.append(((gate * jax.nn.sigmoid(gate)) * up * cw).astype(BF16))
    o_ref[...] += _dot(jnp.concatenate(hid, axis=1), wd_ref[0].astype(BF16))


def _moe(xn, wg, wu, wd, comb, x1, layer):
    t = xn.shape[0]
    n = t // TM_MOE
    gh = EXPERTS_PER_GROUP * EXPERT_HIDDEN
    wblk = pl.BlockSpec((None, EXPERTS_PER_GROUP, D_MODEL, EXPERT_HIDDEN),
                        lambda i, g: (layer, g, 0, 0))
    return pl.pallas_call(
        _moe_kernel,
        grid=(n, N_GROUPS),
        in_specs=[pl.BlockSpec((TM_MOE, D_MODEL), lambda i, g: (i, 0)),
                  wblk, wblk,
                  pl.BlockSpec((None, 1, gh, D_MODEL), lambda i, g: (layer, g, 0, 0)),
                  pl.BlockSpec((TM_MOE, LANES), lambda i, g: (i, 0)),
                  pl.BlockSpec((TM_MOE, D_MODEL), lambda i, g: (i, 0))],
        out_specs=pl.BlockSpec((TM_MOE, D_MODEL), lambda i, g: (i, 0)),
        out_shape=jax.ShapeDtypeStruct((t, D_MODEL), F32),
        compiler_params=_cparams(("parallel", "arbitrary")),
        name="moe",
    )(xn, wg, wu, wd, comb, x1)


def _router_params(w_group, b_group, w_router, b_router):
    wr = jnp.concatenate([w_router.reshape(D_MODEL, N_EXPERTS), w_group], axis=1)
    wr = jnp.pad(wr, ((0, 0), (0, LANES - wr.shape[1]))).astype(BF16)
    br = jnp.concatenate([b_router.reshape(N_EXPERTS), b_group])
    br = jnp.pad(br, (0, LANES - br.shape[0])).reshape(1, LANES).astype(F32)
    return wr, br


def _moe_params(w_gate, w_up, w_down):
    wd = w_down.reshape(w_down.shape[0], N_GROUPS, EXPERTS_PER_GROUP * EXPERT_HIDDEN, D_MODEL)
    return w_gate, w_up, wd


def kernel(x, ev_norm, ev_w_in, ev_b_forget, ev_w_s, ev_b_s, ev_g_v, ev_g_q, ev_g_k, ev_w_out,
           od_norm, od_w_in, od_conv_w, od_w_pool, od_pool_scale, od_w_out,
           moe_norm, moe_w_group, moe_b_group, moe_w_router, moe_b_router,
           moe_w_gate, moe_w_up, moe_w_down):
    bsz, seq, d = x.shape
    t = bsz * seq
    x2 = x.reshape(t, d)

    w_f = ev_w_in[0][:, EVEN_MAIN:]
    w_f3 = jnp.pad(jnp.tile(w_f, (1, 3)), ((0, 0), (0, LANES - 3 * FOX_HEADS))).astype(BF16)
    bf = jnp.pad(jnp.tile(ev_b_forget[0], 3), (0, LANES - 3 * FOX_HEADS)).reshape(1, LANES)
    tril = jnp.tril(jnp.ones((CHUNK, CHUNK), F32))
    wtril = (ev_w_s[0] * tril).astype(BF16)
    bmat = jnp.repeat(ev_b_s[0].T, GMLP_HEAD, axis=1)
    gv = ev_g_v[0].reshape(1, GMLP_WIDTH)
    gq = jnp.broadcast_to((jnp.tile(ev_g_q[0], 2) * (HEAD_DIM ** -0.5 * LOG2E))[:, None],
                          (LANES, TM // SUBTILES))
    gk = jnp.tile(ev_g_k[0], FOX_HEADS).reshape(1, FOX_WIDTH)
    blk = jnp.arange(256) // HEAD_DIM
    bd = jnp.where(blk[:, None] == blk[None, :], 1.0 / HEAD_DIM, 0.0).astype(BF16)
    ltri = jnp.tril(jnp.ones((TM // SUBTILES, TM // SUBTILES), F32)).astype(BF16)

    a, qt, kn, vt, f3, fq = _even_in(x2, ev_norm[0].reshape(1, d), ev_w_in[0].T, w_f3, bf, wtril,
                                     bmat, gv, gq, gk, bd, ltri, seq)
    bound = (HEAD_DIM ** 0.5 * LOG2E * 1.01) * jnp.max(jnp.abs(ev_g_q[0])) * jnp.max(
        jnp.abs(ev_g_k[0]))
    o = _fox(qt, kn, f3, vt, fq, bound, bsz, seq)

    wr0, br0 = _router_params(moe_w_group[0], moe_b_group[0], moe_w_router[0], moe_b_router[0])
    w_out0 = ev_w_out[0].astype(BF16)
    x1, xn1, comb1 = _even_out(a, o, x2, w_out0,
                               moe_norm[0].reshape(1, d), wr0, br0)
    moe_w = _moe_params(moe_w_gate, moe_w_up, moe_w_down)
    xa = _moe(xn1, *moe_w, comb1, x1, 0)

    wr1, br1 = _router_params(moe_w_group[1], moe_b_group[1], moe_w_router[1], moe_b_router[1])
    x3, xn3, comb3 = _odd(xa, od_norm[0].reshape(1, d), od_w_in[0].astype(BF16), od_conv_w[0],
                          od_w_pool[0].astype(BF16), od_pool_scale[0].reshape(1, POOL_WIDTH),
                          od_w_out[0].astype(BF16), moe_norm[1].reshape(1, d), wr1, br1, seq)
    xb = _moe(xn3, *moe_w, comb3, x3, 1)
    return xb.reshape(bsz, seq, d)
```

```python
import functools

import jax
import jax.numpy as jnp
from jax import lax
from jax.experimental import pallas as pl
from jax.experimental.pallas import tpu as pltpu

F32 = jnp.float32
BF16 = jnp.bfloat16

D_MODEL = 1024
EPS = 1e-6
CHUNK = 128
GMLP_GROUPS = 4
GMLP_HEAD = 64
GMLP_WIDTH = 256
FOX_HEADS = 12
HEAD_DIM = 64
FOX_WIDTH = 768
CONV_WIDTH = 512
CONV_K = 3
POOL_WINDOWS = (2, 4, 8, 16)
POOL_GROUP = 128
POOL_WIDTH = 512
N_GROUPS = 4
EXPERTS_PER_GROUP = 4
N_EXPERTS = 16
EXPERT_HIDDEN = 256

LANES = 128
MXU_DIM = 256
V7X_VMEM_LIMIT_BYTES = 56 * 1024 * 1024

TM = 512
TQ = 512
TQ_ONLINE = 256
Z_SLOTS = 2
TM_MOE = 1024
SUBTILES = 2
EVEN_MAIN = 2 * GMLP_WIDTH + 3 * FOX_WIDTH
NEG = -1e30
LOG2E = 1.4426950408889634
FOX_FAST_BOUND_LOG2 = 50.0


def _cparams(sem):
    return pltpu.CompilerParams(dimension_semantics=sem,
                                vmem_limit_bytes=V7X_VMEM_LIMIT_BYTES)


def _dot(a, b):
    return jnp.dot(a, b, preferred_element_type=F32)


def _split_bf16(x, terms):
    parts = []
    r = x
    for _ in range(terms):
        p = r.astype(BF16)
        parts.append(p)
        r = r - p.astype(F32)
    return parts


def _dot_split(x, w_bf16, terms=2, w_left=False):
    acc = None
    for p in _split_bf16(x, terms):
        d = _dot(w_bf16, p) if w_left else _dot(p, w_bf16)
        acc = d if acc is None else acc + d
    return acc


def _rms_rows(x, g):
    ms = jnp.mean(x * x, axis=-1, keepdims=True)
    return x * lax.rsqrt(ms + EPS) * g


def _head_rms(x, bd, g):
    outs = []
    for c in range(x.shape[1] // MXU_DIM):
        xc = x[:, c * MXU_DIM:(c + 1) * MXU_DIM]
        ms = _dot_split(xc * xc, bd)
        outs.append(xc * lax.rsqrt(ms + EPS))
    y = outs[0] if len(outs) == 1 else jnp.concatenate(outs, axis=1)
    return y * g


def _log_sigmoid(x):
    return -(jnp.maximum(-x, 0.0) + jnp.log(1.0 + jnp.exp(-jnp.abs(x))))


def _router(xn_bf16, wr_ref, br_ref):
    r = _dot(xn_bf16, wr_ref[...]) + br_ref[...]
    tm = r.shape[0]
    rt = r.T
    er = rt[0:N_EXPERTS]
    gr = rt[N_EXPERTS:N_EXPERTS + 8]
    grow = lax.broadcasted_iota(jnp.int32, (8, tm), 0).astype(F32)
    erow = lax.broadcasted_iota(jnp.int32, (N_EXPERTS, tm), 0).astype(F32)
    is_g = grow < float(N_GROUPS)
    gl = jnp.where(is_g, gr, NEG)
    gmax = jnp.max(gl, axis=0, keepdims=True)
    gidx = jnp.min(jnp.where(gl == gmax, grow, 999.0), axis=0, keepdims=True)
    gsum = jnp.sum(jnp.where(is_g, jnp.exp(gl - gmax), 0.0), axis=0, keepdims=True)
    gp = 1.0 / gsum
    lo = gidx * float(EXPERTS_PER_GROUP)
    sel = (erow >= lo) & (erow < lo + float(EXPERTS_PER_GROUP))
    el = jnp.where(sel, er, NEG)
    emax = jnp.max(el, axis=0, keepdims=True)
    ee = jnp.where(sel, jnp.exp(el - emax), 0.0)
    ep = ee / jnp.sum(ee, axis=0, keepdims=True)
    ep = jnp.where(sel, ep, -1.0)
    p1 = jnp.max(ep, axis=0, keepdims=True)
    i1 = jnp.min(jnp.where(ep == p1, erow, 999.0), axis=0, keepdims=True)
    ep2 = jnp.where(erow == i1, -1.0, ep)
    p2 = jnp.max(ep2, axis=0, keepdims=True)
    i2 = jnp.min(jnp.where(ep2 == p2, erow, 999.0), axis=0, keepdims=True)
    den = p1 + p2
    comb_t = (jnp.where(erow == i1, gp * (p1 / den), 0.0)
              + jnp.where(erow == i2, gp * (p2 / den), 0.0))
    comb_t = jnp.concatenate([comb_t, jnp.zeros((LANES - N_EXPERTS, tm), F32)], axis=0)
    return comb_t.T


def _even_in_kernel(tiles_per_seq, x_ref, g_ref, w_ref, wf_ref, bf_ref, wtril_ref, bmat_ref,
                    gv_ref, gq_ref, gk_ref, bd_ref, ltri_ref,
                    a_ref, q_ref, k_ref, v_ref, f_ref, fq_ref, wbf_ref, carry_ref):
    i = pl.program_id(0)
    tm = x_ref.shape[0]

    @pl.when(i == 0)
    def _():
        for c in range(0, w_ref.shape[0], MXU_DIM):
            wbf_ref[:, c:c + MXU_DIM] = w_ref[c:c + MXU_DIM, :].T.astype(BF16)

    @pl.when(i % tiles_per_seq == 0)
    def _():
        carry_ref[...] = jnp.zeros_like(carry_ref)

    sub = tm // SUBTILES
    lane = lax.broadcasted_iota(jnp.int32, (sub, LANES), 1)
    lo_half = lane < GMLP_HEAD
    bmat = bmat_ref[...]
    bd = bd_ref[...]
    xn_all = _rms_rows(x_ref[...], g_ref[...]).astype(BF16)
    u_all = _dot(xn_all, wbf_ref[:, 0:256])
    v_all = _dot(xn_all, wbf_ref[:, 256:512])
    q_all = _dot(xn_all, wbf_ref[:, 512:1280])
    k_all = _dot(xn_all, wbf_ref[:, 1280:2048])
    val_all = _dot(xn_all, wbf_ref[:, 2048:2816])
    f_all = _dot(xn_all, wf_ref[...])
    for st in range(SUBTILES):
        rows = slice(st * sub, (st + 1) * sub)
        u, v, q, k = u_all[rows], v_all[rows], q_all[rows], k_all[rows]
        val, f = val_all[rows], f_all[rows]

        gu = jax.nn.gelu(u)
        vn = _head_rms(jax.nn.gelu(v), bd, gv_ref[...])
        pair_out = []
        for pr in range(2):
            vp = vn[:, pr * 128:(pr + 1) * 128]
            v_lo = jnp.where(lo_half, vp, 0.0).astype(BF16)
            v_hi = jnp.where(lo_half, 0.0, vp).astype(BF16)
            chunks = []
            for c in range(sub // CHUNK):
                rs = slice(c * CHUNK, (c + 1) * CHUNK)
                s = (_dot(wtril_ref[2 * pr], v_lo[rs]) + _dot(wtril_ref[2 * pr + 1], v_hi[rs])
                     + bmat[:, pr * 128:(pr + 1) * 128])
                chunks.append(s)
            pair_out.append(jnp.concatenate(chunks, axis=0))
        s_all = jnp.concatenate(pair_out, axis=1)
        a_ref[rows, :] = (gu * s_all).astype(BF16)

        for pr in range(FOX_HEADS // 2):
            ls = slice(pr * LANES, (pr + 1) * LANES)
            qt = q[:, ls].T
            heads = []
            for hh in range(2):
                xh = qt[hh * HEAD_DIM:(hh + 1) * HEAD_DIM]
                ms = jnp.sum(xh * xh, axis=0, keepdims=True) * (1.0 / HEAD_DIM)
                heads.append(xh * lax.rsqrt(ms + EPS))
            q_ref[pr, :, rows] = (jnp.concatenate(heads, axis=0) * gq_ref[...]).astype(BF16)
            v_ref[pr, :, rows] = val[:, ls].T.astype(BF16)
        k_ref[rows, :] = _head_rms(k, bd, gk_ref[...]).astype(BF16)

        logf = _log_sigmoid(f + bf_ref[...])
        cum = (_dot_split(logf, ltri_ref[...], terms=3, w_left=True)
               + carry_ref[0:1, :])
        carry_ref[0:1, :] = cum[sub - 1:sub, :]
        cum2 = cum * LOG2E
        fq_ref[rows, :] = cum2
        hi, mid, lo = (p.astype(F32) for p in _split_bf16(cum2, 3))
        f_ref[rows, :] = jnp.where(
            lane < FOX_HEADS, hi,
            jnp.where(lane < 2 * FOX_HEADS, mid,
                      jnp.where(lane < 3 * FOX_HEADS, lo,
                                jnp.where(lane < 3 * FOX_HEADS + 3, 1.0, 0.0)))).astype(BF16)


def _even_in(x2, g, w_all, wf, bf, wtril, bmat, gv, gq, gk, bd, ltri, seq):
    t = x2.shape[0]
    n = t // TM
    full = lambda a: pl.BlockSpec(a.shape, lambda i: (0,) * a.ndim)
    row = lambda wdt: pl.BlockSpec((TM, wdt), lambda i: (i, 0))
    pairs = FOX_HEADS // 2
    colT = pl.BlockSpec((pairs, LANES, TM), lambda i: (0, 0, i))
    w_main = pl.BlockSpec((EVEN_MAIN, D_MODEL), lambda i: (0, 0), pipeline_mode=pl.Buffered(1))
    return pl.pallas_call(
        functools.partial(_even_in_kernel, seq // TM),
        grid=(n,),
        in_specs=[row(D_MODEL), full(g), w_main, full(wf), full(bf), full(wtril), full(bmat),
                  full(gv), full(gq), full(gk), full(bd), full(ltri)],
        out_specs=[row(GMLP_WIDTH), colT, row(FOX_WIDTH), colT, row(LANES), row(LANES)],
        out_shape=[jax.ShapeDtypeStruct((t, GMLP_WIDTH), BF16),
                   jax.ShapeDtypeStruct((pairs, LANES, t), BF16),
                   jax.ShapeDtypeStruct((t, FOX_WIDTH), BF16),
                   jax.ShapeDtypeStruct((pairs, LANES, t), BF16),
                   jax.ShapeDtypeStruct((t, LANES), BF16),
                   jax.ShapeDtypeStruct((t, LANES), F32)],
        scratch_shapes=[pltpu.VMEM((D_MODEL, EVEN_MAIN), BF16), pltpu.VMEM((8, LANES), F32)],
        compiler_params=_cparams(("arbitrary",)),
        name="even_in",
    )(x2, g, w_all, wf, bf, wtril, bmat, gv, gq, gk, bd, ltri)


def _fox_query_operand(qt, pr, hh, shift_terms=None):
    tq = qt.shape[1]
    row = lax.broadcasted_iota(jnp.int32, (LANES, tq), 0)
    head = 2 * pr + hh
    in_head = (row >= hh * HEAD_DIM) & (row < (hh + 1) * HEAD_DIM)
    f_rows = (row == head) | (row == head + FOX_HEADS) | (row == head + 2 * FOX_HEADS)
    extra = jnp.where(f_rows, -1.0, 0.0)
    if shift_terms is not None:
        for n, term in enumerate(shift_terms):
            extra = jnp.where(row == 3 * FOX_HEADS + n, term, extra)
    return jnp.concatenate([jnp.where(in_head, qt, 0.0).astype(BF16), extra.astype(BF16)],
                           axis=0)


def _fox_online_kernel(q_ref, k_ref, f_ref, v_ref, o_ref, m_sc, l_sc, acc_sc):
    pr = pl.program_id(1)
    i = pl.program_id(2)
    tq = q_ref.shape[2]
    qt = q_ref[0].astype(F32)
    rhs = [_fox_query_operand(qt, pr, hh) for hh in range(2)]

    m_sc[...] = jnp.full_like(m_sc, NEG)
    l_sc[...] = jnp.zeros_like(l_sc)
    acc_sc[...] = jnp.zeros_like(acc_sc)

    def step(j, masked):
        start = pl.multiple_of(j * tq, tq)
        kaug = jnp.concatenate([k_ref[pl.ds(start, tq), :], f_ref[pl.ds(start, tq), :]],
                               axis=1)
        for hh in range(2):
            s = _dot(kaug, rhs[hh])
            if masked:
                r_id = lax.broadcasted_iota(jnp.int32, (tq, tq), 0)
                c_id = lax.broadcasted_iota(jnp.int32, (tq, tq), 1)
                s = jnp.where(r_id <= c_id, s, NEG)
            m_prev = m_sc[hh]
            m_new = jnp.maximum(m_prev, jnp.max(s, axis=0, keepdims=True))
            alpha = jnp.exp2(m_prev - m_new)
            p = jnp.exp2(s - m_new)
            l_sc[hh] = alpha * l_sc[hh] + jnp.sum(p, axis=0, keepdims=True)
            vt = v_ref[0, hh * HEAD_DIM:(hh + 1) * HEAD_DIM, pl.ds(start, tq)]
            acc_sc[hh] = alpha * acc_sc[hh] + _dot(vt, p.astype(BF16))
            m_sc[hh] = m_new

    def body(j, c):
        step(j, False)
        return c

    lax.fori_loop(0, i, body, 0)
    step(i, True)
    ot = jnp.concatenate([acc_sc[0] / l_sc[0], acc_sc[1] / l_sc[1]], axis=0)
    o_ref[...] = ot.T.astype(o_ref.dtype)


def _fox_fixed_kernel(b_ref, q_ref, k_ref, f_ref, v_ref, fq_ref, o_ref,
                      rhs_sc, z_sc, acc_sc, mask_sc):
    pr = pl.program_id(1)
    tq = z_sc.shape[3]
    tk = z_sc.shape[2]
    nq = rhs_sc.shape[0]
    row = lax.broadcasted_iota(jnp.int32, (LANES, tq), 0)
    for i in range(nq):
        qt = q_ref[0, :, i * tq:(i + 1) * tq].astype(F32)
        fqt = fq_ref[i * tq:(i + 1) * tq, :].T
        for hh in range(2):
            ft = jnp.sum(jnp.where(row == 2 * pr + hh, fqt, 0.0), axis=0, keepdims=True)
            shift = [p.astype(F32) for p in _split_bf16(ft - b_ref[...], 3)]
            rhs_sc[i, hh] = _fox_query_operand(qt, pr, hh, shift)

    mask_sc[...] = jnp.where(lax.broadcasted_iota(jnp.int32, (tk, tq), 0)
                             > lax.broadcasted_iota(jnp.int32, (tk, tq), 1), NEG, 0.0)
    acc_sc[...] = jnp.zeros_like(acc_sc)
    ones = jnp.ones((acc_sc.shape[2] - HEAD_DIM, tk), BF16)

    half = tk // 2

    def scores(i, j, slot):
        rows = slice(j * tk, (j + 1) * tk)
        kaug = jnp.concatenate([k_ref[rows, :], f_ref[rows, :]], axis=1)
        for hh in range(2):
            if i == j:
                z_sc[slot, hh, 0:half, :] = _dot(kaug[0:half], rhs_sc[i, hh])
                z_sc[slot, hh, half:tk, half:tq] = _dot(kaug[half:tk], rhs_sc[i, hh, :, half:tq])
            else:
                z_sc[slot, hh] = _dot(kaug, rhs_sc[i, hh])

    def values(hh, start, size):
        return jnp.concatenate(
            [v_ref[0, hh * HEAD_DIM:(hh + 1) * HEAD_DIM, start:start + size], ones[:, 0:size]],
            axis=0)

    def consume(i, j, slot):
        for hh in range(2):
            p = jnp.exp2(z_sc[slot, hh]).astype(BF16)
            acc_sc[i, hh] += _dot(values(hh, j * tk, tk), p)

    def consume_diagonal(i, slot):
        outs = []
        for hh in range(2):
            p_top = jnp.exp2(z_sc[slot, hh, 0:half, :] + mask_sc[0:half, :]).astype(BF16)
            p_bot = jnp.exp2(z_sc[slot, hh, half:tk, half:tq]
                             + mask_sc[half:tk, half:tq]).astype(BF16)
            acc = acc_sc[i, hh] + _dot(values(hh, i * tk, half), p_top)
            right = acc[:, half:tq] + _dot(values(hh, i * tk + half, half), p_bot)
            acc = jnp.concatenate([acc[:, 0:half], right], axis=1)
            outs.append(acc[0:HEAD_DIM] / acc[HEAD_DIM:HEAD_DIM + 1])
        o_ref[i * tq:(i + 1) * tq, :] = jnp.concatenate(outs, axis=0).T.astype(o_ref.dtype)

    items = [(i, j) for i in range(1, nq) for j in range(i)] + [(i, i) for i in range(nq)]
    scores(*items[0], 0)
    for s, (i, j) in enumerate(items):
        if s + 1 < len(items):
            scores(*items[s + 1], (s + 1) % 2)
        if i == j:
            consume_diagonal(i, s % 2)
        else:
            consume(i, j, s % 2)


def _fox(qt, k, f3, vt, fq, bound, bsz, seq):
    t = k.shape[0]
    pairs = FOX_HEADS // 2
    out_shape = jax.ShapeDtypeStruct((t, FOX_WIDTH), BF16)
    sem = _cparams(("parallel", "parallel", "arbitrary"))

    def common(tq):
        nq = seq // tq
        in_specs = [pl.BlockSpec((1, LANES, tq), lambda b, p, i: (p, 0, b * nq + i)),
                    pl.BlockSpec((seq, LANES), lambda b, p, i: (b, p)),
                    pl.BlockSpec((seq, LANES), lambda b, p, i: (b, 0)),
                    pl.BlockSpec((1, LANES, seq), lambda b, p, i: (p, 0, b))]
        return nq, in_specs, pl.BlockSpec((tq, LANES), lambda b, p, i: (b * nq + i, p))

    def fixed(brow):
        nq = seq // TQ
        return pl.pallas_call(
            _fox_fixed_kernel,
            grid=(bsz, pairs),
            in_specs=[pl.BlockSpec((1, TQ), lambda b, p: (0, 0)),
                      pl.BlockSpec((1, LANES, seq), lambda b, p: (p, 0, b)),
                      pl.BlockSpec((seq, LANES), lambda b, p: (b, p)),
                      pl.BlockSpec((seq, LANES), lambda b, p: (b, 0)),
                      pl.BlockSpec((1, LANES, seq), lambda b, p: (p, 0, b)),
                      pl.BlockSpec((seq, LANES), lambda b, p: (b, 0))],
            out_specs=pl.BlockSpec((seq, LANES), lambda b, p: (b, p)),
            out_shape=out_shape,
            scratch_shapes=[pltpu.VMEM((nq, 2, 2 * LANES, TQ), BF16),
                            pltpu.VMEM((Z_SLOTS, 2, TQ, TQ), F32),
                            pltpu.VMEM((nq, 2, HEAD_DIM + 16, TQ), F32),
                            pltpu.VMEM((TQ, TQ), F32)],
            compiler_params=_cparams(("parallel", "parallel")), name="fox_attention_fixed",
        )(brow, qt, k, f3, vt, fq)

    def online(brow):
        del brow
        nq, in_specs, out_spec = common(TQ_ONLINE)
        stats = pltpu.VMEM((2, 1, TQ_ONLINE), F32)
        return pl.pallas_call(
            _fox_online_kernel,
            grid=(bsz, pairs, nq),
            in_specs=in_specs, out_specs=out_spec, out_shape=out_shape,
            scratch_shapes=[stats, stats, pltpu.VMEM((2, HEAD_DIM, TQ_ONLINE), F32)],
            compiler_params=sem, name="fox_attention_online",
        )(qt, k, f3, vt)

    brow = jnp.full((1, TQ), bound, F32)
    return lax.cond(bound <= FOX_FAST_BOUND_LOG2, fixed, online, brow)


def _even_out_kernel(a_ref, o_ref, x_ref, w_ref, gm_ref, wr_ref, br_ref,
                     x1_ref, xn_ref, comb_ref):
    y_all = _dot(jnp.concatenate([a_ref[...], o_ref[...]], axis=1), w_ref[...])
    sub = x_ref.shape[0] // SUBTILES
    for s in range(SUBTILES):
        rs = slice(s * sub, (s + 1) * sub)
        x1 = x_ref[rs, :] + y_all[rs]
        x1_ref[rs, :] = x1
        xn = _rms_rows(x1, gm_ref[...]).astype(BF16)
        xn_ref[rs, :] = xn
        comb_ref[rs, :] = _router(xn, wr_ref, br_ref)


def _even_out(a, o, x2, w, gm, wr, br):
    t = x2.shape[0]
    n = t // TM
    full = lambda arr: pl.BlockSpec(arr.shape, lambda i: (0,) * arr.ndim)
    row = lambda wdt: pl.BlockSpec((TM, wdt), lambda i: (i, 0))
    return pl.pallas_call(
        _even_out_kernel,
        grid=(n,),
        in_specs=[row(GMLP_WIDTH), row(FOX_WIDTH), row(D_MODEL), full(w), full(gm),
                  full(wr), full(br)],
        out_specs=[row(D_MODEL), row(D_MODEL), row(LANES)],
        out_shape=[jax.ShapeDtypeStruct((t, D_MODEL), F32),
                   jax.ShapeDtypeStruct((t, D_MODEL), BF16),
                   jax.ShapeDtypeStruct((t, LANES), F32)],
        compiler_params=_cparams(("parallel",)),
        name="even_out",
    )(a, o, x2, w, gm, wr, br)


def _odd_kernel(tiles_per_seq, x_ref, g_ref, w_ref, cw_ref, wp_ref, ps_ref, wo_ref, gm_ref,
                wr_ref, br_ref, x1_ref, xn_ref, comb_ref, zbuf, pbuf):
    i = pl.program_id(0)
    tm = x_ref.shape[0]
    zpad = zbuf.shape[0] - tm
    ppad = pbuf.shape[0] - tm
    @pl.when(i % tiles_per_seq == 0)
    def _():
        zbuf[0:zpad, :] = jnp.zeros((zpad, CONV_WIDTH), F32)
        pbuf[0:ppad, :] = jnp.zeros((ppad, POOL_WIDTH), F32)

    cw = cw_ref[...]
    ps = ps_ref[...]
    sub = tm // SUBTILES
    xn_all = _rms_rows(x_ref[...], g_ref[...]).astype(BF16)
    cg_all = _dot(xn_all, w_ref[:, 512:1024])
    hc_all = _dot(xn_all, w_ref[:, 1024:1536])
    p_all = _dot(xn_all, w_ref[:, 1536:2048])
    bg_all = _dot(xn_all, w_ref[:, 0:512])
    for s in range(SUBTILES):
        r0 = s * sub
        rs = slice(r0, r0 + sub)
        x = x_ref[rs, :]
        cg, hc, p, bg = cg_all[rs], hc_all[rs], p_all[rs], bg_all[rs]

        z = cg * hc
        zbuf[zpad + r0:zpad + r0 + sub, :] = z
        y = (cw[0:1, :] * zbuf[zpad + r0 - 2:zpad + r0 - 2 + sub, :]
             + cw[1:2, :] * zbuf[zpad + r0 - 1:zpad + r0 - 1 + sub, :]
             + cw[2:3, :] * z)
        c = bg * y

        pbuf[ppad + r0:ppad + r0 + sub, :] = p
        pos = ((i % tiles_per_seq) * tm + r0
               + lax.broadcasted_iota(jnp.int32, (sub, POOL_GROUP), 0)).astype(F32) + 1.0
        pooled_out = []
        for gi, win in enumerate(POOL_WINDOWS):
            ls = slice(gi * POOL_GROUP, (gi + 1) * POOL_GROUP)
            pg = p[:, ls]
            sm = pg
            for sh in range(1, win):
                sm = sm + pbuf[ppad + r0 - sh:ppad + r0 - sh + sub, ls]
            cnt = jnp.minimum(pos, float(win))
            pooled = sm / cnt - pg
            pooled_out.append((_dot(pooled.astype(BF16), wp_ref[gi]) * ps[:, ls]).astype(BF16))
        mix = jnp.concatenate([c.astype(BF16)] + pooled_out, axis=1)
        acc = x + _dot(mix, wo_ref[...])

        x1_ref[rs, :] = acc
        xn2 = _rms_rows(acc, gm_ref[...]).astype(BF16)
        xn_ref[rs, :] = xn2
        comb_ref[rs, :] = _router(xn2, wr_ref, br_ref)

    zbuf[0:zpad, :] = zbuf[tm:tm + zpad, :]
    pbuf[0:ppad, :] = pbuf[tm:tm + ppad, :]


def _odd(x2, g, w, cw, wp, ps, wo, gm, wr, br, seq):
    t = x2.shape[0]
    n = t // TM
    full = lambda arr: pl.BlockSpec(arr.shape, lambda i: (0,) * arr.ndim)
    row = lambda wdt: pl.BlockSpec((TM, wdt), lambda i: (i, 0))
    return pl.pallas_call(
        functools.partial(_odd_kernel, seq // TM),
        grid=(n,),
        in_specs=[row(D_MODEL), full(g), full(w), full(cw), full(wp), full(ps), full(wo),
                  full(gm), full(wr), full(br)],
        out_specs=[row(D_MODEL), row(D_MODEL), row(LANES)],
        out_shape=[jax.ShapeDtypeStruct((t, D_MODEL), F32),
                   jax.ShapeDtypeStruct((t, D_MODEL), BF16),
                   jax.ShapeDtypeStruct((t, LANES), F32)],
        scratch_shapes=[pltpu.VMEM((TM + 8, CONV_WIDTH), F32),
                        pltpu.VMEM((TM + 16, POOL_WIDTH), F32)],
        compiler_params=_cparams(("arbitrary",)),
        name="odd_mixer",
    )(x2, g, w, cw, wp, ps, wo, gm, wr, br)


def _moe_kernel(xn_ref, wg_ref, wu_ref, wd_ref, comb_ref, x1_ref, o_ref):
    g = pl.program_id(1)

    @pl.when(g == 0)
    def _():
        o_ref[...] = x1_ref[...]

    tm = xn_ref.shape[0]
    xn = xn_ref[...]
    lane = lax.broadcasted_iota(jnp.int32, (tm, LANES), 1)
    comb = comb_ref[...]
    hid = []
    for e in range(EXPERTS_PER_GROUP):
        gate = _dot(xn, wg_ref[e].astype(BF16))
        up = _dot(xn, wu_ref[e].astype(BF16))
        cw = jnp.sum(jnp.where(lane == g * EXPERTS_PER_GROUP + e, comb, 0.0), axis=1,
                     keepdims=True)
        hid.append(((gate * jax.nn.sigmoid(gate)) * up * cw).astype(BF16))
    o_ref[...] += _dot(jnp.concatenate(hid, axis=1), wd_ref[0].astype(BF16))


def _moe(xn, wg, wu, wd, comb, x1, layer):
    t = xn.shape[0]
    n = t // TM_MOE
    gh = EXPERTS_PER_GROUP * EXPERT_HIDDEN
    wblk = pl.BlockSpec((None, EXPERTS_PER_GROUP, D_MODEL, EXPERT_HIDDEN),
                        lambda i, g: (layer, g, 0, 0))
    return pl.pallas_call(
        _moe_kernel,
        grid=(n, N_GROUPS),
        in_specs=[pl.BlockSpec((TM_MOE, D_MODEL), lambda i, g: (i, 0)),
                  wblk, wblk,
                  pl.BlockSpec((None, 1, gh, D_MODEL), lambda i, g: (layer, g, 0, 0)),
                  pl.BlockSpec((TM_MOE, LANES), lambda i, g: (i, 0)),
                  pl.BlockSpec((TM_MOE, D_MODEL), lambda i, g: (i, 0))],
        out_specs=pl.BlockSpec((TM_MOE, D_MODEL), lambda i, g: (i, 0)),
        out_shape=jax.ShapeDtypeStruct((t, D_MODEL), F32),
        compiler_params=_cparams(("parallel", "arbitrary")),
        name="moe",
    )(xn, wg, wu, wd, comb, x1)


def _router_params(w_group, b_group, w_router, b_router):
    wr = jnp.concatenate([w_router.reshape(D_MODEL, N_EXPERTS), w_group], axis=1)
    wr = jnp.pad(wr, ((0, 0), (0, LANES - wr.shape[1]))).astype(BF16)
    br = jnp.concatenate([b_router.reshape(N_EXPERTS), b_group])
    br = jnp.pad(br, (0, LANES - br.shape[0])).reshape(1, LANES).astype(F32)
    return wr, br


def _moe_params(w_gate, w_up, w_down):
    wd = w_down.reshape(w_down.shape[0], N_GROUPS, EXPERTS_PER_GROUP * EXPERT_HIDDEN, D_MODEL)
    return w_gate, w_up, wd


def kernel(x, ev_norm, ev_w_in, ev_b_forget, ev_w_s, ev_b_s, ev_g_v, ev_g_q, ev_g_k, ev_w_out,
           od_norm, od_w_in, od_conv_w, od_w_pool, od_pool_scale, od_w_out,
           moe_norm, moe_w_group, moe_b_group, moe_w_router, moe_b_router,
           moe_w_gate, moe_w_up, moe_w_down):
    bsz, seq, d = x.shape
    t = bsz * seq
    x2 = x.reshape(t, d)

    w_in_t = ev_w_in[0].T
    w_f = w_in_t[EVEN_MAIN:].T
    w_f3 = jnp.pad(jnp.tile(w_f, (1, 3)), ((0, 0), (0, LANES - 3 * FOX_HEADS))).astype(BF16)
    bf = jnp.pad(jnp.tile(ev_b_forget[0], 3), (0, LANES - 3 * FOX_HEADS)).reshape(1, LANES)
    tril = jnp.tril(jnp.ones((CHUNK, CHUNK), F32))
    wtril = (ev_w_s[0] * tril).astype(BF16)
    bmat = jnp.repeat(ev_b_s[0].T, GMLP_HEAD, axis=1)
    gv = ev_g_v[0].reshape(1, GMLP_WIDTH)
    gq = jnp.broadcast_to((jnp.tile(ev_g_q[0], 2) * (HEAD_DIM ** -0.5 * LOG2E))[:, None],
                          (LANES, TM // SUBTILES))
    gk = jnp.tile(ev_g_k[0], FOX_HEADS).reshape(1, FOX_WIDTH)
    blk = jnp.arange(MXU_DIM) // HEAD_DIM
    bd = jnp.where(blk[:, None] == blk[None, :], 1.0 / HEAD_DIM, 0.0).astype(BF16)
    ltri = jnp.tril(jnp.ones((TM // SUBTILES, TM // SUBTILES), F32)).astype(BF16)

    a, qt, kn, vt, f3, fq = _even_in(x2, ev_norm[0].reshape(1, d), w_in_t, w_f3, bf, wtril,
                                     bmat, gv, gq, gk, bd, ltri, seq)
    bound = (HEAD_DIM ** 0.5 * LOG2E * 1.01) * jnp.max(jnp.abs(ev_g_q[0])) * jnp.max(
        jnp.abs(ev_g_k[0]))
    o = _fox(qt, kn, f3, vt, fq, bound, bsz, seq)

    wr0, br0 = _router_params(moe_w_group[0], moe_b_group[0], moe_w_router[0], moe_b_router[0])
    w_out0 = ev_w_out[0].astype(BF16)
    x1, xn1, comb1 = _even_out(a, o, x2, w_out0,
                               moe_norm[0].reshape(1, d), wr0, br0)
    moe_w = _moe_params(moe_w_gate, moe_w_up, moe_w_down)
    xa = _moe(xn1, *moe_w, comb1, x1, 0)

    wr1, br1 = _router_params(moe_w_group[1], moe_b_group[1], moe_w_router[1], moe_b_router[1])
    x3, xn3, comb3 = _odd(xa, od_norm[0].reshape(1, d), od_w_in[0].astype(BF16), od_conv_w[0],
                          od_w_pool[0].astype(BF16), od_pool_scale[0].reshape(1, POOL_WIDTH),
                          od_w_out[0].astype(BF16), moe_norm[1].reshape(1, d), wr1, br1, seq)
    xb = _moe(xn3, *moe_w, comb3, x3, 1)
    return xb.reshape(bsz, seq, d)
```

```python
import functools

import jax
import jax.numpy as jnp
from jax import lax
from jax.experimental import pallas as pl
from jax.experimental.pallas import tpu as pltpu

F32 = jnp.float32
BF16 = jnp.bfloat16

D_MODEL = 1024
EPS = 1e-6
CHUNK = 128
GMLP_GROUPS = 4
GMLP_HEAD = 64
GMLP_WIDTH = 256
FOX_HEADS = 12
HEAD_DIM = 64
FOX_WIDTH = 768
CONV_WIDTH = 512
CONV_K = 3
POOL_WINDOWS = (2, 4, 8, 16)
POOL_GROUP = 128
POOL_WIDTH = 512
N_GROUPS = 4
EXPERTS_PER_GROUP = 4
N_EXPERTS = 16
EXPERT_HIDDEN = 256

LANES = 128
MXU_DIM = 256
V7X_VMEM_LIMIT_BYTES = 56 * 1024 * 1024

TM = 512
TQ = 512
TQ_ONLINE = 256
Z_SLOTS = 2
TM_MOE = 1024
SUBTILES = 2
EVEN_MAIN = 2 * GMLP_WIDTH + 3 * FOX_WIDTH
NEG = -1e30
LOG2E = 1.4426950408889634
FOX_FAST_BOUND_LOG2 = 50.0


def _cparams(sem):
    return pltpu.CompilerParams(dimension_semantics=sem,
                                vmem_limit_bytes=V7X_VMEM_LIMIT_BYTES)


def _dot(a, b):
    return jnp.dot(a, b, preferred_element_type=F32)


def _split_bf16(x, terms):
    parts = []
    r = x
    for _ in range(terms):
        p = r.astype(BF16)
        parts.append(p)
        r = r - p.astype(F32)
    return parts


def _dot_split(x, w_bf16, terms=2, w_left=False):
    acc = None
    for p in _split_bf16(x, terms):
        d = _dot(w_bf16, p) if w_left else _dot(p, w_bf16)
        acc = d if acc is None else acc + d
    return acc


def _rms_rows(x, g):
    ms = jnp.mean(x * x, axis=-1, keepdims=True)
    return x * lax.rsqrt(ms + EPS) * g


def _head_rms(x, bd, g):
    outs = []
    for c in range(x.shape[1] // MXU_DIM):
        xc = x[:, c * MXU_DIM:(c + 1) * MXU_DIM]
        ms = _dot_split(xc * xc, bd)
        outs.append(xc * lax.rsqrt(ms + EPS))
    y = outs[0] if len(outs) == 1 else jnp.concatenate(outs, axis=1)
    return y * g


def _log_sigmoid(x):
    return -(jnp.maximum(-x, 0.0) + jnp.log(1.0 + jnp.exp(-jnp.abs(x))))


def _router(xn_bf16, wr_ref, br_ref):
    r = _dot(xn_bf16, wr_ref[...]) + br_ref[...]
    tm = r.shape[0]
    rt = r.T
    er = rt[0:N_EXPERTS]
    gr = rt[N_EXPERTS:N_EXPERTS + 8]
    grow = lax.broadcasted_iota(jnp.int32, (8, tm), 0).astype(F32)
    erow = lax.broadcasted_iota(jnp.int32, (N_EXPERTS, tm), 0).astype(F32)
    is_g = grow < float(N_GROUPS)
    gl = jnp.where(is_g, gr, NEG)
    gmax = jnp.max(gl, axis=0, keepdims=True)
    gidx = jnp.min(jnp.where(gl == gmax, grow, 999.0), axis=0, keepdims=True)
    gsum = jnp.sum(jnp.where(is_g, jnp.exp(gl - gmax), 0.0), axis=0, keepdims=True)
    gp = 1.0 / gsum
    lo = gidx * float(EXPERTS_PER_GROUP)
    sel = (erow >= lo) & (erow < lo + float(EXPERTS_PER_GROUP))
    el = jnp.where(sel, er, NEG)
    emax = jnp.max(el, axis=0, keepdims=True)
    ee = jnp.where(sel, jnp.exp(el - emax), 0.0)
    ep = ee / jnp.sum(ee, axis=0, keepdims=True)
    ep = jnp.where(sel, ep, -1.0)
    p1 = jnp.max(ep, axis=0, keepdims=True)
    i1 = jnp.min(jnp.where(ep == p1, erow, 999.0), axis=0, keepdims=True)
    ep2 = jnp.where(erow == i1, -1.0, ep)
    p2 = jnp.max(ep2, axis=0, keepdims=True)
    i2 = jnp.min(jnp.where(ep2 == p2, erow, 999.0), axis=0, keepdims=True)
    den = p1 + p2
    comb_t = (jnp.where(erow == i1, gp * (p1 / den), 0.0)
              + jnp.where(erow == i2, gp * (p2 / den), 0.0))
    comb_t = jnp.concatenate([comb_t, jnp.zeros((LANES - N_EXPERTS, tm), F32)], axis=0)
    return comb_t.T


def _even_in_kernel(tiles_per_seq, x_ref, g_ref, w_ref, wf_ref, bf_ref, wtril_ref, bmat_ref,
                    gv_ref, gq_ref, gk_ref, bd_ref, ltri_ref,
                    a_ref, q_ref, k_ref, v_ref, f_ref, fq_ref, wbf_ref, carry_ref):
    i = pl.program_id(0)
    tm = x_ref.shape[0]

    @pl.when(i == 0)
    def _():
        for c in range(0, w_ref.shape[0], MXU_DIM):
            wbf_ref[:, c:c + MXU_DIM] = w_ref[c:c + MXU_DIM, :].T.astype(BF16)

    @pl.when(i % tiles_per_seq == 0)
    def _():
        carry_ref[...] = jnp.zeros_like(carry_ref)

    sub = tm // SUBTILES
    lane = lax.broadcasted_iota(jnp.int32, (sub, LANES), 1)
    lo_half = lane < GMLP_HEAD
    bmat = bmat_ref[...]
    bd = bd_ref[...]
    xn_all = _rms_rows(x_ref[...], g_ref[...]).astype(BF16)
    u_all = _dot(xn_all, wbf_ref[:, 0:256])
    v_all = _dot(xn_all, wbf_ref[:, 256:512])
    q_all = _dot(xn_all, wbf_ref[:, 512:1280])
    k_all = _dot(xn_all, wbf_ref[:, 1280:2048])
    val_all = _dot(xn_all, wbf_ref[:, 2048:2816])
    f_all = _dot(xn_all, wf_ref[...])
    for st in range(SUBTILES):
        rows = slice(st * sub, (st + 1) * sub)
        u, v, q, k = u_all[rows], v_all[rows], q_all[rows], k_all[rows]
        val, f = val_all[rows], f_all[rows]

        gu = jax.nn.gelu(u)
        vn = _head_rms(jax.nn.gelu(v), bd, gv_ref[...])
        pair_out = []
        for pr in range(2):
            vp = vn[:, pr * 128:(pr + 1) * 128]
            v_lo = jnp.where(lo_half, vp, 0.0).astype(BF16)
            v_hi = jnp.where(lo_half, 0.0, vp).astype(BF16)
            chunks = []
            for c in range(sub // CHUNK):
                rs = slice(c * CHUNK, (c + 1) * CHUNK)
                s = (_dot(wtril_ref[2 * pr], v_lo[rs]) + _dot(wtril_ref[2 * pr + 1], v_hi[rs])
                     + bmat[:, pr * 128:(pr + 1) * 128])
                chunks.append(s)
            pair_out.append(jnp.concatenate(chunks, axis=0))
        s_all = jnp.concatenate(pair_out, axis=1)
        a_ref[rows, :] = (gu * s_all).astype(BF16)

        for pr in range(FOX_HEADS // 2):
            ls = slice(pr * LANES, (pr + 1) * LANES)
            qt = q[:, ls].T
            heads = []
            for hh in range(2):
                xh = qt[hh * HEAD_DIM:(hh + 1) * HEAD_DIM]
                ms = jnp.sum(xh * xh, axis=0, keepdims=True) * (1.0 / HEAD_DIM)
                heads.append(xh * lax.rsqrt(ms + EPS))
            q_ref[pr, :, rows] = (jnp.concatenate(heads, axis=0) * gq_ref[...]).astype(BF16)
            v_ref[pr, :, rows] = val[:, ls].T.astype(BF16)
        k_ref[rows, :] = _head_rms(k, bd, gk_ref[...]).astype(BF16)

        logf = _log_sigmoid(f + bf_ref[...])
        cum = (_dot_split(logf, ltri_ref[...], terms=3, w_left=True)
               + carry_ref[0:1, :])
        carry_ref[0:1, :] = cum[sub - 1:sub, :]
        cum2 = cum * LOG2E
        fq_ref[rows, :] = cum2
        hi, mid, lo = (p.astype(F32) for p in _split_bf16(cum2, 3))
        f_ref[rows, :] = jnp.where(
            lane < FOX_HEADS, hi,
            jnp.where(lane < 2 * FOX_HEADS, mid,
                      jnp.where(lane < 3 * FOX_HEADS, lo,
                                jnp.where(lane < 3 * FOX_HEADS + 3, 1.0, 0.0)))).astype(BF16)


def _even_in(x2, g, w_all, wf, bf, wtril, bmat, gv, gq, gk, bd, ltri, seq):
    t = x2.shape[0]
    n = t // TM
    full = lambda a: pl.BlockSpec(a.shape, lambda i: (0,) * a.ndim)
    row = lambda wdt: pl.BlockSpec((TM, wdt), lambda i: (i, 0))
    pairs = FOX_HEADS // 2
    colT = pl.BlockSpec((pairs, LANES, TM), lambda i: (0, 0, i))
    w_main = pl.BlockSpec((EVEN_MAIN, D_MODEL), lambda i: (0, 0), pipeline_mode=pl.Buffered(1))
    return pl.pallas_call(
        functools.partial(_even_in_kernel, seq // TM),
        grid=(n,),
        in_specs=[row(D_MODEL), full(g), w_main, full(wf), full(bf), full(wtril), full(bmat),
                  full(gv), full(gq), full(gk), full(bd), full(ltri)],
        out_specs=[row(GMLP_WIDTH), colT, row(FOX_WIDTH), colT, row(LANES), row(LANES)],
        out_shape=[jax.ShapeDtypeStruct((t, GMLP_WIDTH), BF16),
                   jax.ShapeDtypeStruct((pairs, LANES, t), BF16),
                   jax.ShapeDtypeStruct((t, FOX_WIDTH), BF16),
                   jax.ShapeDtypeStruct((pairs, LANES, t), BF16),
                   jax.ShapeDtypeStruct((t, LANES), BF16),
                   jax.ShapeDtypeStruct((t, LANES), F32)],
        scratch_shapes=[pltpu.VMEM((D_MODEL, EVEN_MAIN), BF16), pltpu.VMEM((8, LANES), F32)],
        compiler_params=_cparams(("arbitrary",)),
        name="even_in",
    )(x2, g, w_all, wf, bf, wtril, bmat, gv, gq, gk, bd, ltri)


def _fox_query_operand(qt, pr, hh, shift_terms=None):
    tq = qt.shape[1]
    row = lax.broadcasted_iota(jnp.int32, (LANES, tq), 0)
    head = 2 * pr + hh
    in_head = (row >= hh * HEAD_DIM) & (row < (hh + 1) * HEAD_DIM)
    f_rows = (row == head) | (row == head + FOX_HEADS) | (row == head + 2 * FOX_HEADS)
    extra = jnp.where(f_rows, -1.0, 0.0)
    if shift_terms is not None:
        for n, term in enumerate(shift_terms):
            extra = jnp.where(row == 3 * FOX_HEADS + n, term, extra)
    return jnp.concatenate([jnp.where(in_head, qt, 0.0).astype(BF16), extra.astype(BF16)],
                           axis=0)


def _fox_online_kernel(q_ref, k_ref, f_ref, v_ref, o_ref, m_sc, l_sc, acc_sc):
    pr = pl.program_id(1)
    i = pl.program_id(2)
    tq = q_ref.shape[2]
    qt = q_ref[0].astype(F32)
    rhs = [_fox_query_operand(qt, pr, hh) for hh in range(2)]

    m_sc[...] = jnp.full_like(m_sc, NEG)
    l_sc[...] = jnp.zeros_like(l_sc)
    acc_sc[...] = jnp.zeros_like(acc_sc)

    def step(j, masked):
        start = pl.multiple_of(j * tq, tq)
        kaug = jnp.concatenate([k_ref[pl.ds(start, tq), :], f_ref[pl.ds(start, tq), :]],
                               axis=1)
        for hh in range(2):
            s = _dot(kaug, rhs[hh])
            if masked:
                r_id = lax.broadcasted_iota(jnp.int32, (tq, tq), 0)
                c_id = lax.broadcasted_iota(jnp.int32, (tq, tq), 1)
                s = jnp.where(r_id <= c_id, s, NEG)
            m_prev = m_sc[hh]
            m_new = jnp.maximum(m_prev, jnp.max(s, axis=0, keepdims=True))
            alpha = jnp.exp2(m_prev - m_new)
            p = jnp.exp2(s - m_new)
            l_sc[hh] = alpha * l_sc[hh] + jnp.sum(p, axis=0, keepdims=True)
            vt = v_ref[0, hh * HEAD_DIM:(hh + 1) * HEAD_DIM, pl.ds(start, tq)]
            acc_sc[hh] = alpha * acc_sc[hh] + _dot(vt, p.astype(BF16))
            m_sc[hh] = m_new

    def body(j, c):
        step(j, False)
        return c

    lax.fori_loop(0, i, body, 0)
    step(i, True)
    ot = jnp.concatenate([acc_sc[0] / l_sc[0], acc_sc[1] / l_sc[1]], axis=0)
    o_ref[...] = ot.T.astype(o_ref.dtype)


def _fox_fixed_kernel(b_ref, q_ref, k_ref, f_ref, v_ref, fq_ref, o_ref,
                      rhs_sc, z_sc, acc_sc, mask_sc):
    pr = pl.program_id(1)
    tq = z_sc.shape[3]
    tk = z_sc.shape[2]
    nq = rhs_sc.shape[0]
    row = lax.broadcasted_iota(jnp.int32, (LANES, tq), 0)
    for i in range(nq):
        qt = q_ref[0, :, i * tq:(i + 1) * tq].astype(F32)
        fqt = fq_ref[i * tq:(i + 1) * tq, :].T
        for hh in range(2):
            ft = jnp.sum(jnp.where(row == 2 * pr + hh, fqt, 0.0), axis=0, keepdims=True)
            shift = [p.astype(F32) for p in _split_bf16(ft - b_ref[...], 3)]
            rhs_sc[i, hh] = _fox_query_operand(qt, pr, hh, shift)

    mask_sc[...] = jnp.where(lax.broadcasted_iota(jnp.int32, (tk, tq), 0)
                             > lax.broadcasted_iota(jnp.int32, (tk, tq), 1), NEG, 0.0)
    acc_sc[...] = jnp.zeros_like(acc_sc)
    ones = jnp.ones((acc_sc.shape[2] - HEAD_DIM, tk), BF16)

    half = tk // 2

    def scores(i, j, slot):
        rows = slice(j * tk, (j + 1) * tk)
        kaug = jnp.concatenate([k_ref[rows, :], f_ref[rows, :]], axis=1)
        for hh in range(2):
            if i == j:
                z_sc[slot, hh, 0:half, :] = _dot(kaug[0:half], rhs_sc[i, hh])
                z_sc[slot, hh, half:tk, half:tq] = _dot(kaug[half:tk], rhs_sc[i, hh, :, half:tq])
            else:
                z_sc[slot, hh] = _dot(kaug, rhs_sc[i, hh])

    def values(hh, start, size):
        return jnp.concatenate(
            [v_ref[0, hh * HEAD_DIM:(hh + 1) * HEAD_DIM, start:start + size], ones[:, 0:size]],
            axis=0)

    def consume(i, j, slot):
        for hh in range(2):
            p = jnp.exp2(z_sc[slot, hh]).astype(BF16)
            acc_sc[i, hh] += _dot(values(hh, j * tk, tk), p)

    def consume_diagonal(i, slot):
        outs = []
        for hh in range(2):
            p_top = jnp.exp2(z_sc[slot, hh, 0:half, :] + mask_sc[0:half, :]).astype(BF16)
            p_bot = jnp.exp2(z_sc[slot, hh, half:tk, half:tq]
                             + mask_sc[half:tk, half:tq]).astype(BF16)
            acc = acc_sc[i, hh] + _dot(values(hh, i * tk, half), p_top)
            right = acc[:, half:tq] + _dot(values(hh, i * tk + half, half), p_bot)
            acc = jnp.concatenate([acc[:, 0:half], right], axis=1)
            outs.append(acc[0:HEAD_DIM] / acc[HEAD_DIM:HEAD_DIM + 1])
        o_ref[i * tq:(i + 1) * tq, :] = jnp.concatenate(outs, axis=0).T.astype(o_ref.dtype)

    items = [(i, j) for i in range(1, nq) for j in range(i)] + [(i, i) for i in range(nq)]
    scores(*items[0], 0)
    for s, (i, j) in enumerate(items):
        if s + 1 < len(items):
            scores(*items[s + 1], (s + 1) % 2)
        if i == j:
            consume_diagonal(i, s % 2)
        else:
            consume(i, j, s % 2)


def _fox(qt, k, f3, vt, fq, bound, bsz, seq):
    t = k.shape[0]
    pairs = FOX_HEADS // 2
    out_shape = jax.ShapeDtypeStruct((t, FOX_WIDTH), BF16)
    sem = _cparams(("parallel", "parallel", "arbitrary"))

    def common(tq):
        nq = seq // tq
        in_specs = [pl.BlockSpec((1, LANES, tq), lambda b, p, i: (p, 0, b * nq + i)),
                    pl.BlockSpec((seq, LANES), lambda b, p, i: (b, p)),
                    pl.BlockSpec((seq, LANES), lambda b, p, i: (b, 0)),
                    pl.BlockSpec((1, LANES, seq), lambda b, p, i: (p, 0, b))]
        return nq, in_specs, pl.BlockSpec((tq, LANES), lambda b, p, i: (b * nq + i, p))

    def fixed(brow):
        nq = seq // TQ
        return pl.pallas_call(
            _fox_fixed_kernel,
            grid=(bsz, pairs),
            in_specs=[pl.BlockSpec((1, TQ), lambda b, p: (0, 0)),
                      pl.BlockSpec((1, LANES, seq), lambda b, p: (p, 0, b)),
                      pl.BlockSpec((seq, LANES), lambda b, p: (b, p)),
                      pl.BlockSpec((seq, LANES), lambda b, p: (b, 0)),
                      pl.BlockSpec((1, LANES, seq), lambda b, p: (p, 0, b)),
                      pl.BlockSpec((seq, LANES), lambda b, p: (b, 0))],
            out_specs=pl.BlockSpec((seq, LANES), lambda b, p: (b, p)),
            out_shape=out_shape,
            scratch_shapes=[pltpu.VMEM((nq, 2, 2 * LANES, TQ), BF16),
                            pltpu.VMEM((Z_SLOTS, 2, TQ, TQ), F32),
                            pltpu.VMEM((nq, 2, HEAD_DIM + 16, TQ), F32),
                            pltpu.VMEM((TQ, TQ), F32)],
            compiler_params=_cparams(("parallel", "parallel")), name="fox_attention_fixed",
        )(brow, qt, k, f3, vt, fq)

    def online(brow):
        del brow
        nq, in_specs, out_spec = common(TQ_ONLINE)
        stats = pltpu.VMEM((2, 1, TQ_ONLINE), F32)
        return pl.pallas_call(
            _fox_online_kernel,
            grid=(bsz, pairs, nq),
            in_specs=in_specs, out_specs=out_spec, out_shape=out_shape,
            scratch_shapes=[stats, stats, pltpu.VMEM((2, HEAD_DIM, TQ_ONLINE), F32)],
            compiler_params=sem, name="fox_attention_online",
        )(qt, k, f3, vt)

    brow = jnp.full((1, TQ), bound, F32)
    return lax.cond(bound <= FOX_FAST_BOUND_LOG2, fixed, online, brow)


def _even_out_kernel(a_ref, o_ref, x_ref, w32_ref, gm_ref, wr_ref, br_ref,
                     x1_ref, xn_ref, comb_ref, w_ref):
    @pl.when(pl.program_id(0) == 0)
    def _():
        w_ref[...] = w32_ref[...].astype(BF16)

    y_all = _dot(jnp.concatenate([a_ref[...], o_ref[...]], axis=1), w_ref[...])
    sub = x_ref.shape[0] // SUBTILES
    for s in range(SUBTILES):
        rs = slice(s * sub, (s + 1) * sub)
        x1 = x_ref[rs, :] + y_all[rs]
        x1_ref[rs, :] = x1
        xn = _rms_rows(x1, gm_ref[...]).astype(BF16)
        xn_ref[rs, :] = xn
        comb_ref[rs, :] = _router(xn, wr_ref, br_ref)


def _even_out(a, o, x2, w, gm, wr, br):
    t = x2.shape[0]
    n = t // TM
    full = lambda arr: pl.BlockSpec(arr.shape, lambda i: (0,) * arr.ndim)
    row = lambda wdt: pl.BlockSpec((TM, wdt), lambda i: (i, 0))
    return pl.pallas_call(
        _even_out_kernel,
        grid=(n,),
        in_specs=[row(GMLP_WIDTH), row(FOX_WIDTH), row(D_MODEL),
                  pl.BlockSpec(w.shape, lambda i: (0, 0), pipeline_mode=pl.Buffered(1)),
                  full(gm), full(wr), full(br)],
        out_specs=[row(D_MODEL), row(D_MODEL), row(LANES)],
        out_shape=[jax.ShapeDtypeStruct((t, D_MODEL), F32),
                   jax.ShapeDtypeStruct((t, D_MODEL), BF16),
                   jax.ShapeDtypeStruct((t, LANES), F32)],
        scratch_shapes=[pltpu.VMEM(w.shape, BF16)],
        compiler_params=_cparams(("arbitrary",)),
        name="even_out",
    )(a, o, x2, w, gm, wr, br)


def _odd_kernel(tiles_per_seq, x_ref, g_ref, w32_ref, cw_ref, wp_ref, ps_ref, wo32_ref, gm_ref,
                wr_ref, br_ref, x1_ref, xn_ref, comb_ref, w_ref, wo_ref, zbuf, pbuf):
    i = pl.program_id(0)
    tm = x_ref.shape[0]
    zpad = zbuf.shape[0] - tm
    ppad = pbuf.shape[0] - tm

    @pl.when(i == 0)
    def _():
        w_ref[...] = w32_ref[...].astype(BF16)
        wo_ref[...] = wo32_ref[...].astype(BF16)

    @pl.when(i % tiles_per_seq == 0)
    def _():
        zbuf[0:zpad, :] = jnp.zeros((zpad, CONV_WIDTH), F32)
        pbuf[0:ppad, :] = jnp.zeros((ppad, POOL_WIDTH), F32)

    cw = cw_ref[...]
    ps = ps_ref[...]
    sub = tm // SUBTILES
    xn_all = _rms_rows(x_ref[...], g_ref[...]).astype(BF16)
    cg_all = _dot(xn_all, w_ref[:, 512:1024])
    hc_all = _dot(xn_all, w_ref[:, 1024:1536])
    p_all = _dot(xn_all, w_ref[:, 1536:2048])
    bg_all = _dot(xn_all, w_ref[:, 0:512])
    for s in range(SUBTILES):
        r0 = s * sub
        rs = slice(r0, r0 + sub)
        x = x_ref[rs, :]
        cg, hc, p, bg = cg_all[rs], hc_all[rs], p_all[rs], bg_all[rs]

        z = cg * hc
        zbuf[zpad + r0:zpad + r0 + sub, :] = z
        y = (cw[0:1, :] * zbuf[zpad + r0 - 2:zpad + r0 - 2 + sub, :]
             + cw[1:2, :] * zbuf[zpad + r0 - 1:zpad + r0 - 1 + sub, :]
             + cw[2:3, :] * z)
        c = bg * y

        pbuf[ppad + r0:ppad + r0 + sub, :] = p
        pos = ((i % tiles_per_seq) * tm + r0
               + lax.broadcasted_iota(jnp.int32, (sub, POOL_GROUP), 0)).astype(F32) + 1.0
        pooled_out = []
        for gi, win in enumerate(POOL_WINDOWS):
            ls = slice(gi * POOL_GROUP, (gi + 1) * POOL_GROUP)
            pg = p[:, ls]
            sm = pg
            for sh in range(1, win):
                sm = sm + pbuf[ppad + r0 - sh:ppad + r0 - sh + sub, ls]
            cnt = jnp.minimum(pos, float(win))
            pooled = sm / cnt - pg
            pooled_out.append((_dot(pooled.astype(BF16), wp_ref[gi]) * ps[:, ls]).astype(BF16))
        mix = jnp.concatenate([c.astype(BF16)] + pooled_out, axis=1)
        acc = x + _dot(mix, wo_ref[...])

        x1_ref[rs, :] = acc
        xn2 = _rms_rows(acc, gm_ref[...]).astype(BF16)
        xn_ref[rs, :] = xn2
        comb_ref[rs, :] = _router(xn2, wr_ref, br_ref)

    zbuf[0:zpad, :] = zbuf[tm:tm + zpad, :]
    pbuf[0:ppad, :] = pbuf[tm:tm + ppad, :]


def _odd(x2, g, w, cw, wp, ps, wo, gm, wr, br, seq):
    t = x2.shape[0]
    n = t // TM
    full = lambda arr: pl.BlockSpec(arr.shape, lambda i: (0,) * arr.ndim)
    once = lambda arr: pl.BlockSpec(arr.shape, lambda i: (0,) * arr.ndim,
                                    pipeline_mode=pl.Buffered(1))
    row = lambda wdt: pl.BlockSpec((TM, wdt), lambda i: (i, 0))
    return pl.pallas_call(
        functools.partial(_odd_kernel, seq // TM),
        grid=(n,),
        in_specs=[row(D_MODEL), full(g), once(w), full(cw), full(wp), full(ps), once(wo),
                  full(gm), full(wr), full(br)],
        out_specs=[row(D_MODEL), row(D_MODEL), row(LANES)],
        out_shape=[jax.ShapeDtypeStruct((t, D_MODEL), F32),
                   jax.ShapeDtypeStruct((t, D_MODEL), BF16),
                   jax.ShapeDtypeStruct((t, LANES), F32)],
        scratch_shapes=[pltpu.VMEM(w.shape, BF16), pltpu.VMEM(wo.shape, BF16),
                        pltpu.VMEM((TM + 8, CONV_WIDTH), F32),
                        pltpu.VMEM((TM + 16, POOL_WIDTH), F32)],
        compiler_params=_cparams(("arbitrary",)),
        name="odd_mixer",
    )(x2, g, w, cw, wp, ps, wo, gm, wr, br)


def _moe_kernel(xn_ref, wg_ref, wu_ref, wd_ref, comb_ref, x1_ref, o_ref):
    g = pl.program_id(1)

    @pl.when(g == 0)
    def _():
        o_ref[...] = x1_ref[...]

    tm = xn_ref.shape[0]
    xn = xn_ref[...]
    lane = lax.broadcasted_iota(jnp.int32, (tm, LANES), 1)
    comb = comb_ref[...]
    hid = []
    for e in range(EXPERTS_PER_GROUP):
        gate = _dot(xn, wg_ref[e].astype(BF16))
        up = _dot(xn, wu_ref[e].astype(BF16))
        cw = jnp.sum(jnp.where(lane == g * EXPERTS_PER_GROUP + e, comb, 0.0), axis=1,
                     keepdims=True)
        hid.append(((gate * jax.nn.sigmoid(gate)) * up * cw).astype(BF16))
    o_ref[...] += _dot(jnp.concatenate(hid, axis=1), wd_ref[0].astype(BF16))


def _moe(xn, wg, wu, wd, comb, x1, layer):
    t = xn.shape[0]
    n = t // TM_MOE
    gh = EXPERTS_PER_GROUP * EXPERT_HIDDEN
    wblk = pl.BlockSpec((None, EXPERTS_PER_GROUP, D_MODEL, EXPERT_HIDDEN),
                        lambda i, g: (layer, g, 0, 0))
    return pl.pallas_call(
        _moe_kernel,
        grid=(n, N_GROUPS),
        in_specs=[pl.BlockSpec((TM_MOE, D_MODEL), lambda i, g: (i, 0)),
                  wblk, wblk,
                  pl.BlockSpec((None, 1, gh, D_MODEL), lambda i, g: (layer, g, 0, 0)),
                  pl.BlockSpec((TM_MOE, LANES), lambda i, g: (i, 0)),
                  pl.BlockSpec((TM_MOE, D_MODEL), lambda i, g: (i, 0))],
        out_specs=pl.BlockSpec((TM_MOE, D_MODEL), lambda i, g: (i, 0)),
        out_shape=jax.ShapeDtypeStruct((t, D_MODEL), F32),
        compiler_params=_cparams(("parallel", "arbitrary")),
        name="moe",
    )(xn, wg, wu, wd, comb, x1)


def _router_params(w_group, b_group, w_router, b_router):
    wr = jnp.concatenate([w_router.reshape(D_MODEL, N_EXPERTS), w_group], axis=1)
    wr = jnp.pad(wr, ((0, 0), (0, LANES - wr.shape[1]))).astype(BF16)
    br = jnp.concatenate([b_router.reshape(N_EXPERTS), b_group])
    br = jnp.pad(br, (0, LANES - br.shape[0])).reshape(1, LANES).astype(F32)
    return wr, br


def _moe_params(w_gate, w_up, w_down):
    wd = w_down.reshape(w_down.shape[0], N_GROUPS, EXPERTS_PER_GROUP * EXPERT_HIDDEN, D_MODEL)
    return w_gate, w_up, wd


def kernel(x, ev_norm, ev_w_in, ev_b_forget, ev_w_s, ev_b_s, ev_g_v, ev_g_q, ev_g_k, ev_w_out,
           od_norm, od_w_in, od_conv_w, od_w_pool, od_pool_scale, od_w_out,
           moe_norm, moe_w_group, moe_b_group, moe_w_router, moe_b_router,
           moe_w_gate, moe_w_up, moe_w_down):
    bsz, seq, d = x.shape
    t = bsz * seq
    x2 = x.reshape(t, d)

    w_in_t = ev_w_in[0].T
    w_f = w_in_t[EVEN_MAIN:].T
    w_f3 = jnp.pad(jnp.tile(w_f, (1, 3)), ((0, 0), (0, LANES - 3 * FOX_HEADS))).astype(BF16)
    bf = jnp.pad(jnp.tile(ev_b_forget[0], 3), (0, LANES - 3 * FOX_HEADS)).reshape(1, LANES)
    tril = jnp.tril(jnp.ones((CHUNK, CHUNK), F32))
    wtril = (ev_w_s[0] * tril).astype(BF16)
    bmat = jnp.repeat(ev_b_s[0].T, GMLP_HEAD, axis=1)
    gv = ev_g_v[0].reshape(1, GMLP_WIDTH)
    gq = jnp.broadcast_to((jnp.tile(ev_g_q[0], 2) * (HEAD_DIM ** -0.5 * LOG2E))[:, None],
                          (LANES, TM // SUBTILES))
    gk = jnp.tile(ev_g_k[0], FOX_HEADS).reshape(1, FOX_WIDTH)
    blk = jnp.arange(MXU_DIM) // HEAD_DIM
    bd = jnp.where(blk[:, None] == blk[None, :], 1.0 / HEAD_DIM, 0.0).astype(BF16)
    ltri = jnp.tril(jnp.ones((TM // SUBTILES, TM // SUBTILES), F32)).astype(BF16)

    a, qt, kn, vt, f3, fq = _even_in(x2, ev_norm[0].reshape(1, d), w_in_t, w_f3, bf, wtril,
                                     bmat, gv, gq, gk, bd, ltri, seq)
    bound = (HEAD_DIM ** 0.5 * LOG2E * 1.01) * jnp.max(jnp.abs(ev_g_q[0])) * jnp.max(
        jnp.abs(ev_g_k[0]))
    o = _fox(qt, kn, f3, vt, fq, bound, bsz, seq)

    wr0, br0 = _router_params(moe_w_group[0], moe_b_group[0], moe_w_router[0], moe_b_router[0])
    x1, xn1, comb1 = _even_out(a, o, x2, ev_w_out[0],
                               moe_norm[0].reshape(1, d), wr0, br0)
    moe_w = _moe_params(moe_w_gate, moe_w_up, moe_w_down)
    xa = _moe(xn1, *moe_w, comb1, x1, 0)

    wr1, br1 = _router_params(moe_w_group[1], moe_b_group[1], moe_w_router[1], moe_b_router[1])
    x3, xn3, comb3 = _odd(xa, od_norm[0].reshape(1, d), od_w_in[0], od_conv_w[0],
                          od_w_pool[0].astype(BF16), od_pool_scale[0].reshape(1, POOL_WIDTH),
                          od_w_out[0], moe_norm[1].reshape(1, d), wr1, br1, seq)
    xb = _moe(xn3, *moe_w, comb3, x3, 1)
    return xb.reshape(bsz, seq, d)
```

```python
import functools

import jax
import jax.numpy as jnp
from jax import lax
from jax.experimental import pallas as pl
from jax.experimental.pallas import tpu as pltpu

F32 = jnp.float32
BF16 = jnp.bfloat16

D_MODEL = 1024
EPS = 1e-6
CHUNK = 128
GMLP_GROUPS = 4
GMLP_HEAD = 64
GMLP_WIDTH = 256
FOX_HEADS = 12
HEAD_DIM = 64
FOX_WIDTH = 768
CONV_WIDTH = 512
CONV_K = 3
POOL_WINDOWS = (2, 4, 8, 16)
POOL_GROUP = 128
POOL_WIDTH = 512
N_GROUPS = 4
EXPERTS_PER_GROUP = 4
N_EXPERTS = 16
EXPERT_HIDDEN = 256

LANES = 128
MXU_DIM = 256
V7X_VMEM_LIMIT_BYTES = 56 * 1024 * 1024

TM = 512
TQ = 512
TQ_ONLINE = 256
Z_SLOTS = 2
TM_MOE = 1024
SUBTILES = 2
EVEN_MAIN = 2 * GMLP_WIDTH + 3 * FOX_WIDTH
NEG = -1e30
LOG2E = 1.4426950408889634
FOX_FAST_BOUND_LOG2 = 50.0


def _cparams(sem):
    return pltpu.CompilerParams(dimension_semantics=sem,
                                vmem_limit_bytes=V7X_VMEM_LIMIT_BYTES)


def _dot(a, b):
    return jnp.dot(a, b, preferred_element_type=F32)


def _split_bf16(x, terms):
    parts = []
    r = x
    for _ in range(terms):
        p = r.astype(BF16)
        parts.append(p)
        r = r - p.astype(F32)
    return parts


def _dot_split(x, w_bf16, terms=2, w_left=False):
    acc = None
    for p in _split_bf16(x, terms):
        d = _dot(w_bf16, p) if w_left else _dot(p, w_bf16)
        acc = d if acc is None else acc + d
    return acc


def _rms_rows(x, g):
    ms = jnp.mean(x * x, axis=-1, keepdims=True)
    return x * lax.rsqrt(ms + EPS) * g


def _head_rms(x, bd, g):
    outs = []
    for c in range(x.shape[1] // MXU_DIM):
        xc = x[:, c * MXU_DIM:(c + 1) * MXU_DIM]
        ms = _dot_split(xc * xc, bd)
        outs.append(xc * lax.rsqrt(ms + EPS))
    y = outs[0] if len(outs) == 1 else jnp.concatenate(outs, axis=1)
    return y * g


def _log_sigmoid(x):
    return -(jnp.maximum(-x, 0.0) + jnp.log(1.0 + jnp.exp(-jnp.abs(x))))


def _router(xn_bf16, wr_ref, br_ref):
    r = _dot(xn_bf16, wr_ref[...]) + br_ref[...]
    tm = r.shape[0]
    rt = r.T
    er = rt[0:N_EXPERTS]
    gr = rt[N_EXPERTS:N_EXPERTS + 8]
    grow = lax.broadcasted_iota(jnp.int32, (8, tm), 0).astype(F32)
    erow = lax.broadcasted_iota(jnp.int32, (N_EXPERTS, tm), 0).astype(F32)
    is_g = grow < float(N_GROUPS)
    gl = jnp.where(is_g, gr, NEG)
    gmax = jnp.max(gl, axis=0, keepdims=True)
    gidx = jnp.min(jnp.where(gl == gmax, grow, 999.0), axis=0, keepdims=True)
    gsum = jnp.sum(jnp.where(is_g, jnp.exp(gl - gmax), 0.0), axis=0, keepdims=True)
    gp = 1.0 / gsum
    lo = gidx * float(EXPERTS_PER_GROUP)
    sel = (erow >= lo) & (erow < lo + float(EXPERTS_PER_GROUP))
    el = jnp.where(sel, er, NEG)
    emax = jnp.max(el, axis=0, keepdims=True)
    ee = jnp.where(sel, jnp.exp(el - emax), 0.0)
    ep = ee / jnp.sum(ee, axis=0, keepdims=True)
    ep = jnp.where(sel, ep, -1.0)
    p1 = jnp.max(ep, axis=0, keepdims=True)
    i1 = jnp.min(jnp.where(ep == p1, erow, 999.0), axis=0, keepdims=True)
    ep2 = jnp.where(erow == i1, -1.0, ep)
    p2 = jnp.max(ep2, axis=0, keepdims=True)
    i2 = jnp.min(jnp.where(ep2 == p2, erow, 999.0), axis=0, keepdims=True)
    den = p1 + p2
    comb_t = (jnp.where(erow == i1, gp * (p1 / den), 0.0)
              + jnp.where(erow == i2, gp * (p2 / den), 0.0))
    comb_t = jnp.concatenate([comb_t, jnp.zeros((LANES - N_EXPERTS, tm), F32)], axis=0)
    return comb_t.T


def _even_in_kernel(tiles_per_seq, x_ref, g_ref, w_ref, wf_ref, bf_ref, wtril_ref, bmat_ref,
                    gv_ref, gq_ref, gk_ref, bd_ref, ltri_ref,
                    a_ref, q_ref, k_ref, v_ref, f_ref, fq_ref, wbf_ref, carry_ref):
    i = pl.program_id(0)
    tm = x_ref.shape[0]

    @pl.when(i == 0)
    def _():
        for c in range(0, w_ref.shape[0], MXU_DIM):
            wbf_ref[:, c:c + MXU_DIM] = w_ref[c:c + MXU_DIM, :].T.astype(BF16)

    @pl.when(i % tiles_per_seq == 0)
    def _():
        carry_ref[...] = jnp.zeros_like(carry_ref)

    sub = tm // SUBTILES
    lane = lax.broadcasted_iota(jnp.int32, (sub, LANES), 1)
    lo_half = lane < GMLP_HEAD
    bmat = bmat_ref[...]
    bd = bd_ref[...]
    xn_all = _rms_rows(x_ref[...], g_ref[...]).astype(BF16)
    u_all = _dot(xn_all, wbf_ref[:, 0:256])
    v_all = _dot(xn_all, wbf_ref[:, 256:512])
    q_all = _dot(xn_all, wbf_ref[:, 512:1280])
    k_all = _dot(xn_all, wbf_ref[:, 1280:2048])
    val_all = _dot(xn_all, wbf_ref[:, 2048:2816])
    f_all = _dot(xn_all, wf_ref[...])
    for st in range(SUBTILES):
        rows = slice(st * sub, (st + 1) * sub)
        u, v, q, k = u_all[rows], v_all[rows], q_all[rows], k_all[rows]
        val, f = val_all[rows], f_all[rows]

        gu = jax.nn.gelu(u)
        vn = _head_rms(jax.nn.gelu(v), bd, gv_ref[...])
        pair_out = []
        for pr in range(2):
            vp = vn[:, pr * 128:(pr + 1) * 128]
            v_lo = jnp.where(lo_half, vp, 0.0).astype(BF16)
            v_hi = jnp.where(lo_half, 0.0, vp).astype(BF16)
            chunks = []
            for c in range(sub // CHUNK):
                rs = slice(c * CHUNK, (c + 1) * CHUNK)
                s = (_dot(wtril_ref[2 * pr], v_lo[rs]) + _dot(wtril_ref[2 * pr + 1], v_hi[rs])
                     + bmat[:, pr * 128:(pr + 1) * 128])
                chunks.append(s)
            pair_out.append(jnp.concatenate(chunks, axis=0))
        s_all = jnp.concatenate(pair_out, axis=1)
        a_ref[rows, :] = (gu * s_all).astype(BF16)

        for pr in range(FOX_HEADS // 2):
            ls = slice(pr * LANES, (pr + 1) * LANES)
            qt = q[:, ls].T
            heads = []
            for hh in range(2):
                xh = qt[hh * HEAD_DIM:(hh + 1) * HEAD_DIM]
                ms = jnp.sum(xh * xh, axis=0, keepdims=True) * (1.0 / HEAD_DIM)
                heads.append(xh * lax.rsqrt(ms + EPS))
            q_ref[pr, :, rows] = (jnp.concatenate(heads, axis=0) * gq_ref[...]).astype(BF16)
            v_ref[pr, :, rows] = val[:, ls].T.astype(BF16)
        k_ref[rows, :] = _head_rms(k, bd, gk_ref[...]).astype(BF16)

        logf = _log_sigmoid(f + bf_ref[...])
        cum = (_dot_split(logf, ltri_ref[...], terms=3, w_left=True)
               + carry_ref[0:1, :])
        carry_ref[0:1, :] = cum[sub - 1:sub, :]
        cum2 = cum * LOG2E
        fq_ref[rows, :] = cum2
        hi, mid, lo = (p.astype(F32) for p in _split_bf16(cum2, 3))
        f_ref[rows, :] = jnp.where(
            lane < FOX_HEADS, hi,
            jnp.where(lane < 2 * FOX_HEADS, mid,
                      jnp.where(lane < 3 * FOX_HEADS, lo,
                                jnp.where(lane < 3 * FOX_HEADS + 3, 1.0, 0.0)))).astype(BF16)


def _even_in(x2, g, w_all, wf, bf, wtril, bmat, gv, gq, gk, bd, ltri, seq):
    t = x2.shape[0]
    n = t // TM
    full = lambda a: pl.BlockSpec(a.shape, lambda i: (0,) * a.ndim)
    row = lambda wdt: pl.BlockSpec((TM, wdt), lambda i: (i, 0))
    pairs = FOX_HEADS // 2
    colT = pl.BlockSpec((pairs, LANES, TM), lambda i: (0, 0, i))
    w_main = pl.BlockSpec((EVEN_MAIN, D_MODEL), lambda i: (0, 0), pipeline_mode=pl.Buffered(1))
    return pl.pallas_call(
        functools.partial(_even_in_kernel, seq // TM),
        grid=(n,),
        in_specs=[row(D_MODEL), full(g), w_main, full(wf), full(bf), full(wtril), full(bmat),
                  full(gv), full(gq), full(gk), full(bd), full(ltri)],
        out_specs=[row(GMLP_WIDTH), colT, row(FOX_WIDTH), colT, row(LANES), row(LANES)],
        out_shape=[jax.ShapeDtypeStruct((t, GMLP_WIDTH), BF16),
                   jax.ShapeDtypeStruct((pairs, LANES, t), BF16),
                   jax.ShapeDtypeStruct((t, FOX_WIDTH), BF16),
                   jax.ShapeDtypeStruct((pairs, LANES, t), BF16),
                   jax.ShapeDtypeStruct((t, LANES), BF16),
                   jax.ShapeDtypeStruct((t, LANES), F32)],
        scratch_shapes=[pltpu.VMEM((D_MODEL, EVEN_MAIN), BF16), pltpu.VMEM((8, LANES), F32)],
        compiler_params=_cparams(("arbitrary",)),
        name="even_in",
    )(x2, g, w_all, wf, bf, wtril, bmat, gv, gq, gk, bd, ltri)


def _fox_query_operand(qt, pr, hh, shift_terms=None):
    tq = qt.shape[1]
    row = lax.broadcasted_iota(jnp.int32, (LANES, tq), 0)
    head = 2 * pr + hh
    in_head = (row >= hh * HEAD_DIM) & (row < (hh + 1) * HEAD_DIM)
    f_rows = (row == head) | (row == head + FOX_HEADS) | (row == head + 2 * FOX_HEADS)
    extra = jnp.where(f_rows, -1.0, 0.0)
    if shift_terms is not None:
        for n, term in enumerate(shift_terms):
            extra = jnp.where(row == 3 * FOX_HEADS + n, term, extra)
    return jnp.concatenate([jnp.where(in_head, qt, 0.0).astype(BF16), extra.astype(BF16)],
                           axis=0)


def _fox_online_kernel(q_ref, k_ref, f_ref, v_ref, o_ref, m_sc, l_sc, acc_sc):
    pr = pl.program_id(1)
    i = pl.program_id(2)
    tq = q_ref.shape[2]
    qt = q_ref[0].astype(F32)
    rhs = [_fox_query_operand(qt, pr, hh) for hh in range(2)]

    m_sc[...] = jnp.full_like(m_sc, NEG)
    l_sc[...] = jnp.zeros_like(l_sc)
    acc_sc[...] = jnp.zeros_like(acc_sc)

    def step(j, masked):
        start = pl.multiple_of(j * tq, tq)
        kaug = jnp.concatenate([k_ref[pl.ds(start, tq), :], f_ref[pl.ds(start, tq), :]],
                               axis=1)
        for hh in range(2):
            s = _dot(kaug, rhs[hh])
            if masked:
                r_id = lax.broadcasted_iota(jnp.int32, (tq, tq), 0)
                c_id = lax.broadcasted_iota(jnp.int32, (tq, tq), 1)
                s = jnp.where(r_id <= c_id, s, NEG)
            m_prev = m_sc[hh]
            m_new = jnp.maximum(m_prev, jnp.max(s, axis=0, keepdims=True))
            alpha = jnp.exp2(m_prev - m_new)
            p = jnp.exp2(s - m_new)
            l_sc[hh] = alpha * l_sc[hh] + jnp.sum(p, axis=0, keepdims=True)
            vt = v_ref[0, hh * HEAD_DIM:(hh + 1) * HEAD_DIM, pl.ds(start, tq)]
            acc_sc[hh] = alpha * acc_sc[hh] + _dot(vt, p.astype(BF16))
            m_sc[hh] = m_new

    def body(j, c):
        step(j, False)
        return c

    lax.fori_loop(0, i, body, 0)
    step(i, True)
    ot = jnp.concatenate([acc_sc[0] / l_sc[0], acc_sc[1] / l_sc[1]], axis=0)
    o_ref[...] = ot.T.astype(o_ref.dtype)


def _fox_fixed_kernel(b_ref, q_ref, k_ref, f_ref, v_ref, fq_ref, o_ref,
                      rhs_sc, z_sc, acc_sc, mask_sc):
    pr = pl.program_id(1)
    tq = z_sc.shape[3]
    tk = z_sc.shape[2]
    nq = rhs_sc.shape[0]
    row = lax.broadcasted_iota(jnp.int32, (LANES, tq), 0)
    for i in range(nq):
        qt = q_ref[0, :, i * tq:(i + 1) * tq].astype(F32)
        fqt = fq_ref[i * tq:(i + 1) * tq, :].T
        for hh in range(2):
            ft = jnp.sum(jnp.where(row == 2 * pr + hh, fqt, 0.0), axis=0, keepdims=True)
            shift = [p.astype(F32) for p in _split_bf16(ft - b_ref[...], 3)]
            rhs_sc[i, hh] = _fox_query_operand(qt, pr, hh, shift)

    mask_sc[...] = jnp.where(lax.broadcasted_iota(jnp.int32, (tk, tq), 0)
                             > lax.broadcasted_iota(jnp.int32, (tk, tq), 1), NEG, 0.0)
    acc_sc[...] = jnp.zeros_like(acc_sc)
    ones = jnp.ones((acc_sc.shape[2] - HEAD_DIM, tk), BF16)

    half = tk // 2

    def scores(i, j, slot):
        rows = slice(j * tk, (j + 1) * tk)
        kaug = jnp.concatenate([k_ref[rows, :], f_ref[rows, :]], axis=1)
        for hh in range(2):
            if i == j:
                z_sc[slot, hh, 0:half, :] = _dot(kaug[0:half], rhs_sc[i, hh])
                z_sc[slot, hh, half:tk, half:tq] = _dot(kaug[half:tk], rhs_sc[i, hh, :, half:tq])
            else:
                z_sc[slot, hh] = _dot(kaug, rhs_sc[i, hh])

    def values(hh, start, size):
        return jnp.concatenate(
            [v_ref[0, hh * HEAD_DIM:(hh + 1) * HEAD_DIM, start:start + size], ones[:, 0:size]],
            axis=0)

    def consume(i, j, slot):
        for hh in range(2):
            p = jnp.exp2(z_sc[slot, hh]).astype(BF16)
            acc_sc[i, hh] += _dot(values(hh, j * tk, tk), p)

    def consume_diagonal(i, slot):
        outs = []
        for hh in range(2):
            p_top = jnp.exp2(z_sc[slot, hh, 0:half, :] + mask_sc[0:half, :]).astype(BF16)
            p_bot = jnp.exp2(z_sc[slot, hh, half:tk, half:tq]
                             + mask_sc[half:tk, half:tq]).astype(BF16)
            acc = acc_sc[i, hh] + _dot(values(hh, i * tk, half), p_top)
            right = acc[:, half:tq] + _dot(values(hh, i * tk + half, half), p_bot)
            acc = jnp.concatenate([acc[:, 0:half], right], axis=1)
            outs.append(acc[0:HEAD_DIM] / acc[HEAD_DIM:HEAD_DIM + 1])
        o_ref[i * tq:(i + 1) * tq, :] = jnp.concatenate(outs, axis=0).T.astype(o_ref.dtype)

    items = [(i, j) for i in range(1, nq) for j in range(i)] + [(i, i) for i in range(nq)]
    scores(*items[0], 0)
    for s, (i, j) in enumerate(items):
        if s + 1 < len(items):
            scores(*items[s + 1], (s + 1) % 2)
        if i == j:
            consume_diagonal(i, s % 2)
        else:
            consume(i, j, s % 2)


def _fox(qt, k, f3, vt, fq, bound, bsz, seq):
    t = k.shape[0]
    pairs = FOX_HEADS // 2
    out_shape = jax.ShapeDtypeStruct((t, FOX_WIDTH), BF16)
    sem = _cparams(("parallel", "parallel", "arbitrary"))

    def common(tq):
        nq = seq // tq
        in_specs = [pl.BlockSpec((1, LANES, tq), lambda b, p, i: (p, 0, b * nq + i)),
                    pl.BlockSpec((seq, LANES), lambda b, p, i: (b, p)),
                    pl.BlockSpec((seq, LANES), lambda b, p, i: (b, 0)),
                    pl.BlockSpec((1, LANES, seq), lambda b, p, i: (p, 0, b))]
        return nq, in_specs, pl.BlockSpec((tq, LANES), lambda b, p, i: (b * nq + i, p))

    def fixed(brow):
        nq = seq // TQ
        return pl.pallas_call(
            _fox_fixed_kernel,
            grid=(bsz, pairs),
            in_specs=[pl.BlockSpec((1, TQ), lambda b, p: (0, 0)),
                      pl.BlockSpec((1, LANES, seq), lambda b, p: (p, 0, b)),
                      pl.BlockSpec((seq, LANES), lambda b, p: (b, p)),
                      pl.BlockSpec((seq, LANES), lambda b, p: (b, 0)),
                      pl.BlockSpec((1, LANES, seq), lambda b, p: (p, 0, b)),
                      pl.BlockSpec((seq, LANES), lambda b, p: (b, 0))],
            out_specs=pl.BlockSpec((seq, LANES), lambda b, p: (b, p)),
            out_shape=out_shape,
            scratch_shapes=[pltpu.VMEM((nq, 2, 2 * LANES, TQ), BF16),
                            pltpu.VMEM((Z_SLOTS, 2, TQ, TQ), F32),
                            pltpu.VMEM((nq, 2, HEAD_DIM + 16, TQ), F32),
                            pltpu.VMEM((TQ, TQ), F32)],
            compiler_params=_cparams(("parallel", "parallel")), name="fox_attention_fixed",
        )(brow, qt, k, f3, vt, fq)

    def online(brow):
        del brow
        nq, in_specs, out_spec = common(TQ_ONLINE)
        stats = pltpu.VMEM((2, 1, TQ_ONLINE), F32)
        return pl.pallas_call(
            _fox_online_kernel,
            grid=(bsz, pairs, nq),
            in_specs=in_specs, out_specs=out_spec, out_shape=out_shape,
            scratch_shapes=[stats, stats, pltpu.VMEM((2, HEAD_DIM, TQ_ONLINE), F32)],
            compiler_params=sem, name="fox_attention_online",
        )(qt, k, f3, vt)

    brow = jnp.full((1, TQ), bound, F32)
    return lax.cond(bound <= FOX_FAST_BOUND_LOG2, fixed, online, brow)


def _even_out_kernel(a_ref, o_ref, x_ref, w32_ref, gm_ref, wr_ref, br_ref,
                     x1_ref, xn_ref, comb_ref, w_ref):
    @pl.when(pl.program_id(0) == 0)
    def _():
        w_ref[...] = w32_ref[...].astype(BF16)

    y_all = _dot(jnp.concatenate([a_ref[...], o_ref[...]], axis=1), w_ref[...])
    sub = x_ref.shape[0] // SUBTILES
    for s in range(SUBTILES):
        rs = slice(s * sub, (s + 1) * sub)
        x1 = x_ref[rs, :] + y_all[rs]
        x1_ref[rs, :] = x1
        xn = _rms_rows(x1, gm_ref[...]).astype(BF16)
        xn_ref[rs, :] = xn
        comb_ref[rs, :] = _router(xn, wr_ref, br_ref)


def _even_out(a, o, x2, w, gm, wr, br):
    t = x2.shape[0]
    n = t // TM
    full = lambda arr: pl.BlockSpec(arr.shape, lambda i: (0,) * arr.ndim)
    row = lambda wdt: pl.BlockSpec((TM, wdt), lambda i: (i, 0))
    return pl.pallas_call(
        _even_out_kernel,
        grid=(n,),
        in_specs=[row(GMLP_WIDTH), row(FOX_WIDTH), row(D_MODEL),
                  pl.BlockSpec(w.shape, lambda i: (0, 0), pipeline_mode=pl.Buffered(1)),
                  full(gm), full(wr), full(br)],
        out_specs=[row(D_MODEL), row(D_MODEL), row(LANES)],
        out_shape=[jax.ShapeDtypeStruct((t, D_MODEL), F32),
                   jax.ShapeDtypeStruct((t, D_MODEL), BF16),
                   jax.ShapeDtypeStruct((t, LANES), F32)],
        scratch_shapes=[pltpu.VMEM(w.shape, BF16)],
        compiler_params=_cparams(("arbitrary",)),
        name="even_out",
    )(a, o, x2, w, gm, wr, br)


def _odd_kernel(tiles_per_seq, x_ref, g_ref, w32_ref, cw_ref, wp_ref, ps_ref, wo32_ref, gm_ref,
                wr_ref, br_ref, x1_ref, xn_ref, comb_ref, w_ref, wo_ref, zbuf, pbuf):
    i = pl.program_id(0)
    tm = x_ref.shape[0]
    zpad = zbuf.shape[0] - tm
    ppad = pbuf.shape[0] - tm

    @pl.when(i == 0)
    def _():
        w_ref[...] = w32_ref[...].astype(BF16)
        wo_ref[...] = wo32_ref[...].astype(BF16)

    @pl.when(i % tiles_per_seq == 0)
    def _():
        zbuf[0:zpad, :] = jnp.zeros((zpad, CONV_WIDTH), F32)
        pbuf[0:ppad, :] = jnp.zeros((ppad, POOL_WIDTH), F32)

    cw = cw_ref[...]
    ps = ps_ref[...]
    sub = tm // SUBTILES
    xn_all = _rms_rows(x_ref[...], g_ref[...]).astype(BF16)
    cg_all = _dot(xn_all, w_ref[:, 512:1024])
    hc_all = _dot(xn_all, w_ref[:, 1024:1536])
    p_all = _dot(xn_all, w_ref[:, 1536:2048])
    bg_all = _dot(xn_all, w_ref[:, 0:512])
    for s in range(SUBTILES):
        r0 = s * sub
        rs = slice(r0, r0 + sub)
        x = x_ref[rs, :]
        cg, hc, p, bg = cg_all[rs], hc_all[rs], p_all[rs], bg_all[rs]

        z = cg * hc
        zbuf[zpad + r0:zpad + r0 + sub, :] = z
        y = (cw[0:1, :] * zbuf[zpad + r0 - 2:zpad + r0 - 2 + sub, :]
             + cw[1:2, :] * zbuf[zpad + r0 - 1:zpad + r0 - 1 + sub, :]
             + cw[2:3, :] * z)
        c = bg * y

        pbuf[ppad + r0:ppad + r0 + sub, :] = p
        pos = ((i % tiles_per_seq) * tm + r0
               + lax.broadcasted_iota(jnp.int32, (sub, POOL_GROUP), 0)).astype(F32) + 1.0
        pooled_out = []
        for gi, win in enumerate(POOL_WINDOWS):
            ls = slice(gi * POOL_GROUP, (gi + 1) * POOL_GROUP)
            pg = p[:, ls]
            sm = pbuf[r0:r0 + ppad + sub, ls]
            span = 1
            while span < win:
                sm = sm + pltpu.roll(sm, span, axis=0)
                span *= 2
            sm = sm[ppad:]
            cnt = jnp.minimum(pos, float(win))
            pooled = sm / cnt - pg
            pooled_out.append((_dot(pooled.astype(BF16), wp_ref[gi]) * ps[:, ls]).astype(BF16))
        mix = jnp.concatenate([c.astype(BF16)] + pooled_out, axis=1)
        acc = x + _dot(mix, wo_ref[...])

        x1_ref[rs, :] = acc
        xn2 = _rms_rows(acc, gm_ref[...]).astype(BF16)
        xn_ref[rs, :] = xn2
        comb_ref[rs, :] = _router(xn2, wr_ref, br_ref)

    zbuf[0:zpad, :] = zbuf[tm:tm + zpad, :]
    pbuf[0:ppad, :] = pbuf[tm:tm + ppad, :]


def _odd(x2, g, w, cw, wp, ps, wo, gm, wr, br, seq):
    t = x2.shape[0]
    n = t // TM
    full = lambda arr: pl.BlockSpec(arr.shape, lambda i: (0,) * arr.ndim)
    once = lambda arr: pl.BlockSpec(arr.shape, lambda i: (0,) * arr.ndim,
                                    pipeline_mode=pl.Buffered(1))
    row = lambda wdt: pl.BlockSpec((TM, wdt), lambda i: (i, 0))
    return pl.pallas_call(
        functools.partial(_odd_kernel, seq // TM),
        grid=(n,),
        in_specs=[row(D_MODEL), full(g), once(w), full(cw), full(wp), full(ps), once(wo),
                  full(gm), full(wr), full(br)],
        out_specs=[row(D_MODEL), row(D_MODEL), row(LANES)],
        out_shape=[jax.ShapeDtypeStruct((t, D_MODEL), F32),
                   jax.ShapeDtypeStruct((t, D_MODEL), BF16),
                   jax.ShapeDtypeStruct((t, LANES), F32)],
        scratch_shapes=[pltpu.VMEM(w.shape, BF16), pltpu.VMEM(wo.shape, BF16),
                        pltpu.VMEM((TM + 8, CONV_WIDTH), F32),
                        pltpu.VMEM((TM + 16, POOL_WIDTH), F32)],
        compiler_params=_cparams(("arbitrary",)),
        name="odd_mixer",
    )(x2, g, w, cw, wp, ps, wo, gm, wr, br)


def _moe_kernel(xn_ref, wg_ref, wu_ref, wd_ref, comb_ref, x1_ref, o_ref):
    g = pl.program_id(1)

    @pl.when(g == 0)
    def _():
        o_ref[...] = x1_ref[...]

    tm = xn_ref.shape[0]
    xn = xn_ref[...]
    lane = lax.broadcasted_iota(jnp.int32, (tm, LANES), 1)
    comb = comb_ref[...]
    hid = []
    for e in range(EXPERTS_PER_GROUP):
        gate = _dot(xn, wg_ref[e].astype(BF16))
        up = _dot(xn, wu_ref[e].astype(BF16))
        cw = jnp.sum(jnp.where(lane == g * EXPERTS_PER_GROUP + e, comb, 0.0), axis=1,
                     keepdims=True)
        hid.append(((gate * jax.nn.sigmoid(gate)) * up * cw).astype(BF16))
    o_ref[...] += _dot(jnp.concatenate(hid, axis=1), wd_ref[0].astype(BF16))


def _moe(xn, wg, wu, wd, comb, x1, layer):
    t = xn.shape[0]
    n = t // TM_MOE
    gh = EXPERTS_PER_GROUP * EXPERT_HIDDEN
    wblk = pl.BlockSpec((None, EXPERTS_PER_GROUP, D_MODEL, EXPERT_HIDDEN),
                        lambda i, g: (layer, g, 0, 0))
    return pl.pallas_call(
        _moe_kernel,
        grid=(n, N_GROUPS),
        in_specs=[pl.BlockSpec((TM_MOE, D_MODEL), lambda i, g: (i, 0)),
                  wblk, wblk,
                  pl.BlockSpec((None, 1, gh, D_MODEL), lambda i, g: (layer, g, 0, 0)),
                  pl.BlockSpec((TM_MOE, LANES), lambda i, g: (i, 0)),
                  pl.BlockSpec((TM_MOE, D_MODEL), lambda i, g: (i, 0))],
        out_specs=pl.BlockSpec((TM_MOE, D_MODEL), lambda i, g: (i, 0)),
        out_shape=jax.ShapeDtypeStruct((t, D_MODEL), F32),
        compiler_params=_cparams(("parallel", "arbitrary")),
        name="moe",
    )(xn, wg, wu, wd, comb, x1)


def _router_params(w_group, b_group, w_router, b_router):
    wr = jnp.concatenate([w_router.reshape(D_MODEL, N_EXPERTS), w_group], axis=1)
    wr = jnp.pad(wr, ((0, 0), (0, LANES - wr.shape[1]))).astype(BF16)
    br = jnp.concatenate([b_router.reshape(N_EXPERTS), b_group])
    br = jnp.pad(br, (0, LANES - br.shape[0])).reshape(1, LANES).astype(F32)
    return wr, br


def _moe_params(w_gate, w_up, w_down):
    wd = w_down.reshape(w_down.shape[0], N_GROUPS, EXPERTS_PER_GROUP * EXPERT_HIDDEN, D_MODEL)
    return w_gate, w_up, wd


def kernel(x, ev_norm, ev_w_in, ev_b_forget, ev_w_s, ev_b_s, ev_g_v, ev_g_q, ev_g_k, ev_w_out,
           od_norm, od_w_in, od_conv_w, od_w_pool, od_pool_scale, od_w_out,
           moe_norm, moe_w_group, moe_b_group, moe_w_router, moe_b_router,
           moe_w_gate, moe_w_up, moe_w_down):
    bsz, seq, d = x.shape
    t = bsz * seq
    x2 = x.reshape(t, d)

    w_in_t = ev_w_in[0].T
    w_f = w_in_t[EVEN_MAIN:].T
    w_f3 = jnp.pad(jnp.tile(w_f, (1, 3)), ((0, 0), (0, LANES - 3 * FOX_HEADS))).astype(BF16)
    bf = jnp.pad(jnp.tile(ev_b_forget[0], 3), (0, LANES - 3 * FOX_HEADS)).reshape(1, LANES)
    tril = jnp.tril(jnp.ones((CHUNK, CHUNK), F32))
    wtril = (ev_w_s[0] * tril).astype(BF16)
    bmat = jnp.repeat(ev_b_s[0].T, GMLP_HEAD, axis=1)
    gv = ev_g_v[0].reshape(1, GMLP_WIDTH)
    gq = jnp.broadcast_to((jnp.tile(ev_g_q[0], 2) * (HEAD_DIM ** -0.5 * LOG2E))[:, None],
                          (LANES, TM // SUBTILES))
    gk = jnp.tile(ev_g_k[0], FOX_HEADS).reshape(1, FOX_WIDTH)
    blk = jnp.arange(MXU_DIM) // HEAD_DIM
    bd = jnp.where(blk[:, None] == blk[None, :], 1.0 / HEAD_DIM, 0.0).astype(BF16)
    ltri = jnp.tril(jnp.ones((TM // SUBTILES, TM // SUBTILES), F32)).astype(BF16)

    a, qt, kn, vt, f3, fq = _even_in(x2, ev_norm[0].reshape(1, d), w_in_t, w_f3, bf, wtril,
                                     bmat, gv, gq, gk, bd, ltri, seq)
    bound = (HEAD_DIM ** 0.5 * LOG2E * 1.01) * jnp.max(jnp.abs(ev_g_q[0])) * jnp.max(
        jnp.abs(ev_g_k[0]))
    o = _fox(qt, kn, f3, vt, fq, bound, bsz, seq)

    wr0, br0 = _router_params(moe_w_group[0], moe_b_group[0], moe_w_router[0], moe_b_router[0])
    x1, xn1, comb1 = _even_out(a, o, x2, ev_w_out[0],
                               moe_norm[0].reshape(1, d), wr0, br0)
    moe_w = _moe_params(moe_w_gate, moe_w_up, moe_w_down)
    xa = _moe(xn1, *moe_w, comb1, x1, 0)

    wr1, br1 = _router_params(moe_w_group[1], moe_b_group[1], moe_w_router[1], moe_b_router[1])
    x3, xn3, comb3 = _odd(xa, od_norm[0].reshape(1, d), od_w_in[0], od_conv_w[0],
                          od_w_pool[0].astype(BF16), od_pool_scale[0].reshape(1, POOL_WIDTH),
                          od_w_out[0], moe_norm[1].reshape(1, d), wr1, br1, seq)
    xb = _moe(xn3, *moe_w, comb3, x3, 1)
    return xb.reshape(bsz, seq, d)
```

```python
import functools

import jax
import jax.numpy as jnp
from jax import lax
from jax.experimental import pallas as pl
from jax.experimental.pallas import tpu as pltpu

F32 = jnp.float32
BF16 = jnp.bfloat16

D_MODEL = 1024
EPS = 1e-6
CHUNK = 128
GMLP_GROUPS = 4
GMLP_HEAD = 64
GMLP_WIDTH = 256
FOX_HEADS = 12
HEAD_DIM = 64
FOX_WIDTH = 768
CONV_WIDTH = 512
CONV_K = 3
POOL_WINDOWS = (2, 4, 8, 16)
POOL_GROUP = 128
POOL_WIDTH = 512
N_GROUPS = 4
EXPERTS_PER_GROUP = 4
N_EXPERTS = 16
EXPERT_HIDDEN = 256

LANES = 128
MXU_DIM = 256
V7X_VMEM_LIMIT_BYTES = 56 * 1024 * 1024

TM = 512
TQ = 512
TQ_ONLINE = 256
Z_SLOTS = 2
TM_MOE = 1024
SUBTILES = 2
RING_DEPTH = 3
EVEN_MAIN = 2 * GMLP_WIDTH + 3 * FOX_WIDTH
NEG = -1e30
LOG2E = 1.4426950408889634
FOX_FAST_BOUND_LOG2 = 50.0


def _cparams(sem):
    return pltpu.CompilerParams(dimension_semantics=sem,
                                vmem_limit_bytes=V7X_VMEM_LIMIT_BYTES)


def _dot(a, b):
    return jnp.dot(a, b, preferred_element_type=F32)


def _split_bf16(x, terms):
    parts = []
    r = x
    for _ in range(terms):
        p = r.astype(BF16)
        parts.append(p)
        r = r - p.astype(F32)
    return parts


def _dot_split(x, w_bf16, terms=2, w_left=False):
    acc = None
    for p in _split_bf16(x, terms):
        d = _dot(w_bf16, p) if w_left else _dot(p, w_bf16)
        acc = d if acc is None else acc + d
    return acc


def _rms_rows(x, g):
    ms = jnp.mean(x * x, axis=-1, keepdims=True)
    return x * lax.rsqrt(ms + EPS) * g


def _head_rms(x, bd, g):
    outs = []
    for c in range(x.shape[1] // MXU_DIM):
        xc = x[:, c * MXU_DIM:(c + 1) * MXU_DIM]
        ms = _dot_split(xc * xc, bd)
        outs.append(xc * lax.rsqrt(ms + EPS))
    y = outs[0] if len(outs) == 1 else jnp.concatenate(outs, axis=1)
    return y * g


def _log_sigmoid(x):
    return -(jnp.maximum(-x, 0.0) + jnp.log(1.0 + jnp.exp(-jnp.abs(x))))


def _router(xn_bf16, wr_ref, br_ref):
    r = _dot(xn_bf16, wr_ref[...]) + br_ref[...]
    tm = r.shape[0]
    rt = r.T
    er = rt[0:N_EXPERTS]
    gr = rt[N_EXPERTS:N_EXPERTS + 8]
    grow = lax.broadcasted_iota(jnp.int32, (8, tm), 0).astype(F32)
    erow = lax.broadcasted_iota(jnp.int32, (N_EXPERTS, tm), 0).astype(F32)
    is_g = grow < float(N_GROUPS)
    gl = jnp.where(is_g, gr, NEG)
    gmax = jnp.max(gl, axis=0, keepdims=True)
    gidx = jnp.min(jnp.where(gl == gmax, grow, 999.0), axis=0, keepdims=True)
    gsum = jnp.sum(jnp.where(is_g, jnp.exp(gl - gmax), 0.0), axis=0, keepdims=True)
    gp = 1.0 / gsum
    lo = gidx * float(EXPERTS_PER_GROUP)
    sel = (erow >= lo) & (erow < lo + float(EXPERTS_PER_GROUP))
    el = jnp.where(sel, er, NEG)
    emax = jnp.max(el, axis=0, keepdims=True)
    ee = jnp.where(sel, jnp.exp(el - emax), 0.0)
    ep = ee / jnp.sum(ee, axis=0, keepdims=True)
    ep = jnp.where(sel, ep, -1.0)
    p1 = jnp.max(ep, axis=0, keepdims=True)
    i1 = jnp.min(jnp.where(ep == p1, erow, 999.0), axis=0, keepdims=True)
    ep2 = jnp.where(erow == i1, -1.0, ep)
    p2 = jnp.max(ep2, axis=0, keepdims=True)
    i2 = jnp.min(jnp.where(ep2 == p2, erow, 999.0), axis=0, keepdims=True)
    den = p1 + p2
    comb_t = (jnp.where(erow == i1, gp * (p1 / den), 0.0)
              + jnp.where(erow == i2, gp * (p2 / den), 0.0))
    comb_t = jnp.concatenate([comb_t, jnp.zeros((LANES - N_EXPERTS, tm), F32)], axis=0)
    return comb_t.T


def _even_in_kernel(tiles_per_seq, x_ref, g_ref, w_ref, wf_ref, bf_ref, wtril_ref, bmat_ref,
                    gv_ref, gq_ref, gk_ref, bd_ref, ltri_ref,
                    a_ref, q_ref, k_ref, v_ref, f_ref, fq_ref, wbf_ref, carry_ref):
    i = pl.program_id(0)
    tm = x_ref.shape[0]

    @pl.when(i == 0)
    def _():
        for c in range(0, w_ref.shape[0], MXU_DIM):
            wbf_ref[:, c:c + MXU_DIM] = w_ref[c:c + MXU_DIM, :].T.astype(BF16)

    @pl.when(i % tiles_per_seq == 0)
    def _():
        carry_ref[...] = jnp.zeros_like(carry_ref)

    sub = tm // SUBTILES
    lane = lax.broadcasted_iota(jnp.int32, (sub, LANES), 1)
    lo_half = lane < GMLP_HEAD
    bmat = bmat_ref[...]
    bd = bd_ref[...]
    xn_all = _rms_rows(x_ref[...], g_ref[...]).astype(BF16)
    u_all = _dot(xn_all, wbf_ref[:, 0:256])
    v_all = _dot(xn_all, wbf_ref[:, 256:512])
    q_all = _dot(xn_all, wbf_ref[:, 512:1280])
    k_all = _dot(xn_all, wbf_ref[:, 1280:2048])
    val_all = _dot(xn_all, wbf_ref[:, 2048:2816])
    f_all = _dot(xn_all, wf_ref[...])
    for st in range(SUBTILES):
        rows = slice(st * sub, (st + 1) * sub)
        u, v, q, k = u_all[rows], v_all[rows], q_all[rows], k_all[rows]
        val, f = val_all[rows], f_all[rows]

        gu = jax.nn.gelu(u)
        vn = _head_rms(jax.nn.gelu(v), bd, gv_ref[...])
        pair_out = []
        for pr in range(2):
            vp = vn[:, pr * 128:(pr + 1) * 128]
            v_lo = jnp.where(lo_half, vp, 0.0).astype(BF16)
            v_hi = jnp.where(lo_half, 0.0, vp).astype(BF16)
            chunks = []
            for c in range(sub // CHUNK):
                rs = slice(c * CHUNK, (c + 1) * CHUNK)
                s = (_dot(wtril_ref[2 * pr], v_lo[rs]) + _dot(wtril_ref[2 * pr + 1], v_hi[rs])
                     + bmat[:, pr * 128:(pr + 1) * 128])
                chunks.append(s)
            pair_out.append(jnp.concatenate(chunks, axis=0))
        s_all = jnp.concatenate(pair_out, axis=1)
        a_ref[rows, :] = (gu * s_all).astype(BF16)

        for pr in range(FOX_HEADS // 2):
            ls = slice(pr * LANES, (pr + 1) * LANES)
            qt = q[:, ls].T
            heads = []
            for hh in range(2):
                xh = qt[hh * HEAD_DIM:(hh + 1) * HEAD_DIM]
                ms = jnp.sum(xh * xh, axis=0, keepdims=True) * (1.0 / HEAD_DIM)
                heads.append(xh * lax.rsqrt(ms + EPS))
            q_ref[pr, :, rows] = (jnp.concatenate(heads, axis=0) * gq_ref[...]).astype(BF16)
            v_ref[pr, :, rows] = val[:, ls].T.astype(BF16)
        k_ref[rows, :] = _head_rms(k, bd, gk_ref[...]).astype(BF16)

        logf = _log_sigmoid(f + bf_ref[...])
        cum = (_dot_split(logf, ltri_ref[...], terms=3, w_left=True)
               + carry_ref[0:1, :])
        carry_ref[0:1, :] = cum[sub - 1:sub, :]
        cum2 = cum * LOG2E
        fq_ref[rows, :] = cum2
        hi, mid, lo = (p.astype(F32) for p in _split_bf16(cum2, 3))
        f_ref[rows, :] = jnp.where(
            lane < FOX_HEADS, hi,
            jnp.where(lane < 2 * FOX_HEADS, mid,
                      jnp.where(lane < 3 * FOX_HEADS, lo,
                                jnp.where(lane < 3 * FOX_HEADS + 3, 1.0, 0.0)))).astype(BF16)


def _even_in(x2, g, w_all, wf, bf, wtril, bmat, gv, gq, gk, bd, ltri, seq):
    t = x2.shape[0]
    n = t // TM
    full = lambda a: pl.BlockSpec(a.shape, lambda i: (0,) * a.ndim)
    row = lambda wdt: pl.BlockSpec((TM, wdt), lambda i: (i, 0))
    pairs = FOX_HEADS // 2
    colT = pl.BlockSpec((pairs, LANES, TM), lambda i: (0, 0, i))
    w_main = pl.BlockSpec((EVEN_MAIN, D_MODEL), lambda i: (0, 0), pipeline_mode=pl.Buffered(1))
    return pl.pallas_call(
        functools.partial(_even_in_kernel, seq // TM),
        grid=(n,),
        in_specs=[row(D_MODEL), full(g), w_main, full(wf), full(bf), full(wtril), full(bmat),
                  full(gv), full(gq), full(gk), full(bd), full(ltri)],
        out_specs=[row(GMLP_WIDTH), colT, row(FOX_WIDTH), colT, row(LANES), row(LANES)],
        out_shape=[jax.ShapeDtypeStruct((t, GMLP_WIDTH), BF16),
                   jax.ShapeDtypeStruct((pairs, LANES, t), BF16),
                   jax.ShapeDtypeStruct((t, FOX_WIDTH), BF16),
                   jax.ShapeDtypeStruct((pairs, LANES, t), BF16),
                   jax.ShapeDtypeStruct((t, LANES), BF16),
                   jax.ShapeDtypeStruct((t, LANES), F32)],
        scratch_shapes=[pltpu.VMEM((D_MODEL, EVEN_MAIN), BF16), pltpu.VMEM((8, LANES), F32)],
        compiler_params=_cparams(("arbitrary",)),
        name="even_in",
    )(x2, g, w_all, wf, bf, wtril, bmat, gv, gq, gk, bd, ltri)


def _fox_query_operand(qt, pr, hh, shift_terms=None):
    tq = qt.shape[1]
    row = lax.broadcasted_iota(jnp.int32, (LANES, tq), 0)
    head = 2 * pr + hh
    in_head = (row >= hh * HEAD_DIM) & (row < (hh + 1) * HEAD_DIM)
    f_rows = (row == head) | (row == head + FOX_HEADS) | (row == head + 2 * FOX_HEADS)
    extra = jnp.where(f_rows, -1.0, 0.0)
    if shift_terms is not None:
        for n, term in enumerate(shift_terms):
            extra = jnp.where(row == 3 * FOX_HEADS + n, term, extra)
    return jnp.concatenate([jnp.where(in_head, qt, 0.0).astype(BF16), extra.astype(BF16)],
                           axis=0)


def _fox_online_kernel(q_ref, k_ref, f_ref, v_ref, o_ref, m_sc, l_sc, acc_sc):
    pr = pl.program_id(1)
    i = pl.program_id(2)
    tq = q_ref.shape[2]
    qt = q_ref[0].astype(F32)
    rhs = [_fox_query_operand(qt, pr, hh) for hh in range(2)]

    m_sc[...] = jnp.full_like(m_sc, NEG)
    l_sc[...] = jnp.zeros_like(l_sc)
    acc_sc[...] = jnp.zeros_like(acc_sc)

    def step(j, masked):
        start = pl.multiple_of(j * tq, tq)
        kaug = jnp.concatenate([k_ref[pl.ds(start, tq), :], f_ref[pl.ds(start, tq), :]],
                               axis=1)
        for hh in range(2):
            s = _dot(kaug, rhs[hh])
            if masked:
                r_id = lax.broadcasted_iota(jnp.int32, (tq, tq), 0)
                c_id = lax.broadcasted_iota(jnp.int32, (tq, tq), 1)
                s = jnp.where(r_id <= c_id, s, NEG)
            m_prev = m_sc[hh]
            m_new = jnp.maximum(m_prev, jnp.max(s, axis=0, keepdims=True))
            alpha = jnp.exp2(m_prev - m_new)
            p = jnp.exp2(s - m_new)
            l_sc[hh] = alpha * l_sc[hh] + jnp.sum(p, axis=0, keepdims=True)
            vt = v_ref[0, hh * HEAD_DIM:(hh + 1) * HEAD_DIM, pl.ds(start, tq)]
            acc_sc[hh] = alpha * acc_sc[hh] + _dot(vt, p.astype(BF16))
            m_sc[hh] = m_new

    def body(j, c):
        step(j, False)
        return c

    lax.fori_loop(0, i, body, 0)
    step(i, True)
    ot = jnp.concatenate([acc_sc[0] / l_sc[0], acc_sc[1] / l_sc[1]], axis=0)
    o_ref[...] = ot.T.astype(o_ref.dtype)


def _fox_fixed_kernel(b_ref, q_ref, k_ref, f_ref, v_ref, fq_ref, o_ref,
                      rhs_sc, z_sc, acc_sc, mask_sc):
    pr = pl.program_id(1)
    tq = z_sc.shape[3]
    tk = z_sc.shape[2]
    nq = rhs_sc.shape[0]
    row = lax.broadcasted_iota(jnp.int32, (LANES, tq), 0)
    for i in range(nq):
        qt = q_ref[0, :, i * tq:(i + 1) * tq].astype(F32)
        fqt = fq_ref[i * tq:(i + 1) * tq, :].T
        for hh in range(2):
            ft = jnp.sum(jnp.where(row == 2 * pr + hh, fqt, 0.0), axis=0, keepdims=True)
            shift = [p.astype(F32) for p in _split_bf16(ft - b_ref[...], 3)]
            rhs_sc[i, hh] = _fox_query_operand(qt, pr, hh, shift)

    mask_sc[...] = jnp.where(lax.broadcasted_iota(jnp.int32, (tk, tq), 0)
                             > lax.broadcasted_iota(jnp.int32, (tk, tq), 1), NEG, 0.0)
    acc_sc[...] = jnp.zeros_like(acc_sc)
    ones = jnp.ones((acc_sc.shape[2] - HEAD_DIM, tk), BF16)

    half = tk // 2

    def scores(i, j, slot):
        rows = slice(j * tk, (j + 1) * tk)
        kaug = jnp.concatenate([k_ref[rows, :], f_ref[rows, :]], axis=1)
        for hh in range(2):
            if i == j:
                z_sc[slot, hh, 0:half, :] = _dot(kaug[0:half], rhs_sc[i, hh])
                z_sc[slot, hh, half:tk, half:tq] = _dot(kaug[half:tk], rhs_sc[i, hh, :, half:tq])
            else:
                z_sc[slot, hh] = _dot(kaug, rhs_sc[i, hh])

    def values(hh, start, size):
        return jnp.concatenate(
            [v_ref[0, hh * HEAD_DIM:(hh + 1) * HEAD_DIM, start:start + size], ones[:, 0:size]],
            axis=0)

    def consume(i, j, slot):
        for hh in range(2):
            p = jnp.exp2(z_sc[slot, hh]).astype(BF16)
            acc_sc[i, hh] += _dot(values(hh, j * tk, tk), p)

    def consume_diagonal(i, slot):
        outs = []
        for hh in range(2):
            p_top = jnp.exp2(z_sc[slot, hh, 0:half, :] + mask_sc[0:half, :]).astype(BF16)
            p_bot = jnp.exp2(z_sc[slot, hh, half:tk, half:tq]
                             + mask_sc[half:tk, half:tq]).astype(BF16)
            acc = acc_sc[i, hh] + _dot(values(hh, i * tk, half), p_top)
            right = acc[:, half:tq] + _dot(values(hh, i * tk + half, half), p_bot)
            acc = jnp.concatenate([acc[:, 0:half], right], axis=1)
            outs.append(acc[0:HEAD_DIM] / acc[HEAD_DIM:HEAD_DIM + 1])
        o_ref[i * tq:(i + 1) * tq, :] = jnp.concatenate(outs, axis=0).T.astype(o_ref.dtype)

    items = [(i, j) for i in range(1, nq) for j in range(i)] + [(i, i) for i in range(nq)]
    scores(*items[0], 0)
    for s, (i, j) in enumerate(items):
        if s + 1 < len(items):
            scores(*items[s + 1], (s + 1) % 2)
        if i == j:
            consume_diagonal(i, s % 2)
        else:
            consume(i, j, s % 2)


def _fox(qt, k, f3, vt, fq, bound, bsz, seq):
    t = k.shape[0]
    pairs = FOX_HEADS // 2
    out_shape = jax.ShapeDtypeStruct((t, FOX_WIDTH), BF16)
    sem = _cparams(("parallel", "parallel", "arbitrary"))

    def common(tq):
        nq = seq // tq
        in_specs = [pl.BlockSpec((1, LANES, tq), lambda b, p, i: (p, 0, b * nq + i)),
                    pl.BlockSpec((seq, LANES), lambda b, p, i: (b, p)),
                    pl.BlockSpec((seq, LANES), lambda b, p, i: (b, 0)),
                    pl.BlockSpec((1, LANES, seq), lambda b, p, i: (p, 0, b))]
        return nq, in_specs, pl.BlockSpec((tq, LANES), lambda b, p, i: (b * nq + i, p))

    def fixed(brow):
        nq = seq // TQ
        return pl.pallas_call(
            _fox_fixed_kernel,
            grid=(bsz, pairs),
            in_specs=[pl.BlockSpec((1, TQ), lambda b, p: (0, 0)),
                      pl.BlockSpec((1, LANES, seq), lambda b, p: (p, 0, b)),
                      pl.BlockSpec((seq, LANES), lambda b, p: (b, p)),
                      pl.BlockSpec((seq, LANES), lambda b, p: (b, 0)),
                      pl.BlockSpec((1, LANES, seq), lambda b, p: (p, 0, b)),
                      pl.BlockSpec((seq, LANES), lambda b, p: (b, 0))],
            out_specs=pl.BlockSpec((seq, LANES), lambda b, p: (b, p)),
            out_shape=out_shape,
            scratch_shapes=[pltpu.VMEM((nq, 2, 2 * LANES, TQ), BF16),
                            pltpu.VMEM((Z_SLOTS, 2, TQ, TQ), F32),
                            pltpu.VMEM((nq, 2, HEAD_DIM + 16, TQ), F32),
                            pltpu.VMEM((TQ, TQ), F32)],
            compiler_params=_cparams(("parallel", "parallel")), name="fox_attention_fixed",
        )(brow, qt, k, f3, vt, fq)

    def online(brow):
        del brow
        nq, in_specs, out_spec = common(TQ_ONLINE)
        stats = pltpu.VMEM((2, 1, TQ_ONLINE), F32)
        return pl.pallas_call(
            _fox_online_kernel,
            grid=(bsz, pairs, nq),
            in_specs=in_specs, out_specs=out_spec, out_shape=out_shape,
            scratch_shapes=[stats, stats, pltpu.VMEM((2, HEAD_DIM, TQ_ONLINE), F32)],
            compiler_params=sem, name="fox_attention_online",
        )(qt, k, f3, vt)

    brow = jnp.full((1, TQ), bound, F32)
    return lax.cond(bound <= FOX_FAST_BOUND_LOG2, fixed, online, brow)


def _even_out_kernel(a_hbm, o_hbm, x_hbm, w32_ref, gm_ref, wr_ref, br_ref,
                     x1_ref, xn_ref, comb_ref, w_ref, a_buf, o_buf, x_buf, sems):
    i = pl.program_id(0)
    n = pl.num_programs(0)
    tm = x1_ref.shape[0]

    def copies(step, slot):
        rows = pl.ds(pl.multiple_of(step * tm, tm), tm)
        return (pltpu.make_async_copy(a_hbm.at[rows], a_buf.at[slot], sems.at[0, slot]),
                pltpu.make_async_copy(o_hbm.at[rows], o_buf.at[slot], sems.at[1, slot]),
                pltpu.make_async_copy(x_hbm.at[rows], x_buf.at[slot], sems.at[2, slot]))

    @pl.when(i == 0)
    def _():
        for s in range(RING_DEPTH - 1):
            for c in copies(s, s):
                c.start()
        w_ref[...] = w32_ref[...].astype(BF16)

    ahead = i + (RING_DEPTH - 1)

    @pl.when(ahead < n)
    def _():
        for c in copies(ahead, ahead % RING_DEPTH):
            c.start()

    slot = i % RING_DEPTH
    for c in copies(i, slot):
        c.wait()

    y_all = _dot(jnp.concatenate([a_buf[slot], o_buf[slot]], axis=1), w_ref[...])
    sub = tm // SUBTILES
    for s in range(SUBTILES):
        rs = slice(s * sub, (s + 1) * sub)
        x1 = x_buf[slot, rs, :] + y_all[rs]
        x1_ref[rs, :] = x1
        xn = _rms_rows(x1, gm_ref[...]).astype(BF16)
        xn_ref[rs, :] = xn
        comb_ref[rs, :] = _router(xn, wr_ref, br_ref)


def _even_out(a, o, x2, w, gm, wr, br):
    t = x2.shape[0]
    n = t // TM
    full = lambda arr: pl.BlockSpec(arr.shape, lambda i: (0,) * arr.ndim)
    row = lambda wdt: pl.BlockSpec((TM, wdt), lambda i: (i, 0))
    assert n >= RING_DEPTH
    hbm = pl.BlockSpec(memory_space=pl.ANY)
    return pl.pallas_call(
        _even_out_kernel,
        grid=(n,),
        in_specs=[hbm, hbm, hbm,
                  pl.BlockSpec(w.shape, lambda i: (0, 0), pipeline_mode=pl.Buffered(1)),
                  full(gm), full(wr), full(br)],
        out_specs=[row(D_MODEL), row(D_MODEL), row(LANES)],
        out_shape=[jax.ShapeDtypeStruct((t, D_MODEL), F32),
                   jax.ShapeDtypeStruct((t, D_MODEL), BF16),
                   jax.ShapeDtypeStruct((t, LANES), F32)],
        scratch_shapes=[pltpu.VMEM(w.shape, BF16),
                        pltpu.VMEM((RING_DEPTH, TM, GMLP_WIDTH), BF16),
                        pltpu.VMEM((RING_DEPTH, TM, FOX_WIDTH), BF16),
                        pltpu.VMEM((RING_DEPTH, TM, D_MODEL), F32),
                        pltpu.SemaphoreType.DMA((3, RING_DEPTH))],
        compiler_params=_cparams(("arbitrary",)),
        name="even_out",
    )(a, o, x2, w, gm, wr, br)


def _odd_kernel(tiles_per_seq, x_ref, g_ref, w32_ref, cw_ref, wp_ref, ps_ref, wo32_ref, gm_ref,
                wr_ref, br_ref, x1_ref, xn_ref, comb_ref, w_ref, wo_ref, zbuf, pbuf):
    i = pl.program_id(0)
    tm = x_ref.shape[0]
    zpad = zbuf.shape[0] - tm
    ppad = pbuf.shape[0] - tm

    @pl.when(i == 0)
    def _():
        w_ref[...] = w32_ref[...].astype(BF16)
        wo_ref[...] = wo32_ref[...].astype(BF16)

    @pl.when(i % tiles_per_seq == 0)
    def _():
        zbuf[0:zpad, :] = jnp.zeros((zpad, CONV_WIDTH), F32)
        pbuf[0:ppad, :] = jnp.zeros((ppad, POOL_WIDTH), F32)

    cw = cw_ref[...]
    ps = ps_ref[...]
    sub = tm // SUBTILES
    xn_all = _rms_rows(x_ref[...], g_ref[...]).astype(BF16)
    cg_all = _dot(xn_all, w_ref[:, 512:1024])
    hc_all = _dot(xn_all, w_ref[:, 1024:1536])
    p_all = _dot(xn_all, w_ref[:, 1536:2048])
    bg_all = _dot(xn_all, w_ref[:, 0:512])
    for s in range(SUBTILES):
        r0 = s * sub
        rs = slice(r0, r0 + sub)
        x = x_ref[rs, :]
        cg, hc, p, bg = cg_all[rs], hc_all[rs], p_all[rs], bg_all[rs]

        z = cg * hc
        zbuf[zpad + r0:zpad + r0 + sub, :] = z
        y = (cw[0:1, :] * zbuf[zpad + r0 - 2:zpad + r0 - 2 + sub, :]
             + cw[1:2, :] * zbuf[zpad + r0 - 1:zpad + r0 - 1 + sub, :]
             + cw[2:3, :] * z)
        c = bg * y

        pbuf[ppad + r0:ppad + r0 + sub, :] = p
        pos = ((i % tiles_per_seq) * tm + r0
               + lax.broadcasted_iota(jnp.int32, (sub, POOL_GROUP), 0)).astype(F32) + 1.0
        pooled_out = []
        for gi, win in enumerate(POOL_WINDOWS):
            ls = slice(gi * POOL_GROUP, (gi + 1) * POOL_GROUP)
            pg = p[:, ls]
            sm = pbuf[r0:r0 + ppad + sub, ls]
            span = 1
            while span < win:
                sm = sm + pltpu.roll(sm, span, axis=0)
                span *= 2
            sm = sm[ppad:]
            cnt = jnp.minimum(pos, float(win))
            pooled = sm / cnt - pg
            pooled_out.append((_dot(pooled.astype(BF16), wp_ref[gi]) * ps[:, ls]).astype(BF16))
        mix = jnp.concatenate([c.astype(BF16)] + pooled_out, axis=1)
        acc = x + _dot(mix, wo_ref[...])

        x1_ref[rs, :] = acc
        xn2 = _rms_rows(acc, gm_ref[...]).astype(BF16)
        xn_ref[rs, :] = xn2
        comb_ref[rs, :] = _router(xn2, wr_ref, br_ref)

    zbuf[0:zpad, :] = zbuf[tm:tm + zpad, :]
    pbuf[0:ppad, :] = pbuf[tm:tm + ppad, :]


def _odd(x2, g, w, cw, wp, ps, wo, gm, wr, br, seq):
    t = x2.shape[0]
    n = t // TM
    full = lambda arr: pl.BlockSpec(arr.shape, lambda i: (0,) * arr.ndim)
    once = lambda arr: pl.BlockSpec(arr.shape, lambda i: (0,) * arr.ndim,
                                    pipeline_mode=pl.Buffered(1))
    row = lambda wdt: pl.BlockSpec((TM, wdt), lambda i: (i, 0))
    return pl.pallas_call(
        functools.partial(_odd_kernel, seq // TM),
        grid=(n,),
        in_specs=[row(D_MODEL), full(g), once(w), full(cw), full(wp), full(ps), once(wo),
                  full(gm), full(wr), full(br)],
        out_specs=[row(D_MODEL), row(D_MODEL), row(LANES)],
        out_shape=[jax.ShapeDtypeStruct((t, D_MODEL), F32),
                   jax.ShapeDtypeStruct((t, D_MODEL), BF16),
                   jax.ShapeDtypeStruct((t, LANES), F32)],
        scratch_shapes=[pltpu.VMEM(w.shape, BF16), pltpu.VMEM(wo.shape, BF16),
                        pltpu.VMEM((TM + 8, CONV_WIDTH), F32),
                        pltpu.VMEM((TM + 16, POOL_WIDTH), F32)],
        compiler_params=_cparams(("arbitrary",)),
        name="odd_mixer",
    )(x2, g, w, cw, wp, ps, wo, gm, wr, br)


def _moe_kernel(xn_ref, wg_ref, wu_ref, wd_ref, comb_ref, x1_ref, o_ref):
    g = pl.program_id(1)

    @pl.when(g == 0)
    def _():
        o_ref[...] = x1_ref[...]

    tm = xn_ref.shape[0]
    xn = xn_ref[...]
    lane = lax.broadcasted_iota(jnp.int32, (tm, LANES), 1)
    comb = comb_ref[...]
    hid = []
    for e in range(EXPERTS_PER_GROUP):
        gate = _dot(xn, wg_ref[e].astype(BF16))
        up = _dot(xn, wu_ref[e].astype(BF16))
        cw = jnp.sum(jnp.where(lane == g * EXPERTS_PER_GROUP + e, comb, 0.0), axis=1,
                     keepdims=True)
        hid.append(((gate * jax.nn.sigmoid(gate)) * up * cw).astype(BF16))
    o_ref[...] += _dot(jnp.concatenate(hid, axis=1), wd_ref[0].astype(BF16))


def _moe(xn, wg, wu, wd, comb, x1, layer):
    t = xn.shape[0]
    n = t // TM_MOE
    gh = EXPERTS_PER_GROUP * EXPERT_HIDDEN
    wblk = pl.BlockSpec((None, EXPERTS_PER_GROUP, D_MODEL, EXPERT_HIDDEN),
                        lambda i, g: (layer, g, 0, 0))
    return pl.pallas_call(
        _moe_kernel,
        grid=(n, N_GROUPS),
        in_specs=[pl.BlockSpec((TM_MOE, D_MODEL), lambda i, g: (i, 0)),
                  wblk, wblk,
                  pl.BlockSpec((None, 1, gh, D_MODEL), lambda i, g: (layer, g, 0, 0)),
                  pl.BlockSpec((TM_MOE, LANES), lambda i, g: (i, 0)),
                  pl.BlockSpec((TM_MOE, D_MODEL), lambda i, g: (i, 0))],
        out_specs=pl.BlockSpec((TM_MOE, D_MODEL), lambda i, g: (i, 0)),
        out_shape=jax.ShapeDtypeStruct((t, D_MODEL), F32),
        compiler_params=_cparams(("parallel", "arbitrary")),
        name="moe",
    )(xn, wg, wu, wd, comb, x1)


def _router_params(w_group, b_group, w_router, b_router):
    wr = jnp.concatenate([w_router.reshape(D_MODEL, N_EXPERTS), w_group], axis=1)
    wr = jnp.pad(wr, ((0, 0), (0, LANES - wr.shape[1]))).astype(BF16)
    br = jnp.concatenate([b_router.reshape(N_EXPERTS), b_group])
    br = jnp.pad(br, (0, LANES - br.shape[0])).reshape(1, LANES).astype(F32)
    return wr, br


def _moe_params(w_gate, w_up, w_down):
    wd = w_down.reshape(w_down.shape[0], N_GROUPS, EXPERTS_PER_GROUP * EXPERT_HIDDEN, D_MODEL)
    return w_gate, w_up, wd


def kernel(x, ev_norm, ev_w_in, ev_b_forget, ev_w_s, ev_b_s, ev_g_v, ev_g_q, ev_g_k, ev_w_out,
           od_norm, od_w_in, od_conv_w, od_w_pool, od_pool_scale, od_w_out,
           moe_norm, moe_w_group, moe_b_group, moe_w_router, moe_b_router,
           moe_w_gate, moe_w_up, moe_w_down):
    bsz, seq, d = x.shape
    t = bsz * seq
    x2 = x.reshape(t, d)

    w_in_t = ev_w_in[0].T
    w_f = w_in_t[EVEN_MAIN:].T
    w_f3 = jnp.pad(jnp.tile(w_f, (1, 3)), ((0, 0), (0, LANES - 3 * FOX_HEADS))).astype(BF16)
    bf = jnp.pad(jnp.tile(ev_b_forget[0], 3), (0, LANES - 3 * FOX_HEADS)).reshape(1, LANES)
    tril = jnp.tril(jnp.ones((CHUNK, CHUNK), F32))
    wtril = (ev_w_s[0] * tril).astype(BF16)
    bmat = jnp.repeat(ev_b_s[0].T, GMLP_HEAD, axis=1)
    gv = ev_g_v[0].reshape(1, GMLP_WIDTH)
    gq = jnp.broadcast_to((jnp.tile(ev_g_q[0], 2) * (HEAD_DIM ** -0.5 * LOG2E))[:, None],
                          (LANES, TM // SUBTILES))
    gk = jnp.tile(ev_g_k[0], FOX_HEADS).reshape(1, FOX_WIDTH)
    blk = jnp.arange(MXU_DIM) // HEAD_DIM
    bd = jnp.where(blk[:, None] == blk[None, :], 1.0 / HEAD_DIM, 0.0).astype(BF16)
    ltri = jnp.tril(jnp.ones((TM // SUBTILES, TM // SUBTILES), F32)).astype(BF16)

    a, qt, kn, vt, f3, fq = _even_in(x2, ev_norm[0].reshape(1, d), w_in_t, w_f3, bf, wtril,
                                     bmat, gv, gq, gk, bd, ltri, seq)
    bound = (HEAD_DIM ** 0.5 * LOG2E * 1.01) * jnp.max(jnp.abs(ev_g_q[0])) * jnp.max(
        jnp.abs(ev_g_k[0]))
    o = _fox(qt, kn, f3, vt, fq, bound, bsz, seq)

    wr0, br0 = _router_params(moe_w_group[0], moe_b_group[0], moe_w_router[0], moe_b_router[0])
    x1, xn1, comb1 = _even_out(a, o, x2, ev_w_out[0],
                               moe_norm[0].reshape(1, d), wr0, br0)
    moe_w = _moe_params(moe_w_gate, moe_w_up, moe_w_down)
    xa = _moe(xn1, *moe_w, comb1, x1, 0)

    wr1, br1 = _router_params(moe_w_group[1], moe_b_group[1], moe_w_router[1], moe_b_router[1])
    x3, xn3, comb3 = _odd(xa, od_norm[0].reshape(1, d), od_w_in[0], od_conv_w[0],
                          od_w_pool[0].astype(BF16), od_pool_scale[0].reshape(1, POOL_WIDTH),
                          od_w_out[0], moe_norm[1].reshape(1, d), wr1, br1, seq)
    xb = _moe(xn3, *moe_w, comb3, x3, 1)
    return xb.reshape(bsz, seq, d)
```

```python
import functools

import jax
import jax.numpy as jnp
from jax import lax
from jax.experimental import pallas as pl
from jax.experimental.pallas import tpu as pltpu

F32 = jnp.float32
BF16 = jnp.bfloat16

D_MODEL = 1024
EPS = 1e-6
CHUNK = 128
GMLP_GROUPS = 4
GMLP_HEAD = 64
GMLP_WIDTH = 256
FOX_HEADS = 12
HEAD_DIM = 64
FOX_WIDTH = 768
CONV_WIDTH = 512
CONV_K = 3
POOL_WINDOWS = (2, 4, 8, 16)
POOL_GROUP = 128
POOL_WIDTH = 512
N_GROUPS = 4
EXPERTS_PER_GROUP = 4
N_EXPERTS = 16
EXPERT_HIDDEN = 256

LANES = 128
MXU_DIM = 256
V7X_VMEM_LIMIT_BYTES = 56 * 1024 * 1024

TM = 512
TQ = 512
TQ_ONLINE = 256
Z_SLOTS = 2
TM_MOE = 1024
SUBTILES = 2
RING_DEPTH = 3
EVEN_MAIN = 2 * GMLP_WIDTH + 3 * FOX_WIDTH
NEG = -1e30
LOG2E = 1.4426950408889634
FOX_FAST_BOUND_LOG2 = 50.0


def _cparams(sem):
    return pltpu.CompilerParams(dimension_semantics=sem,
                                vmem_limit_bytes=V7X_VMEM_LIMIT_BYTES)


def _dot(a, b):
    return jnp.dot(a, b, preferred_element_type=F32)


def _split_bf16(x, terms):
    parts = []
    r = x
    for _ in range(terms):
        p = r.astype(BF16)
        parts.append(p)
        r = r - p.astype(F32)
    return parts


def _dot_split(x, w_bf16, terms=2, w_left=False):
    acc = None
    for p in _split_bf16(x, terms):
        d = _dot(w_bf16, p) if w_left else _dot(p, w_bf16)
        acc = d if acc is None else acc + d
    return acc


def _rms_rows(x, g):
    ms = jnp.mean(x * x, axis=-1, keepdims=True)
    return x * lax.rsqrt(ms + EPS) * g


def _head_rms(x, bd, g):
    outs = []
    for c in range(x.shape[1] // MXU_DIM):
        xc = x[:, c * MXU_DIM:(c + 1) * MXU_DIM]
        ms = _dot_split(xc * xc, bd)
        outs.append(xc * lax.rsqrt(ms + EPS))
    y = outs[0] if len(outs) == 1 else jnp.concatenate(outs, axis=1)
    return y * g


def _log_sigmoid(x):
    return -(jnp.maximum(-x, 0.0) + jnp.log(1.0 + jnp.exp(-jnp.abs(x))))


def _router(xn_bf16, wr_ref, br_ref):
    r = _dot(xn_bf16, wr_ref[...]) + br_ref[...]
    tm = r.shape[0]
    rt = r.T
    er = rt[0:N_EXPERTS]
    gr = rt[N_EXPERTS:N_EXPERTS + 8]
    grow = lax.broadcasted_iota(jnp.int32, (8, tm), 0).astype(F32)
    erow = lax.broadcasted_iota(jnp.int32, (N_EXPERTS, tm), 0).astype(F32)
    is_g = grow < float(N_GROUPS)
    gl = jnp.where(is_g, gr, NEG)
    gmax = jnp.max(gl, axis=0, keepdims=True)
    gidx = jnp.min(jnp.where(gl == gmax, grow, 999.0), axis=0, keepdims=True)
    gsum = jnp.sum(jnp.where(is_g, jnp.exp(gl - gmax), 0.0), axis=0, keepdims=True)
    gp = 1.0 / gsum
    lo = gidx * float(EXPERTS_PER_GROUP)
    sel = (erow >= lo) & (erow < lo + float(EXPERTS_PER_GROUP))
    el = jnp.where(sel, er, NEG)
    emax = jnp.max(el, axis=0, keepdims=True)
    ee = jnp.where(sel, jnp.exp(el - emax), 0.0)
    ep = ee / jnp.sum(ee, axis=0, keepdims=True)
    ep = jnp.where(sel, ep, -1.0)
    p1 = jnp.max(ep, axis=0, keepdims=True)
    i1 = jnp.min(jnp.where(ep == p1, erow, 999.0), axis=0, keepdims=True)
    ep2 = jnp.where(erow == i1, -1.0, ep)
    p2 = jnp.max(ep2, axis=0, keepdims=True)
    i2 = jnp.min(jnp.where(ep2 == p2, erow, 999.0), axis=0, keepdims=True)
    den = p1 + p2
    comb_t = (jnp.where(erow == i1, gp * (p1 / den), 0.0)
              + jnp.where(erow == i2, gp * (p2 / den), 0.0))
    comb_t = jnp.concatenate([comb_t, jnp.zeros((LANES - N_EXPERTS, tm), F32)], axis=0)
    return comb_t.T


def _even_in_kernel(tiles_per_seq, x_ref, g_ref, w_ref, wf_ref, bf_ref, wtril_ref, bmat_ref,
                    gv_ref, gq_ref, gk_ref, bd_ref, ltri_ref,
                    a_ref, q_ref, k_ref, v_ref, f_ref, fq_ref, wbf_ref, carry_ref):
    i = pl.program_id(0)
    tm = x_ref.shape[0]

    @pl.when(i == 0)
    def _():
        for c in range(0, w_ref.shape[0], MXU_DIM):
            wbf_ref[:, c:c + MXU_DIM] = w_ref[c:c + MXU_DIM, :].T.astype(BF16)

    @pl.when(i % tiles_per_seq == 0)
    def _():
        carry_ref[...] = jnp.zeros_like(carry_ref)

    sub = tm // SUBTILES
    lane = lax.broadcasted_iota(jnp.int32, (sub, LANES), 1)
    lo_half = lane < GMLP_HEAD
    bmat = bmat_ref[...]
    bd = bd_ref[...]
    xn_all = _rms_rows(x_ref[...], g_ref[...]).astype(BF16)
    u_all = _dot(xn_all, wbf_ref[:, 0:256])
    v_all = _dot(xn_all, wbf_ref[:, 256:512])
    q_all = _dot(xn_all, wbf_ref[:, 512:1280])
    k_all = _dot(xn_all, wbf_ref[:, 1280:2048])
    val_all = _dot(xn_all, wbf_ref[:, 2048:2816])
    f_all = _dot(xn_all, wf_ref[...])
    for st in range(SUBTILES):
        rows = slice(st * sub, (st + 1) * sub)
        u, v, q, k = u_all[rows], v_all[rows], q_all[rows], k_all[rows]
        val, f = val_all[rows], f_all[rows]

        gu = jax.nn.gelu(u)
        vn = _head_rms(jax.nn.gelu(v), bd, gv_ref[...])
        pair_out = []
        for pr in range(2):
            vp = vn[:, pr * 128:(pr + 1) * 128]
            v_lo = jnp.where(lo_half, vp, 0.0).astype(BF16)
            v_hi = jnp.where(lo_half, 0.0, vp).astype(BF16)
            chunks = []
            for c in range(sub // CHUNK):
                rs = slice(c * CHUNK, (c + 1) * CHUNK)
                s = (_dot(wtril_ref[2 * pr], v_lo[rs]) + _dot(wtril_ref[2 * pr + 1], v_hi[rs])
                     + bmat[:, pr * 128:(pr + 1) * 128])
                chunks.append(s)
            pair_out.append(jnp.concatenate(chunks, axis=0))
        s_all = jnp.concatenate(pair_out, axis=1)
        a_ref[rows, :] = (gu * s_all).astype(BF16)

        for pr in range(FOX_HEADS // 2):
            ls = slice(pr * LANES, (pr + 1) * LANES)
            qt = q[:, ls].T
            heads = []
            for hh in range(2):
                xh = qt[hh * HEAD_DIM:(hh + 1) * HEAD_DIM]
                ms = jnp.sum(xh * xh, axis=0, keepdims=True) * (1.0 / HEAD_DIM)
                heads.append(xh * lax.rsqrt(ms + EPS))
            q_ref[pr, :, rows] = (jnp.concatenate(heads, axis=0) * gq_ref[...]).astype(BF16)
            v_ref[pr, :, rows] = val[:, ls].T.astype(BF16)
        k_ref[rows, :] = _head_rms(k, bd, gk_ref[...]).astype(BF16)

        logf = _log_sigmoid(f + bf_ref[...])
        cum = (_dot_split(logf, ltri_ref[...], terms=3, w_left=True)
               + carry_ref[0:1, :])
        carry_ref[0:1, :] = cum[sub - 1:sub, :]
        cum2 = cum * LOG2E
        fq_ref[rows, :] = cum2
        hi, mid, lo = (p.astype(F32) for p in _split_bf16(cum2, 3))
        f_ref[rows, :] = jnp.where(
            lane < FOX_HEADS, hi,
            jnp.where(lane < 2 * FOX_HEADS, mid,
                      jnp.where(lane < 3 * FOX_HEADS, lo,
                                jnp.where(lane < 3 * FOX_HEADS + 3, 1.0, 0.0)))).astype(BF16)


def _even_in(x2, g, w_all, wf, bf, wtril, bmat, gv, gq, gk, bd, ltri, seq):
    t = x2.shape[0]
    n = t // TM
    full = lambda a: pl.BlockSpec(a.shape, lambda i: (0,) * a.ndim)
    row = lambda wdt: pl.BlockSpec((TM, wdt), lambda i: (i, 0))
    pairs = FOX_HEADS // 2
    colT = pl.BlockSpec((pairs, LANES, TM), lambda i: (0, 0, i))
    w_main = pl.BlockSpec((EVEN_MAIN, D_MODEL), lambda i: (0, 0), pipeline_mode=pl.Buffered(1))
    return pl.pallas_call(
        functools.partial(_even_in_kernel, seq // TM),
        grid=(n,),
        in_specs=[row(D_MODEL), full(g), w_main, full(wf), full(bf), full(wtril), full(bmat),
                  full(gv), full(gq), full(gk), full(bd), full(ltri)],
        out_specs=[row(GMLP_WIDTH), colT, row(FOX_WIDTH), colT, row(LANES), row(LANES)],
        out_shape=[jax.ShapeDtypeStruct((t, GMLP_WIDTH), BF16),
                   jax.ShapeDtypeStruct((pairs, LANES, t), BF16),
                   jax.ShapeDtypeStruct((t, FOX_WIDTH), BF16),
                   jax.ShapeDtypeStruct((pairs, LANES, t), BF16),
                   jax.ShapeDtypeStruct((t, LANES), BF16),
                   jax.ShapeDtypeStruct((t, LANES), F32)],
        scratch_shapes=[pltpu.VMEM((D_MODEL, EVEN_MAIN), BF16), pltpu.VMEM((8, LANES), F32)],
        compiler_params=_cparams(("arbitrary",)),
        name="even_in",
    )(x2, g, w_all, wf, bf, wtril, bmat, gv, gq, gk, bd, ltri)


def _fox_query_operand(qt, pr, hh, shift_terms=None):
    tq = qt.shape[1]
    row = lax.broadcasted_iota(jnp.int32, (LANES, tq), 0)
    head = 2 * pr + hh
    in_head = (row >= hh * HEAD_DIM) & (row < (hh + 1) * HEAD_DIM)
    f_rows = (row == head) | (row == head + FOX_HEADS) | (row == head + 2 * FOX_HEADS)
    extra = jnp.where(f_rows, -1.0, 0.0)
    if shift_terms is not None:
        for n, term in enumerate(shift_terms):
            extra = jnp.where(row == 3 * FOX_HEADS + n, term, extra)
    return jnp.concatenate([jnp.where(in_head, qt, 0.0).astype(BF16), extra.astype(BF16)],
                           axis=0)


def _fox_online_kernel(q_ref, k_ref, f_ref, v_ref, o_ref, m_sc, l_sc, acc_sc):
    pr = pl.program_id(1)
    i = pl.program_id(2)
    tq = q_ref.shape[2]
    qt = q_ref[0].astype(F32)
    rhs = [_fox_query_operand(qt, pr, hh) for hh in range(2)]

    m_sc[...] = jnp.full_like(m_sc, NEG)
    l_sc[...] = jnp.zeros_like(l_sc)
    acc_sc[...] = jnp.zeros_like(acc_sc)

    def step(j, masked):
        start = pl.multiple_of(j * tq, tq)
        kaug = jnp.concatenate([k_ref[pl.ds(start, tq), :], f_ref[pl.ds(start, tq), :]],
                               axis=1)
        for hh in range(2):
            s = _dot(kaug, rhs[hh])
            if masked:
                r_id = lax.broadcasted_iota(jnp.int32, (tq, tq), 0)
                c_id = lax.broadcasted_iota(jnp.int32, (tq, tq), 1)
                s = jnp.where(r_id <= c_id, s, NEG)
            m_prev = m_sc[hh]
            m_new = jnp.maximum(m_prev, jnp.max(s, axis=0, keepdims=True))
            alpha = jnp.exp2(m_prev - m_new)
            p = jnp.exp2(s - m_new)
            l_sc[hh] = alpha * l_sc[hh] + jnp.sum(p, axis=0, keepdims=True)
            vt = v_ref[0, hh * HEAD_DIM:(hh + 1) * HEAD_DIM, pl.ds(start, tq)]
            acc_sc[hh] = alpha * acc_sc[hh] + _dot(vt, p.astype(BF16))
            m_sc[hh] = m_new

    def body(j, c):
        step(j, False)
        return c

    lax.fori_loop(0, i, body, 0)
    step(i, True)
    ot = jnp.concatenate([acc_sc[0] / l_sc[0], acc_sc[1] / l_sc[1]], axis=0)
    o_ref[...] = ot.T.astype(o_ref.dtype)


def _fox_fixed_kernel(b_ref, q_ref, k_ref, f_ref, v_ref, fq_ref, o_ref,
                      rhs_sc, z_sc, acc_sc, mask_sc):
    pr = pl.program_id(1)
    tq = z_sc.shape[3]
    tk = z_sc.shape[2]
    nq = rhs_sc.shape[0]
    row = lax.broadcasted_iota(jnp.int32, (LANES, tq), 0)
    for i in range(nq):
        qt = q_ref[0, :, i * tq:(i + 1) * tq].astype(F32)
        fqt = fq_ref[i * tq:(i + 1) * tq, :].T
        for hh in range(2):
            ft = jnp.sum(jnp.where(row == 2 * pr + hh, fqt, 0.0), axis=0, keepdims=True)
            shift = [p.astype(F32) for p in _split_bf16(ft - b_ref[...], 3)]
            rhs_sc[i, hh] = _fox_query_operand(qt, pr, hh, shift)

    mask_sc[...] = jnp.where(lax.broadcasted_iota(jnp.int32, (tk, tq), 0)
                             > lax.broadcasted_iota(jnp.int32, (tk, tq), 1), NEG, 0.0)
    acc_sc[...] = jnp.zeros_like(acc_sc)
    ones = jnp.ones((acc_sc.shape[2] - HEAD_DIM, tk), BF16)

    half = tk // 2

    def scores(i, j, slot):
        rows = slice(j * tk, (j + 1) * tk)
        kaug = jnp.concatenate([k_ref[rows, :], f_ref[rows, :]], axis=1)
        for hh in range(2):
            if i == j:
                z_sc[slot, hh, 0:half, :] = _dot(kaug[0:half], rhs_sc[i, hh])
                z_sc[slot, hh, half:tk, half:tq] = _dot(kaug[half:tk], rhs_sc[i, hh, :, half:tq])
            else:
                z_sc[slot, hh] = _dot(kaug, rhs_sc[i, hh])

    def values(hh, start, size):
        return jnp.concatenate(
            [v_ref[0, hh * HEAD_DIM:(hh + 1) * HEAD_DIM, start:start + size], ones[:, 0:size]],
            axis=0)

    def consume(i, j, slot):
        for hh in range(2):
            p = jnp.exp2(z_sc[slot, hh]).astype(BF16)
            acc_sc[i, hh] += _dot(values(hh, j * tk, tk), p)

    def consume_diagonal(i, slot):
        outs = []
        for hh in range(2):
            p_top = jnp.exp2(z_sc[slot, hh, 0:half, :] + mask_sc[0:half, :]).astype(BF16)
            p_bot = jnp.exp2(z_sc[slot, hh, half:tk, half:tq]
                             + mask_sc[half:tk, half:tq]).astype(BF16)
            acc = acc_sc[i, hh] + _dot(values(hh, i * tk, half), p_top)
            right = acc[:, half:tq] + _dot(values(hh, i * tk + half, half), p_bot)
            acc = jnp.concatenate([acc[:, 0:half], right], axis=1)
            outs.append(acc[0:HEAD_DIM] / acc[HEAD_DIM:HEAD_DIM + 1])
        o_ref[i * tq:(i + 1) * tq, :] = jnp.concatenate(outs, axis=0).T.astype(o_ref.dtype)

    items = [(i, j) for i in range(1, nq) for j in range(i)] + [(i, i) for i in range(nq)]
    scores(*items[0], 0)
    for s, (i, j) in enumerate(items):
        if s + 1 < len(items):
            scores(*items[s + 1], (s + 1) % 2)
        if i == j:
            consume_diagonal(i, s % 2)
        else:
            consume(i, j, s % 2)


def _fox(qt, k, f3, vt, fq, bound, bsz, seq):
    t = k.shape[0]
    pairs = FOX_HEADS // 2
    out_shape = jax.ShapeDtypeStruct((t, FOX_WIDTH), BF16)
    sem = _cparams(("parallel", "parallel", "arbitrary"))

    def common(tq):
        nq = seq // tq
        in_specs = [pl.BlockSpec((1, LANES, tq), lambda b, p, i: (p, 0, b * nq + i)),
                    pl.BlockSpec((seq, LANES), lambda b, p, i: (b, p)),
                    pl.BlockSpec((seq, LANES), lambda b, p, i: (b, 0)),
                    pl.BlockSpec((1, LANES, seq), lambda b, p, i: (p, 0, b))]
        return nq, in_specs, pl.BlockSpec((tq, LANES), lambda b, p, i: (b * nq + i, p))

    def fixed(brow):
        nq = seq // TQ
        return pl.pallas_call(
            _fox_fixed_kernel,
            grid=(bsz, pairs),
            in_specs=[pl.BlockSpec((1, TQ), lambda b, p: (0, 0)),
                      pl.BlockSpec((1, LANES, seq), lambda b, p: (p, 0, b)),
                      pl.BlockSpec((seq, LANES), lambda b, p: (b, p)),
                      pl.BlockSpec((seq, LANES), lambda b, p: (b, 0)),
                      pl.BlockSpec((1, LANES, seq), lambda b, p: (p, 0, b)),
                      pl.BlockSpec((seq, LANES), lambda b, p: (b, 0))],
            out_specs=pl.BlockSpec((seq, LANES), lambda b, p: (b, p)),
            out_shape=out_shape,
            scratch_shapes=[pltpu.VMEM((nq, 2, 2 * LANES, TQ), BF16),
                            pltpu.VMEM((Z_SLOTS, 2, TQ, TQ), F32),
                            pltpu.VMEM((nq, 2, HEAD_DIM + 16, TQ), F32),
                            pltpu.VMEM((TQ, TQ), F32)],
            compiler_params=_cparams(("parallel", "parallel")), name="fox_attention_fixed",
        )(brow, qt, k, f3, vt, fq)

    def online(brow):
        del brow
        nq, in_specs, out_spec = common(TQ_ONLINE)
        stats = pltpu.VMEM((2, 1, TQ_ONLINE), F32)
        return pl.pallas_call(
            _fox_online_kernel,
            grid=(bsz, pairs, nq),
            in_specs=in_specs, out_specs=out_spec, out_shape=out_shape,
            scratch_shapes=[stats, stats, pltpu.VMEM((2, HEAD_DIM, TQ_ONLINE), F32)],
            compiler_params=sem, name="fox_attention_online",
        )(qt, k, f3, vt)

    brow = jnp.full((1, TQ), bound, F32)
    return lax.cond(bound <= FOX_FAST_BOUND_LOG2, fixed, online, brow)


def _even_out_kernel(a_hbm, o_hbm, x_hbm, w32_ref, gm_ref, wr_ref, br_ref,
                     x1_ref, xn_ref, comb_ref, w_ref, a_buf, o_buf, x_buf, sems):
    i = pl.program_id(0)
    n = pl.num_programs(0)
    tm = x1_ref.shape[0]

    def copies(step, slot):
        rows = pl.ds(pl.multiple_of(step * tm, tm), tm)
        return (pltpu.make_async_copy(a_hbm.at[rows], a_buf.at[slot], sems.at[0, slot]),
                pltpu.make_async_copy(o_hbm.at[rows], o_buf.at[slot], sems.at[1, slot]),
                pltpu.make_async_copy(x_hbm.at[rows], x_buf.at[slot], sems.at[2, slot]))

    @pl.when(i == 0)
    def _():
        for s in range(RING_DEPTH - 1):
            for c in copies(s, s):
                c.start()
        w_ref[...] = w32_ref[...].astype(BF16)

    ahead = i + (RING_DEPTH - 1)

    @pl.when(ahead < n)
    def _():
        for c in copies(ahead, ahead % RING_DEPTH):
            c.start()

    slot = i % RING_DEPTH
    for c in copies(i, slot):
        c.wait()

    y_all = _dot(jnp.concatenate([a_buf[slot], o_buf[slot]], axis=1), w_ref[...])
    sub = tm // SUBTILES
    for s in range(SUBTILES):
        rs = slice(s * sub, (s + 1) * sub)
        x1 = x_buf[slot, rs, :] + y_all[rs]
        x1_ref[rs, :] = x1
        xn = _rms_rows(x1, gm_ref[...]).astype(BF16)
        xn_ref[rs, :] = xn
        comb_ref[rs, :] = _router(xn, wr_ref, br_ref)


def _even_out(a, o, x2, w, gm, wr, br):
    t = x2.shape[0]
    n = t // TM
    full = lambda arr: pl.BlockSpec(arr.shape, lambda i: (0,) * arr.ndim)
    row = lambda wdt: pl.BlockSpec((TM, wdt), lambda i: (i, 0))
    assert n >= RING_DEPTH
    hbm = pl.BlockSpec(memory_space=pl.ANY)
    return pl.pallas_call(
        _even_out_kernel,
        grid=(n,),
        in_specs=[hbm, hbm, hbm,
                  pl.BlockSpec(w.shape, lambda i: (0, 0), pipeline_mode=pl.Buffered(1)),
                  full(gm), full(wr), full(br)],
        out_specs=[row(D_MODEL), row(D_MODEL), row(LANES)],
        out_shape=[jax.ShapeDtypeStruct((t, D_MODEL), F32),
                   jax.ShapeDtypeStruct((t, D_MODEL), BF16),
                   jax.ShapeDtypeStruct((t, LANES), F32)],
        scratch_shapes=[pltpu.VMEM(w.shape, BF16),
                        pltpu.VMEM((RING_DEPTH, TM, GMLP_WIDTH), BF16),
                        pltpu.VMEM((RING_DEPTH, TM, FOX_WIDTH), BF16),
                        pltpu.VMEM((RING_DEPTH, TM, D_MODEL), F32),
                        pltpu.SemaphoreType.DMA((3, RING_DEPTH))],
        compiler_params=_cparams(("arbitrary",)),
        name="even_out",
    )(a, o, x2, w, gm, wr, br)


def _odd_kernel(tiles_per_seq, x_hbm, g_ref, w32_ref, cw_ref, wp_ref, ps_ref, wo32_ref, gm_ref,
                wr_ref, br_ref, x1_ref, xn_ref, comb_ref, w_ref, wo_ref, zbuf, pbuf, x_buf, sems):
    i = pl.program_id(0)
    n = pl.num_programs(0)
    tm = x1_ref.shape[0]
    zpad = zbuf.shape[0] - tm
    ppad = pbuf.shape[0] - tm

    def copy(step, slot):
        rows = pl.ds(pl.multiple_of(step * tm, tm), tm)
        return pltpu.make_async_copy(x_hbm.at[rows], x_buf.at[slot], sems.at[slot])

    @pl.when(i == 0)
    def _():
        for s in range(RING_DEPTH - 1):
            copy(s, s).start()
        w_ref[...] = w32_ref[...].astype(BF16)
        wo_ref[...] = wo32_ref[...].astype(BF16)

    ahead = i + (RING_DEPTH - 1)

    @pl.when(ahead < n)
    def _():
        copy(ahead, ahead % RING_DEPTH).start()

    slot = i % RING_DEPTH
    copy(i, slot).wait()
    x_ref = x_buf.at[slot]

    @pl.when(i % tiles_per_seq == 0)
    def _():
        zbuf[0:zpad, :] = jnp.zeros((zpad, CONV_WIDTH), F32)
        pbuf[0:ppad, :] = jnp.zeros((ppad, POOL_WIDTH), F32)

    cw = cw_ref[...]
    ps = ps_ref[...]
    sub = tm // SUBTILES
    xn_all = _rms_rows(x_ref[...], g_ref[...]).astype(BF16)
    cg_all = _dot(xn_all, w_ref[:, 512:1024])
    hc_all = _dot(xn_all, w_ref[:, 1024:1536])
    p_all = _dot(xn_all, w_ref[:, 1536:2048])
    bg_all = _dot(xn_all, w_ref[:, 0:512])
    for s in range(SUBTILES):
        r0 = s * sub
        rs = slice(r0, r0 + sub)
        x = x_ref[rs, :]
        cg, hc, p, bg = cg_all[rs], hc_all[rs], p_all[rs], bg_all[rs]

        z = cg * hc
        zbuf[zpad + r0:zpad + r0 + sub, :] = z
        y = (cw[0:1, :] * zbuf[zpad + r0 - 2:zpad + r0 - 2 + sub, :]
             + cw[1:2, :] * zbuf[zpad + r0 - 1:zpad + r0 - 1 + sub, :]
             + cw[2:3, :] * z)
        c = bg * y

        pbuf[ppad + r0:ppad + r0 + sub, :] = p
        pos = ((i % tiles_per_seq) * tm + r0
               + lax.broadcasted_iota(jnp.int32, (sub, POOL_GROUP), 0)).astype(F32) + 1.0
        pooled_out = []
        for gi, win in enumerate(POOL_WINDOWS):
            ls = slice(gi * POOL_GROUP, (gi + 1) * POOL_GROUP)
            pg = p[:, ls]
            sm = pbuf[r0:r0 + ppad + sub, ls]
            span = 1
            while span < win:
                sm = sm + pltpu.roll(sm, span, axis=0)
                span *= 2
            sm = sm[ppad:]
            cnt = jnp.minimum(pos, float(win))
            pooled = sm / cnt - pg
            pooled_out.append((_dot(pooled.astype(BF16), wp_ref[gi]) * ps[:, ls]).astype(BF16))
        mix = jnp.concatenate([c.astype(BF16)] + pooled_out, axis=1)
        acc = x + _dot(mix, wo_ref[...])

        x1_ref[rs, :] = acc
        xn2 = _rms_rows(acc, gm_ref[...]).astype(BF16)
        xn_ref[rs, :] = xn2
        comb_ref[rs, :] = _router(xn2, wr_ref, br_ref)

    zbuf[0:zpad, :] = zbuf[tm:tm + zpad, :]
    pbuf[0:ppad, :] = pbuf[tm:tm + ppad, :]


def _odd(x2, g, w, cw, wp, ps, wo, gm, wr, br, seq):
    t = x2.shape[0]
    n = t // TM
    full = lambda arr: pl.BlockSpec(arr.shape, lambda i: (0,) * arr.ndim)
    once = lambda arr: pl.BlockSpec(arr.shape, lambda i: (0,) * arr.ndim,
                                    pipeline_mode=pl.Buffered(1))
    row = lambda wdt: pl.BlockSpec((TM, wdt), lambda i: (i, 0))
    return pl.pallas_call(
        functools.partial(_odd_kernel, seq // TM),
        grid=(n,),
        in_specs=[pl.BlockSpec(memory_space=pl.ANY), full(g), once(w), full(cw), full(wp),
                  full(ps), once(wo), full(gm), full(wr), full(br)],
        out_specs=[row(D_MODEL), row(D_MODEL), row(LANES)],
        out_shape=[jax.ShapeDtypeStruct((t, D_MODEL), F32),
                   jax.ShapeDtypeStruct((t, D_MODEL), BF16),
                   jax.ShapeDtypeStruct((t, LANES), F32)],
        scratch_shapes=[pltpu.VMEM(w.shape, BF16), pltpu.VMEM(wo.shape, BF16),
                        pltpu.VMEM((TM + 8, CONV_WIDTH), F32),
                        pltpu.VMEM((TM + 16, POOL_WIDTH), F32),
                        pltpu.VMEM((RING_DEPTH, TM, D_MODEL), F32),
                        pltpu.SemaphoreType.DMA((RING_DEPTH,))],
        compiler_params=_cparams(("arbitrary",)),
        name="odd_mixer",
    )(x2, g, w, cw, wp, ps, wo, gm, wr, br)


def _moe_kernel(xn_ref, wg_ref, wu_ref, wd_ref, comb_ref, x1_ref, o_ref):
    g = pl.program_id(1)

    @pl.when(g == 0)
    def _():
        o_ref[...] = x1_ref[...]

    tm = xn_ref.shape[0]
    xn = xn_ref[...]
    lane = lax.broadcasted_iota(jnp.int32, (tm, LANES), 1)
    comb = comb_ref[...]
    hid = []
    for e in range(EXPERTS_PER_GROUP):
        gate = _dot(xn, wg_ref[e].astype(BF16))
        up = _dot(xn, wu_ref[e].astype(BF16))
        cw = jnp.sum(jnp.where(lane == g * EXPERTS_PER_GROUP + e, comb, 0.0), axis=1,
                     keepdims=True)
        hid.append(((gate * jax.nn.sigmoid(gate)) * up * cw).astype(BF16))
    o_ref[...] += _dot(jnp.concatenate(hid, axis=1), wd_ref[0].astype(BF16))


def _moe(xn, wg, wu, wd, comb, x1, layer):
    t = xn.shape[0]
    n = t // TM_MOE
    gh = EXPERTS_PER_GROUP * EXPERT_HIDDEN
    wblk = pl.BlockSpec((None, EXPERTS_PER_GROUP, D_MODEL, EXPERT_HIDDEN),
                        lambda i, g: (layer, g, 0, 0))
    return pl.pallas_call(
        _moe_kernel,
        grid=(n, N_GROUPS),
        in_specs=[pl.BlockSpec((TM_MOE, D_MODEL), lambda i, g: (i, 0)),
                  wblk, wblk,
                  pl.BlockSpec((None, 1, gh, D_MODEL), lambda i, g: (layer, g, 0, 0)),
                  pl.BlockSpec((TM_MOE, LANES), lambda i, g: (i, 0)),
                  pl.BlockSpec((TM_MOE, D_MODEL), lambda i, g: (i, 0))],
        out_specs=pl.BlockSpec((TM_MOE, D_MODEL), lambda i, g: (i, 0)),
        out_shape=jax.ShapeDtypeStruct((t, D_MODEL), F32),
        compiler_params=_cparams(("parallel", "arbitrary")),
        name="moe",
    )(xn, wg, wu, wd, comb, x1)


def _router_params(w_group, b_group, w_router, b_router):
    wr = jnp.concatenate([w_router.reshape(D_MODEL, N_EXPERTS), w_group], axis=1)
    wr = jnp.pad(wr, ((0, 0), (0, LANES - wr.shape[1]))).astype(BF16)
    br = jnp.concatenate([b_router.reshape(N_EXPERTS), b_group])
    br = jnp.pad(br, (0, LANES - br.shape[0])).reshape(1, LANES).astype(F32)
    return wr, br


def _moe_params(w_gate, w_up, w_down):
    wd = w_down.reshape(w_down.shape[0], N_GROUPS, EXPERTS_PER_GROUP * EXPERT_HIDDEN, D_MODEL)
    return w_gate, w_up, wd


def kernel(x, ev_norm, ev_w_in, ev_b_forget, ev_w_s, ev_b_s, ev_g_v, ev_g_q, ev_g_k, ev_w_out,
           od_norm, od_w_in, od_conv_w, od_w_pool, od_pool_scale, od_w_out,
           moe_norm, moe_w_group, moe_b_group, moe_w_router, moe_b_router,
           moe_w_gate, moe_w_up, moe_w_down):
    bsz, seq, d = x.shape
    t = bsz * seq
    x2 = x.reshape(t, d)

    w_in_t = ev_w_in[0].T
    w_f = w_in_t[EVEN_MAIN:].T
    w_f3 = jnp.pad(jnp.tile(w_f, (1, 3)), ((0, 0), (0, LANES - 3 * FOX_HEADS))).astype(BF16)
    bf = jnp.pad(jnp.tile(ev_b_forget[0], 3), (0, LANES - 3 * FOX_HEADS)).reshape(1, LANES)
    tril = jnp.tril(jnp.ones((CHUNK, CHUNK), F32))
    wtril = (ev_w_s[0] * tril).astype(BF16)
    bmat = jnp.repeat(ev_b_s[0].T, GMLP_HEAD, axis=1)
    gv = ev_g_v[0].reshape(1, GMLP_WIDTH)
    gq = jnp.broadcast_to((jnp.tile(ev_g_q[0], 2) * (HEAD_DIM ** -0.5 * LOG2E))[:, None],
                          (LANES, TM // SUBTILES))
    gk = jnp.tile(ev_g_k[0], FOX_HEADS).reshape(1, FOX_WIDTH)
    blk = jnp.arange(MXU_DIM) // HEAD_DIM
    bd = jnp.where(blk[:, None] == blk[None, :], 1.0 / HEAD_DIM, 0.0).astype(BF16)
    ltri = jnp.tril(jnp.ones((TM // SUBTILES, TM // SUBTILES), F32)).astype(BF16)

    a, qt, kn, vt, f3, fq = _even_in(x2, ev_norm[0].reshape(1, d), w_in_t, w_f3, bf, wtril,
                                     bmat, gv, gq, gk, bd, ltri, seq)
    bound = (HEAD_DIM ** 0.5 * LOG2E * 1.01) * jnp.max(jnp.abs(ev_g_q[0])) * jnp.max(
        jnp.abs(ev_g_k[0]))
    o = _fox(qt, kn, f3, vt, fq, bound, bsz, seq)

    wr0, br0 = _router_params(moe_w_group[0], moe_b_group[0], moe_w_router[0], moe_b_router[0])
    x1, xn1, comb1 = _even_out(a, o, x2, ev_w_out[0],
                               moe_norm[0].reshape(1, d), wr0, br0)
    moe_w = _moe_params(moe_w_gate, moe_w_up, moe_w_down)
    xa = _moe(xn1, *moe_w, comb1, x1, 0)

    wr1, br1 = _router_params(moe_w_group[1], moe_b_group[1], moe_w_router[1], moe_b_router[1])
    x3, xn3, comb3 = _odd(xa, od_norm[0].reshape(1, d), od_w_in[0], od_conv_w[0],
                          od_w_pool[0].astype(BF16), od_pool_scale[0].reshape(1, POOL_WIDTH),
                          od_w_out[0], moe_norm[1].reshape(1, d), wr1, br1, seq)
    xb = _moe(xn3, *moe_w, comb3, x3, 1)
    return xb.reshape(bsz, seq, d)
```
